```python
import math
import jax, jax.numpy as jnp
from jax import lax
import numpy as np

D_MODEL = 1024
BATCH = 4
SEQ = 4096
DEPTH = 2
DEC_BATCH = 128
DEC_SEQ = 4
PAST_LEN = 2048
PAGE_SIZE = 128

N_META = 16
D_FF = 4 * D_MODEL
LN_EPS = 1e-5
DEEPNORM_ALPHA = (2 * DEPTH) ** 0.25
DEEPNORM_BETA = (8 * DEPTH) ** -0.25
N_GDN_LAYERS = (DEPTH + 1) // 2
N_DSA_LAYERS = DEPTH // 2

GDN_K_HEADS = 8
GDN_V_HEADS = 16
GDN_HEAD_K = 128
GDN_HEAD_V = 128
GDN_KEY_DIM = GDN_K_HEADS * GDN_HEAD_K
GDN_VALUE_DIM = GDN_V_HEADS * GDN_HEAD_V
GDN_CONV_DIM = 2 * GDN_KEY_DIM + GDN_VALUE_DIM
GDN_CONV_WIDTH = 4
GDN_CHUNK = 64
GDN_IN_DIM = GDN_CONV_DIM + GDN_VALUE_DIM + 2 * GDN_V_HEADS
L2_EPS = 1e-6
RMS_EPS = 1e-6

ATT_HEADS = 8
ATT_KV_HEADS = 2
ATT_HEAD_DIM = 128
IDX_HEADS = 8
IDX_HEAD_DIM = 64
TOPK_MAX = 256
Q_BLOCK = 128
ROPE_THETA = 500000.0
ATT_ROT_DIM = ATT_HEAD_DIM // 4
IDX_ROT_DIM = IDX_HEAD_DIM // 4
DSA_SIZES = (ATT_HEADS * ATT_HEAD_DIM, ATT_KV_HEADS * ATT_HEAD_DIM, ATT_KV_HEADS * ATT_HEAD_DIM,
             IDX_HEADS * IDX_HEAD_DIM, IDX_HEAD_DIM, IDX_HEADS)
DSA_IN_DIM = sum(DSA_SIZES)

kernel_name = "hybrid_gdn_dsa_decoder_step"

F32 = jnp.float32


def _layernorm(x, g, b):
    xf = x.astype(F32)
    mu = jnp.mean(xf, -1, keepdims=True)
    var = jnp.mean(jnp.square(xf - mu), -1, keepdims=True)
    return ((xf - mu) * lax.rsqrt(var + LN_EPS) * g.astype(F32) + b.astype(F32)).astype(x.dtype)


def _l2norm(x):
    xf = x.astype(F32)
    return xf * lax.rsqrt(jnp.sum(xf * xf, -1, keepdims=True) + L2_EPS)


def _rope_partial(x, pos, rot_dim):
    half = rot_dim // 2
    inv_freq = ROPE_THETA ** (-jnp.arange(half, dtype=F32) * 2.0 / rot_dim)
    ang = pos.astype(F32)[:, None] * inv_freq[None, :]
    cos = jnp.cos(ang)[:, None, :]
    sin = jnp.sin(ang)[:, None, :]
    xf = x.astype(F32)
    x1 = xf[..., :half]
    x2 = xf[..., half:rot_dim]
    out = jnp.concatenate([x1 * cos - x2 * sin, x2 * cos + x1 * sin, xf[..., rot_dim:]], -1)
    return out.astype(x.dtype)


def _sqrelu_mlp(x, w1, w2):
    return jnp.square(jax.nn.relu(x @ w1)) @ w2


def _gdn_project(x, w_in):
    proj = x @ w_in
    c0 = GDN_CONV_DIM
    c1 = c0 + GDN_VALUE_DIM
    mixed = proj[..., :c0]
    z = proj[..., c0:c1]
    b = proj[..., c1:c1 + GDN_V_HEADS]
    a = proj[..., c1 + GDN_V_HEADS:]
    return mixed, z, b, a


def _causal_conv_silu(xp, conv_w, T):
    out = xp[:, 0:T] * conv_w[0]
    for j in range(1, GDN_CONV_WIDTH):
        out = out + xp[:, j:j + T] * conv_w[j]
    return jax.nn.silu(out)


def _gdn_heads(mixed_c, b, a, a_log, dt_bias):
    B, T, _ = mixed_c.shape
    rep = GDN_V_HEADS // GDN_K_HEADS
    q = mixed_c[..., :GDN_KEY_DIM].reshape(B, T, GDN_K_HEADS, GDN_HEAD_K)
    k = mixed_c[..., GDN_KEY_DIM:2 * GDN_KEY_DIM].reshape(B, T, GDN_K_HEADS, GDN_HEAD_K)
    v = mixed_c[..., 2 * GDN_KEY_DIM:].reshape(B, T, GDN_V_HEADS, GDN_HEAD_V).astype(F32)
    q = jnp.repeat(_l2norm(q), rep, axis=2) * (GDN_HEAD_K ** -0.5)
    k = jnp.repeat(_l2norm(k), rep, axis=2)
    beta = jax.nn.sigmoid(b.astype(F32))
    g = -jnp.exp(a_log.astype(F32)) * jax.nn.softplus(a.astype(F32) + dt_bias.astype(F32))
    return q, k, v, beta, g


def _gdn_chunked(q, k, v, beta, g, s0):
    B, T, H, _ = q.shape
    dv = v.shape[-1]
    C = GDN_CHUNK
    N = T // C

    def to_chunks(t):
        return jnp.moveaxis(t.reshape((B, N, C) + t.shape[2:]), 3, 1)

    qc, kc, vc, bc = to_chunks(q), to_chunks(k), to_chunks(v), to_chunks(beta)
    gc = jnp.cumsum(to_chunks(g), axis=-1)
    tril = jnp.tril(jnp.ones((C, C), bool))
    strict = jnp.tril(jnp.ones((C, C), bool), -1)
    diff = gc[..., :, None] - gc[..., None, :]
    decay = jnp.where(tril, jnp.exp(jnp.where(tril, diff, 0.0)), 0.0)
    kk = jnp.einsum('bhncd,bhnsd->bhncs', kc, kc)
    a_mat = jnp.where(strict, kk * decay * bc[..., :, None], 0.0)
    eye = jnp.eye(C, dtype=F32)
    rhs = jnp.concatenate([vc * bc[..., None], kc * (bc * jnp.exp(gc))[..., None]], -1)
    sol = lax.linalg.triangular_solve(eye + a_mat, rhs, left_side=True, lower=True,
                                      unit_diagonal=True)
    u = sol[..., :dv]
    w = sol[..., dv:]
    qk = jnp.einsum('bhncd,bhnsd->bhncs', qc, kc) * decay
    qg = qc * jnp.exp(gc)[..., None]
    glast = gc[..., -1]
    kd = kc * jnp.exp(glast[..., None] - gc)[..., None]

    def step(S, xs):
        u_i, w_i, qg_i, qk_i, kd_i, gl_i = xs
        v_new = u_i - jnp.einsum('bhck,bhkv->bhcv', w_i, S)
        o = jnp.einsum('bhck,bhkv->bhcv', qg_i, S) + jnp.einsum('bhcs,bhsv->bhcv', qk_i, v_new)
        S = S * jnp.exp(gl_i)[..., None, None] + jnp.einsum('bhck,bhcv->bhkv', kd_i, v_new)
        return S, o

    xs = tuple(jnp.moveaxis(t, 2, 0) for t in (u, w, qg, qk, kd, glast))
    s_final, o = lax.scan(step, s0, xs)
    o = jnp.moveaxis(o, 0, 2).reshape(B, H, T, dv).transpose(0, 2, 1, 3)
    return o, s_final


def _gdn_output(o, z, norm_w, w_out, dtype):
    B, T = o.shape[:2]
    on = o * lax.rsqrt(jnp.mean(o * o, -1, keepdims=True) + RMS_EPS) * norm_w.astype(F32)
    zf = z.astype(F32).reshape(B, T, GDN_V_HEADS, GDN_HEAD_V)
    out = (on * jax.nn.silu(zf)).reshape(B, T, GDN_VALUE_DIM).astype(dtype)
    return out @ w_out


def _gdn_prompt(x, w_in, conv_w, a_log, dt_bias, norm_w, w_out):
    B, L, _ = x.shape
    mixed, z, b, a = _gdn_project(x, w_in)
    xp = jnp.concatenate([jnp.zeros((B, GDN_CONV_WIDTH - 1, GDN_CONV_DIM), mixed.dtype), mixed], 1)
    conv_tail = xp[:, -(GDN_CONV_WIDTH - 1):]
    q, k, v, beta, g = _gdn_heads(_causal_conv_silu(xp, conv_w, L), b, a, a_log, dt_bias)
    n_pad = (-N_META) % GDN_CHUNK
    n_tail = (-(n_pad + L)) % GDN_CHUNK

    def padt(t):
        return jnp.pad(t, ((0, 0), (n_pad, n_tail)) + ((0, 0),) * (t.ndim - 2))

    s0 = jnp.zeros((B, GDN_V_HEADS, GDN_HEAD_K, GDN_HEAD_V), F32)
    o, s_final = _gdn_chunked(padt(q), padt(k), padt(v), padt(beta), padt(g), s0)
    o = o[:, n_pad:n_pad + L]
    return _gdn_output(o, z, norm_w, w_out, x.dtype), s_final, conv_tail


def _gdn_sample(x, state, conv_state, w_in, conv_w, a_log, dt_bias, norm_w, w_out):
    B, T, _ = x.shape
    mixed, z, b, a = _gdn_project(x, w_in)
    xp = jnp.concatenate([conv_state.astype(mixed.dtype), mixed], 1)
    conv_new = xp[:, -(GDN_CONV_WIDTH - 1):]
    q, k, v, beta, g = _gdn_heads(_causal_conv_silu(xp, conv_w, T), b, a, a_log, dt_bias)

    def step(S, xs):
        q_t, k_t, v_t, b_t, g_t = xs
        S = S * jnp.exp(g_t)[..., None, None]
        delta = (v_t - jnp.einsum('bhk,bhkv->bhv', k_t, S)) * b_t[..., None]
        S = S + jnp.einsum('bhk,bhv->bhkv', k_t, delta)
        return S, jnp.einsum('bhk,bhkv->bhv', q_t, S)

    xs = tuple(jnp.moveaxis(t, 1, 0) for t in (q, k, v, beta, g))
    s_new, o = lax.scan(step, state.astype(F32), xs)
    o = jnp.moveaxis(o, 0, 1)
    return _gdn_output(o, z, norm_w, w_out, x.dtype), s_new, conv_new


def _dsa_project(x, pos, w_in, ik_g, ik_b):
    B, T, _ = x.shape
    splits = np.cumsum(DSA_SIZES)[:-1].tolist()
    q, k, v, iq, ik, iw = jnp.split(x @ w_in, splits, axis=-1)
    q = _rope_partial(q.reshape(B, T, ATT_HEADS, ATT_HEAD_DIM), pos, ATT_ROT_DIM)
    k = _rope_partial(k.reshape(B, T, ATT_KV_HEADS, ATT_HEAD_DIM), pos, ATT_ROT_DIM)
    v = v.reshape(B, T, ATT_KV_HEADS, ATT_HEAD_DIM)
    iq = _rope_partial(iq.reshape(B, T, IDX_HEADS, IDX_HEAD_DIM), pos, IDX_ROT_DIM)
    ik = _rope_partial(_layernorm(ik, ik_g, ik_b)[:, :, None, :], pos, IDX_ROT_DIM)[:, :, 0, :]
    iw = iw * (IDX_HEADS ** -0.5)
    return q, k, v, iq, ik, iw


def _index_topk(iq, iw, qpos, ik, topk):
    dots = jnp.einsum('bthd,bsd->bths', iq.astype(F32), ik.astype(F32))
    score = jnp.einsum('bth,bths->bts', iw.astype(F32), jax.nn.relu(dots)) * (IDX_HEAD_DIM ** -0.5)
    kpos = jnp.arange(ik.shape[1], dtype=jnp.int32)
    score = jnp.where(kpos[None, None, :] < N_META, jnp.inf, score)
    score = jnp.where(kpos[None, None, :] <= qpos[None, :, None], score, -jnp.inf)
    _, sel = lax.top_k(score, topk)
    valid = sel <= qpos[None, :, None]
    return sel, valid


def _sparse_attend(q, kg, vg, valid):
    B, T, H, Dh = q.shape
    G = H // ATT_KV_HEADS
    qg = q.reshape(B, T, ATT_KV_HEADS, G, Dh).astype(F32)
    s = jnp.einsum('bthgd,btkhd->bthgk', qg, kg.astype(F32)) * (Dh ** -0.5)
    s = jnp.where(valid[:, :, None, None, :], s, -jnp.inf)
    p = jax.nn.softmax(s, axis=-1)
    o = jnp.einsum('bthgk,btkhd->bthgd', p, vg.astype(F32))
    return o.reshape(B, T, H * Dh)


def _gather_rows(rows, sel):
    return jax.vmap(lambda r, s: r[s])(rows, sel)


def _dsa_prompt(x, w_in, ik_g, ik_b, w_o):
    B, L, _ = x.shape
    pos = jnp.arange(L, dtype=jnp.int32)
    q, k, v, iq, ik, iw = _dsa_project(x, pos, w_in, ik_g, ik_b)
    topk = min(TOPK_MAX, (L - N_META) // 4)
    n_blk = -(-L // Q_BLOCK)
    pad = n_blk * Q_BLOCK - L

    def padq(t):
        return jnp.pad(t, ((0, 0), (0, pad)) + ((0, 0),) * (t.ndim - 2))

    qp, iqp, iwp = padq(q), padq(iq), padq(iw)

    def block(i):
        s0 = i * Q_BLOCK
        qb = lax.dynamic_slice_in_dim(qp, s0, Q_BLOCK, axis=1)
        iqb = lax.dynamic_slice_in_dim(iqp, s0, Q_BLOCK, axis=1)
        iwb = lax.dynamic_slice_in_dim(iwp, s0, Q_BLOCK, axis=1)
        qpos = s0 + jnp.arange(Q_BLOCK, dtype=jnp.int32)
        sel, valid = _index_topk(iqb, iwb, qpos, ik, topk)
        return _sparse_attend(qb, _gather_rows(k, sel), _gather_rows(v, sel), valid)

    o = lax.map(block, jnp.arange(n_blk, dtype=jnp.int32))
    o = jnp.moveaxis(o, 0, 1).reshape(B, n_blk * Q_BLOCK, -1)[:, :L]
    return o.astype(x.dtype) @ w_o, k, v, ik


def _dsa_sample(x, cache_k, cache_v, cache_ik, page_table, w_in, ik_g, ik_b, w_o):
    B, T, _ = x.shape
    past = page_table.shape[1] * PAGE_SIZE
    pos = past + jnp.arange(T, dtype=jnp.int32)
    q, k, v, iq, ik, iw = _dsa_project(x, pos, w_in, ik_g, ik_b)
    ik_past = cache_ik[page_table].reshape(B, past, IDX_HEAD_DIM)
    ik_all = jnp.concatenate([ik_past.astype(ik.dtype), ik], 1)
    topk = min(TOPK_MAX, (past + T) // 4)
    sel, valid = _index_topk(iq, iw, pos, ik_all, topk)
    in_past = sel < past
    sp = jnp.minimum(sel, past - 1)
    phys = jax.vmap(lambda pt, s: pt[s])(page_table, sp // PAGE_SIZE)
    off = sp % PAGE_SIZE
    sn = jnp.clip(sel - past, 0, T - 1)

    def gather(pool, new):
        rows_past = pool[phys, off]
        rows_new = _gather_rows(new, sn)
        return jnp.where(in_past[..., None, None], rows_past.astype(new.dtype), rows_new)

    o = _sparse_attend(q, gather(cache_k, k), gather(cache_v, v), valid)
    return o.astype(x.dtype) @ w_o, k, v, ik


def setup_inputs(seed: int = 0) -> dict:
    key = jax.random.key(seed)
    ks = jax.random.split(key, 26)

    def nrm(kk, shape, scale=1.0):
        return jax.random.normal(kk, shape, F32) * scale

    n_pages = PAST_LEN // PAGE_SIZE
    n_used = DEC_BATCH * n_pages
    n_pool = n_used + (n_used + 3) // 4
    page_table = jax.random.permutation(ks[0], n_pool)[:n_used].reshape(DEC_BATCH, n_pages).astype(jnp.int32)
    ng, nd = N_GDN_LAYERS, N_DSA_LAYERS
    dt = jnp.exp(jax.random.uniform(ks[1], (ng, GDN_V_HEADS), F32, math.log(1e-3), math.log(1e-1)))
    return {
        "x_prompt": nrm(ks[2], (BATCH, SEQ, D_MODEL)),
        "x_sample": nrm(ks[3], (DEC_BATCH, DEC_SEQ, D_MODEL)),
        "state_gdn": nrm(ks[4], (ng, DEC_BATCH, GDN_V_HEADS, GDN_HEAD_K, GDN_HEAD_V), 0.2),
        "state_gdn_conv": nrm(ks[5], (ng, DEC_BATCH, GDN_CONV_WIDTH - 1, GDN_CONV_DIM)),
        "cache_k": nrm(ks[6], (nd, n_pool, PAGE_SIZE, ATT_KV_HEADS, ATT_HEAD_DIM)),
        "cache_v": nrm(ks[7], (nd, n_pool, PAGE_SIZE, ATT_KV_HEADS, ATT_HEAD_DIM)),
        "cache_idx_k": nrm(ks[8], (nd, n_pool, PAGE_SIZE, IDX_HEAD_DIM)),
        "page_table": page_table,
        "meta_tokens": nrm(ks[9], (N_META, D_MODEL)),
        "ln1_g": 1.0 + nrm(ks[10], (DEPTH, D_MODEL), 0.05),
        "ln1_b": nrm(ks[11], (DEPTH, D_MODEL), 0.02),
        "ln2_g": 1.0 + nrm(ks[12], (DEPTH, D_MODEL), 0.05),
        "ln2_b": nrm(ks[13], (DEPTH, D_MODEL), 0.02),
        "mlp_w1": nrm(ks[14], (DEPTH, D_MODEL, D_FF), D_MODEL ** -0.5),
        "mlp_w2": nrm(ks[15], (DEPTH, D_FF, D_MODEL), DEEPNORM_BETA * D_FF ** -0.5),
        "gdn_w_in": nrm(ks[16], (ng, D_MODEL, GDN_IN_DIM), D_MODEL ** -0.5),
        "gdn_conv_w": nrm(ks[17], (ng, GDN_CONV_WIDTH, GDN_CONV_DIM), GDN_CONV_WIDTH ** -0.5),
        "gdn_a_log": jnp.log(jax.random.uniform(ks[18], (ng, GDN_V_HEADS), F32, 1.0, 16.0)),
        "gdn_dt_bias": dt + jnp.log(-jnp.expm1(-dt)),
        "gdn_norm_w": 1.0 + nrm(ks[19], (ng, GDN_HEAD_V), 0.05),
        "gdn_w_out": nrm(ks[20], (ng, GDN_VALUE_DIM, D_MODEL), DEEPNORM_BETA * GDN_VALUE_DIM ** -0.5),
        "dsa_w_in": nrm(ks[21], (nd, D_MODEL, DSA_IN_DIM), D_MODEL ** -0.5),
        "dsa_ik_norm_g": 1.0 + nrm(ks[22], (nd, IDX_HEAD_DIM), 0.05),
        "dsa_ik_norm_b": nrm(ks[23], (nd, IDX_HEAD_DIM), 0.02),
        "dsa_w_o": nrm(ks[24], (nd, ATT_HEADS * ATT_HEAD_DIM, D_MODEL),
                       DEEPNORM_BETA * (ATT_HEADS * ATT_HEAD_DIM) ** -0.5),
    }


def reference(x_prompt, x_sample, state_gdn, state_gdn_conv, cache_k, cache_v, cache_idx_k, page_table,
              meta_tokens, ln1_g, ln1_b, ln2_g, ln2_b, mlp_w1, mlp_w2,
              gdn_w_in, gdn_conv_w, gdn_a_log, gdn_dt_bias, gdn_norm_w, gdn_w_out,
              dsa_w_in, dsa_ik_norm_g, dsa_ik_norm_b, dsa_w_o):
    B = x_prompt.shape[0]
    meta = jnp.broadcast_to(meta_tokens[None].astype(x_prompt.dtype), (B, N_META, D_MODEL))
    hp = jnp.concatenate([meta, x_prompt], 1)
    hs = x_sample
    gsp, gcp, gss, gcs = [], [], [], []
    kp, vp, ikp, ksm, vsm, iks = [], [], [], [], [], []
    for i in range(DEPTH):
        j = i // 2
        if i % 2 == 0:
            mp, s_p, c_p = _gdn_prompt(hp, gdn_w_in[j], gdn_conv_w[j], gdn_a_log[j], gdn_dt_bias[j],
                                       gdn_norm_w[j], gdn_w_out[j])
            ms, s_s, c_s = _gdn_sample(hs, state_gdn[j], state_gdn_conv[j], gdn_w_in[j], gdn_conv_w[j],
                                       gdn_a_log[j], gdn_dt_bias[j], gdn_norm_w[j], gdn_w_out[j])
            gsp.append(s_p.astype(state_gdn.dtype))
            gcp.append(c_p.astype(state_gdn_conv.dtype))
            gss.append(s_s.astype(state_gdn.dtype))
            gcs.append(c_s.astype(state_gdn_conv.dtype))
        else:
            mp, k_p, v_p, ik_p = _dsa_prompt(hp, dsa_w_in[j], dsa_ik_norm_g[j], dsa_ik_norm_b[j], dsa_w_o[j])
            ms, k_s, v_s, ik_s = _dsa_sample(hs, cache_k[j], cache_v[j], cache_idx_k[j], page_table,
                                             dsa_w_in[j], dsa_ik_norm_g[j], dsa_ik_norm_b[j], dsa_w_o[j])
            kp.append(k_p)
            vp.append(v_p)
            ikp.append(ik_p)
            ksm.append(k_s)
            vsm.append(v_s)
            iks.append(ik_s)
        hp = _layernorm(DEEPNORM_ALPHA * hp + mp, ln1_g[i], ln1_b[i])
        hs = _layernorm(DEEPNORM_ALPHA * hs + ms, ln1_g[i], ln1_b[i])
        hp = _layernorm(DEEPNORM_ALPHA * hp + _sqrelu_mlp(hp, mlp_w1[i], mlp_w2[i]), ln2_g[i], ln2_b[i])
        hs = _layernorm(DEEPNORM_ALPHA * hs + _sqrelu_mlp(hs, mlp_w1[i], mlp_w2[i]), ln2_g[i], ln2_b[i])
    y_prompt = hp[:, N_META:]
    return (y_prompt, hs, jnp.stack(gsp), jnp.stack(gcp), jnp.stack(gss), jnp.stack(gcs),
            jnp.stack(kp), jnp.stack(vp), jnp.stack(ikp), jnp.stack(ksm), jnp.stack(vsm), jnp.stack(iks))
```

```python
import functools
import math

import jax
import jax.numpy as jnp
import numpy as np
from jax import lax
from jax.experimental import pallas as pl
from jax.experimental.pallas import tpu as pltpu

F32 = jnp.float32
BF16 = jnp.bfloat16
HIGHEST = lax.Precision.HIGHEST

D_MODEL = 1024
N_META = 16
DEPTH = 2
LN_EPS = 1e-5
DEEPNORM_ALPHA = (2 * DEPTH) ** 0.25
GDN_K_HEADS = 8
GDN_V_HEADS = 16
GDN_HEAD = 128
GDN_KEY_DIM = GDN_K_HEADS * GDN_HEAD
GDN_VALUE_DIM = GDN_V_HEADS * GDN_HEAD
GDN_CONV_DIM = 2 * GDN_KEY_DIM + GDN_VALUE_DIM
GDN_CONV_WIDTH = 4
GDN_CHUNK = 64
L2_EPS = 1e-6
RMS_EPS = 1e-6
ATT_HEADS = 8
ATT_KV_HEADS = 2
ATT_HEAD_DIM = 128
ATT_GROUP = ATT_HEADS // ATT_KV_HEADS
IDX_HEADS = 8
IDX_HEAD_DIM = 64
TOPK_MAX = 256
ROPE_THETA = 500000.0
ATT_ROT_HALF = ATT_HEAD_DIM // 8
IDX_ROT_HALF = IDX_HEAD_DIM // 8
PAGE = 128

LANES = 128
SUBLANES = 8
ROW_TILE = 512
KEY_TILE = 128
VMEM_LIMIT = 56 * 1024 * 1024

NEG_INF = float("-inf")
F32_MAX = float(np.finfo(np.float32).max)


def _cparams(sem):
    return pltpu.CompilerParams(dimension_semantics=sem, vmem_limit_bytes=VMEM_LIMIT)


def _dot(a, b):
    return jnp.dot(a, b, preferred_element_type=F32)


def _dot_nt(a, b, precision=None):
    return lax.dot_general(a, b, (((1,), (1,)), ((), ())), preferred_element_type=F32,
                           precision=precision)


def _dot_tn(a, b):
    return lax.dot_general(a, b, (((0,), (0,)), ((), ())), preferred_element_type=F32)


def _layernorm_rows(x, g, b):
    mu = jnp.mean(x, axis=-1, keepdims=True)
    xc = x - mu
    var = jnp.mean(xc * xc, axis=-1, keepdims=True)
    return xc * lax.rsqrt(var + LN_EPS) * g + b


def _sigmoid(x):
    return 1.0 / (1.0 + jnp.exp(-x))


def _silu(x):
    return x * _sigmoid(x)


def _softplus(x):
    return jnp.maximum(x, 0.0) + jnp.log(1.0 + jnp.exp(-jnp.abs(x)))


def _matmul_kernel(x_ref, w_ref, o_ref):
    o_ref[...] = _dot(x_ref[...].astype(BF16), w_ref[...]).astype(o_ref.dtype)


def _matmul(x, w, tn):
    rows, k = x.shape
    n = w.shape[1]
    return pl.pallas_call(
        _matmul_kernel,
        grid=(rows // ROW_TILE, n // tn),
        in_specs=[pl.BlockSpec((ROW_TILE, k), lambda i, j: (i, 0)),
                  pl.BlockSpec((k, tn), lambda i, j: (0, j))],
        out_specs=pl.BlockSpec((ROW_TILE, tn), lambda i, j: (i, j)),
        out_shape=jax.ShapeDtypeStruct((rows, n), F32),
        compiler_params=_cparams(("parallel", "arbitrary")),
        name="matmul",
    )(x, w)


def _outproj_ln_kernel(a_ref, x_ref, w_ref, g_ref, b_ref, o_ref):
    y = DEEPNORM_ALPHA * x_ref[...] + _dot(a_ref[...], w_ref[...])
    o_ref[...] = _layernorm_rows(y, g_ref[...], b_ref[...])


def _outproj_ln(a, x, w, g, b):
    rows, k = a.shape
    d = x.shape[1]
    return pl.pallas_call(
        _outproj_ln_kernel,
        grid=(rows // ROW_TILE,),
        in_specs=[pl.BlockSpec((ROW_TILE, k), lambda i: (i, 0)),
                  pl.BlockSpec((ROW_TILE, d), lambda i: (i, 0)),
                  pl.BlockSpec((k, d), lambda i: (0, 0)),
                  pl.BlockSpec((1, d), lambda i: (0, 0)),
                  pl.BlockSpec((1, d), lambda i: (0, 0))],
        out_specs=pl.BlockSpec((ROW_TILE, d), lambda i: (i, 0)),
        out_shape=jax.ShapeDtypeStruct((rows, d), F32),
        compiler_params=_cparams(("parallel",)),
        name="outproj_ln",
    )(a, x, w, g, b)


def _mlp_ln_kernel(x_ref, w1_ref, w2_ref, g_ref, b_ref, o_ref):
    x = x_ref[...]
    h = jnp.maximum(_dot(x.astype(BF16), w1_ref[...]), 0.0)
    y = DEEPNORM_ALPHA * x + _dot((h * h).astype(BF16), w2_ref[...])
    o_ref[...] = _layernorm_rows(y, g_ref[...], b_ref[...])


def _mlp_ln(x, w1, w2, g, b):
    rows, d = x.shape
    f = w1.shape[1]
    return pl.pallas_call(
        _mlp_ln_kernel,
        grid=(rows // ROW_TILE,),
        in_specs=[pl.BlockSpec((ROW_TILE, d), lambda i: (i, 0)),
                  pl.BlockSpec((d, f), lambda i: (0, 0)),
                  pl.BlockSpec((f, d), lambda i: (0, 0)),
                  pl.BlockSpec((1, d), lambda i: (0, 0)),
                  pl.BlockSpec((1, d), lambda i: (0, 0))],
        out_specs=pl.BlockSpec((ROW_TILE, d), lambda i: (i, 0)),
        out_shape=jax.ShapeDtypeStruct((rows, d), F32),
        compiler_params=_cparams(("parallel",)),
        name="mlp_ln",
    )(x, w1, w2, g, b)


def _unit_lower_inverse(a, n_factors):
    c = a.shape[0]
    row = lax.broadcasted_iota(jnp.int32, (c, c), 0)
    col = lax.broadcasted_iota(jnp.int32, (c, c), 1)
    eye = (row == col).astype(F32)
    p = -a
    t = eye + p
    for _ in range(n_factors - 1):
        pb = p.astype(BF16)
        p = _dot(pb, pb)
        t = t + _dot(t.astype(BF16), p.astype(BF16))
    return t


def _pairwise_diff(col):
    c = col.shape[0]
    lane = lax.broadcasted_iota(jnp.int32, (c, LANES), 1)
    left = jnp.where(lane == 0, col, jnp.where(lane == 1, 1.0, 0.0))
    right = jnp.where(lane == 0, 1.0, jnp.where(lane == 1, -col, 0.0))
    return _dot_nt(left, right, precision=HIGHEST)


def _l2norm_rows(x):
    return x * lax.rsqrt(jnp.sum(x * x, axis=-1, keepdims=True) + L2_EPS)


def _gated_rmsnorm(o, z, norm_w):
    on = o * lax.rsqrt(jnp.mean(o * o, axis=-1, keepdims=True) + RMS_EPS) * norm_w
    return on * _silu(z)


def _gdn_gates(ba, a_log, dt_bias):
    beta = _sigmoid(ba[:, :GDN_V_HEADS])
    g = -jnp.exp(a_log) * _softplus(ba[:, GDN_V_HEADS:] + dt_bias)
    return beta, g


def _gdn_prompt_kernel(n_tokens, mixed_ref, z_ref, ba_ref, convw_ref, alog_ref, dtb_ref, normw_ref,
                       o_ref, sfin_ref, xc_ref, s_ref):
    c = pl.program_id(1)
    C = GDN_CHUNK
    halo = SUBLANES

    @pl.when(c == 0)
    def _():
        xc_ref[0:halo, :] = jnp.zeros((halo, GDN_CONV_DIM), F32)
        s_ref[...] = jnp.zeros_like(s_ref)

    xc_ref[halo:halo + C, :] = mixed_ref[...]
    w = convw_ref[...]
    acc = xc_ref[halo - 3:halo - 3 + C, :] * w[0:1, :]
    for j in range(1, GDN_CONV_WIDTH):
        acc = acc + xc_ref[halo - 3 + j:halo - 3 + j + C, :] * w[j:j + 1, :]
    qkv = _silu(acc)
    xc_ref[halo - 3:halo, :] = xc_ref[halo + C - 3:halo + C, :]

    row = lax.broadcasted_iota(jnp.int32, (C, 1), 0)
    valid = (c * C + row) < n_tokens
    beta, g = _gdn_gates(ba_ref[...], alog_ref[...], dtb_ref[...])
    beta = jnp.where(valid, beta, 0.0)
    g = jnp.where(valid, g, 0.0)

    ri = lax.broadcasted_iota(jnp.int32, (C, C), 0)
    ci = lax.broadcasted_iota(jnp.int32, (C, C), 1)
    tril = ci <= ri
    strict = ci < ri
    gc = jnp.dot(tril.astype(F32), g, preferred_element_type=F32, precision=HIGHEST)
    egc = jnp.exp(gc)
    glast = gc[C - 1:C, :]
    ekd = jnp.exp(glast - gc)
    eglast = jnp.exp(glast)
    normw = normw_ref[...]

    for kh in range(GDN_K_HEADS):
        q = _l2norm_rows(qkv[:, kh * GDN_HEAD:(kh + 1) * GDN_HEAD]) * (GDN_HEAD ** -0.5)
        k = _l2norm_rows(qkv[:, GDN_KEY_DIM + kh * GDN_HEAD:GDN_KEY_DIM + (kh + 1) * GDN_HEAD])
        qb = q.astype(BF16)
        kb = k.astype(BF16)
        kk = _dot_nt(kb, kb)
        qk0 = _dot_nt(qb, kb)
        for h in range(2 * kh, 2 * kh + 2):
            v = qkv[:, 2 * GDN_KEY_DIM + h * GDN_HEAD:2 * GDN_KEY_DIM + (h + 1) * GDN_HEAD]
            bcol = beta[:, h:h + 1]
            diff = _pairwise_diff(gc[:, h:h + 1])
            decay = jnp.where(tril, jnp.exp(jnp.where(tril, diff, 0.0)), 0.0)
            a_mat = jnp.where(strict, kk * decay * bcol, 0.0)
            t_inv = _unit_lower_inverse(a_mat, 6).astype(BF16)
            u = _dot(t_inv, (v * bcol).astype(BF16))
            wm = _dot(t_inv, (k * (bcol * egc[:, h:h + 1])).astype(BF16))
            s_old = s_ref[h]
            sb = s_old.astype(BF16)
            v_new = u - _dot(wm.astype(BF16), sb)
            vnb = v_new.astype(BF16)
            o = _dot((q * egc[:, h:h + 1]).astype(BF16), sb) + _dot((qk0 * decay).astype(BF16), vnb)
            kd = (k * ekd[:, h:h + 1]).astype(BF16)
            s_ref[h] = s_old * eglast[:, h:h + 1] + _dot_tn(kd, vnb)
            zh = z_ref[:, h * GDN_HEAD:(h + 1) * GDN_HEAD]
            o_ref[:, h * GDN_HEAD:(h + 1) * GDN_HEAD] = _gated_rmsnorm(o, zh, normw).astype(o_ref.dtype)

    @pl.when(c == pl.num_programs(1) - 1)
    def _():
        sfin_ref[0] = s_ref[...]


def _gdn_prompt(proj, ba, conv_w, a_log, dt_bias, norm_w, n_batch, lp, n_tokens):
    C = GDN_CHUNK
    per_b = lp // C
    n_chunks = per_b
    z_off = GDN_CONV_DIM // GDN_VALUE_DIM
    kern = functools.partial(_gdn_prompt_kernel, n_tokens)
    return pl.pallas_call(
        kern,
        grid=(n_batch, n_chunks),
        in_specs=[pl.BlockSpec((C, GDN_CONV_DIM), lambda b, c: (b * per_b + c, 0)),
                  pl.BlockSpec((C, GDN_VALUE_DIM), lambda b, c: (b * per_b + c, z_off)),
                  pl.BlockSpec((C, 2 * GDN_V_HEADS), lambda b, c: (b * per_b + c, 0)),
                  pl.BlockSpec((GDN_CONV_WIDTH, GDN_CONV_DIM), lambda b, c: (0, 0)),
                  pl.BlockSpec((1, GDN_V_HEADS), lambda b, c: (0, 0)),
                  pl.BlockSpec((1, GDN_V_HEADS), lambda b, c: (0, 0)),
                  pl.BlockSpec((1, GDN_HEAD), lambda b, c: (0, 0))],
        out_specs=[pl.BlockSpec((C, GDN_VALUE_DIM), lambda b, c: (b * per_b + c, 0)),
                   pl.BlockSpec((1, GDN_V_HEADS, GDN_HEAD, GDN_HEAD), lambda b, c: (b, 0, 0, 0))],
        out_shape=[jax.ShapeDtypeStruct((n_batch * lp, GDN_VALUE_DIM), BF16),
                   jax.ShapeDtypeStruct((n_batch, GDN_V_HEADS, GDN_HEAD, GDN_HEAD), F32)],
        scratch_shapes=[pltpu.VMEM((SUBLANES + C, GDN_CONV_DIM), F32),
                        pltpu.VMEM((GDN_V_HEADS, GDN_HEAD, GDN_HEAD), F32)],
        compiler_params=_cparams(("parallel", "arbitrary")),
        name="gdn_prompt",
    )(proj, proj, ba, conv_w, a_log, dt_bias, norm_w)


def _twice(dst_ref, x):
    t = x.shape[0]
    dst_ref[0:t, :] = x
    dst_ref[t:2 * t, :] = x
    return dst_ref[...]


def _gdn_sample_kernel(cs_ref, mixed_ref, z_ref, ba_ref, convw_ref, alog_ref, dtb_ref, normw_ref, s_ref,
                       o_ref, snew_ref, xc_ref, dq_ref, dz_ref, dba_ref):
    T = mixed_ref.shape[1]
    R = 2 * T
    W = GDN_CONV_WIDTH
    xc_ref[0:W - 1, :] = cs_ref[0]
    xc_ref[W - 1:W - 1 + T, :] = mixed_ref[0]
    w = convw_ref[...]
    acc = xc_ref[0:T, :] * w[0:1, :]
    for j in range(1, W):
        acc = acc + xc_ref[j:j + T, :] * w[j:j + 1, :]
    qkv = _twice(dq_ref, _silu(acc))
    z2 = _twice(dz_ref, z_ref[0])
    ba2 = _twice(dba_ref, ba_ref[0])
    beta, g = _gdn_gates(ba2, alog_ref[...], dtb_ref[...])

    rr = lax.broadcasted_iota(jnp.int32, (R, 1), 0)
    tmod = rr % T
    first = rr < T
    gc = jnp.zeros_like(g)
    for s in range(T):
        gc = gc + jnp.where(tmod >= s, g[s:s + 1, :], 0.0)
    glast = gc[T - 1:T, :]
    normw = normw_ref[...]

    NP = GDN_K_HEADS
    M = NP * R

    def pair_col(x, kh):
        return jnp.where(first, x[:, 2 * kh:2 * kh + 1], x[:, 2 * kh + 1:2 * kh + 2])

    def pair_tile(x, base, kh):
        a = x[:, base + (2 * kh) * GDN_HEAD:base + (2 * kh + 1) * GDN_HEAD]
        b = x[:, base + (2 * kh + 1) * GDN_HEAD:base + (2 * kh + 2) * GDN_HEAD]
        return jnp.where(first, a, b)

    q_t, k_t, v_t, z_t, b_c, gc_c, gl_c = [], [], [], [], [], [], []
    for kh in range(NP):
        q_t.append(_l2norm_rows(qkv[:, kh * GDN_HEAD:(kh + 1) * GDN_HEAD]) * (GDN_HEAD ** -0.5))
        k_t.append(_l2norm_rows(qkv[:, GDN_KEY_DIM + kh * GDN_HEAD:GDN_KEY_DIM + (kh + 1) * GDN_HEAD]))
        v_t.append(pair_tile(qkv, 2 * GDN_KEY_DIM, kh))
        z_t.append(pair_tile(z2, 0, kh))
        b_c.append(pair_col(beta, kh))
        gc_c.append(pair_col(gc, kh))
        gl_c.append(pair_col(jnp.broadcast_to(glast, (R, GDN_V_HEADS)), kh))
    q64 = jnp.concatenate(q_t, axis=0)
    k64 = jnp.concatenate(k_t, axis=0)
    v64 = jnp.concatenate(v_t, axis=0)
    bcol = jnp.concatenate(b_c, axis=0)
    gcol = jnp.concatenate(gc_c, axis=0)
    glcol = jnp.concatenate(gl_c, axis=0)

    ri = lax.broadcasted_iota(jnp.int32, (M, M), 0)
    ci = lax.broadcasted_iota(jnp.int32, (M, M), 1)
    same = (ri // T) == (ci // T)
    tril = same & ((ci % T) <= (ri % T))
    strict = same & ((ci % T) < (ri % T))
    kb = k64.astype(BF16)
    kk = _dot_nt(kb, kb)
    qk0 = _dot_nt(q64.astype(BF16), kb)
    diff = _pairwise_diff(gcol)
    decay = jnp.where(tril, jnp.exp(jnp.where(tril, diff, 0.0)), 0.0)
    a_mat = jnp.where(strict, kk * decay * bcol, 0.0)
    t_inv = _unit_lower_inverse(a_mat, max(1, (T - 1).bit_length())).astype(BF16)
    egc = jnp.exp(gcol)
    u = _dot(t_inv, (v64 * bcol).astype(BF16))
    wm = _dot(t_inv, (k64 * (bcol * egc)).astype(BF16))
    qg = q64 * egc
    kd = k64 * jnp.exp(glcol - gcol)
    qkd = (qk0 * decay).astype(BF16)

    v_new_t, qs_t = [], []
    for kh in range(NP):
        lhs = jnp.concatenate([wm[kh * R:(kh + 1) * R], qg[kh * R:(kh + 1) * R]], axis=0).astype(BF16)
        r0 = _dot(lhs, s_ref[0, 2 * kh].astype(BF16))
        r1 = _dot(lhs, s_ref[0, 2 * kh + 1].astype(BF16))
        ws = jnp.where(first, r0[:R], r1[:R])
        qs_t.append(jnp.where(first, r0[R:], r1[R:]))
        v_new_t.append(u[kh * R:(kh + 1) * R] - ws)
    v_new = jnp.concatenate(v_new_t, axis=0)
    o64 = jnp.concatenate(qs_t, axis=0) + _dot(qkd, v_new.astype(BF16))
    on = _gated_rmsnorm(o64, jnp.concatenate(z_t, axis=0), normw)

    for kh in range(NP):
        vn = v_new_t[kh].astype(BF16)
        kdp = kd[kh * R:(kh + 1) * R]
        for j in range(2):
            h = 2 * kh + j
            keep = first if j == 0 else jnp.logical_not(first)
            kdm = jnp.where(keep, kdp, 0.0).astype(BF16)
            eg = jnp.exp(glast[:, h:h + 1])
            snew_ref[0, h] = s_ref[0, h] * eg + _dot_tn(kdm, vn)
            tile = on[kh * R:(kh + 1) * R]
            if j == 1:
                tile = pltpu.roll(tile, T, axis=0)
            o_ref[0, :, h * GDN_HEAD:(h + 1) * GDN_HEAD] = tile[:T].astype(o_ref.dtype)


def _gdn_sample(conv_state, mixed, z, ba, conv_w, a_log, dt_bias, norm_w, state):
    nb, t, _ = mixed.shape
    return pl.pallas_call(
        _gdn_sample_kernel,
        grid=(nb,),
        in_specs=[pl.BlockSpec((1, GDN_CONV_WIDTH - 1, GDN_CONV_DIM), lambda b: (b, 0, 0)),
                  pl.BlockSpec((1, t, GDN_CONV_DIM), lambda b: (b, 0, 0)),
                  pl.BlockSpec((1, t, GDN_VALUE_DIM), lambda b: (b, 0, 0)),
                  pl.BlockSpec((1, t, 2 * GDN_V_HEADS), lambda b: (b, 0, 0)),
                  pl.BlockSpec((GDN_CONV_WIDTH, GDN_CONV_DIM), lambda b: (0, 0)),
                  pl.BlockSpec((1, GDN_V_HEADS), lambda b: (0, 0)),
                  pl.BlockSpec((1, GDN_V_HEADS), lambda b: (0, 0)),
                  pl.BlockSpec((1, GDN_HEAD), lambda b: (0, 0)),
                  pl.BlockSpec((1, GDN_V_HEADS, GDN_HEAD, GDN_HEAD), lambda b: (b, 0, 0, 0))],
        out_specs=[pl.BlockSpec((1, t, GDN_VALUE_DIM), lambda b: (b, 0, 0)),
                   pl.BlockSpec((1, GDN_V_HEADS, GDN_HEAD, GDN_HEAD), lambda b: (b, 0, 0, 0))],
        out_shape=[jax.ShapeDtypeStruct((nb, t, GDN_VALUE_DIM), BF16),
                   jax.ShapeDtypeStruct(state.shape, F32)],
        scratch_shapes=[pltpu.VMEM((GDN_CONV_WIDTH - 1 + t, GDN_CONV_DIM), F32),
                        pltpu.VMEM((2 * t, GDN_CONV_DIM), F32),
                        pltpu.VMEM((2 * t, GDN_VALUE_DIM), F32),
                        pltpu.VMEM((2 * t, 2 * GDN_V_HEADS), F32)],
        compiler_params=_cparams(("parallel",)),
        name="gdn_sample",
    )(conv_state, mixed, z, ba, conv_w, a_log, dt_bias, norm_w, state)


def _rope_tables(pos, half, period):
    inv_freq = ROPE_THETA ** (-jnp.arange(half, dtype=F32) * 2.0 / (2 * half))
    ang = pos.astype(F32)[:, None] * inv_freq[None, :]
    cos, sin = jnp.cos(ang), jnp.sin(ang)
    ones = jnp.ones((pos.shape[0], period - 2 * half), F32)
    cos_p = jnp.concatenate([cos, cos, ones], axis=1)
    sin_p = jnp.concatenate([-sin, sin, 0.0 * ones], axis=1)
    reps = LANES // period
    return jnp.tile(cos_p, (1, reps)), jnp.tile(sin_p, (1, reps))


def _rope_tile(x, cos, sin, half, period):
    lane = lax.broadcasted_iota(jnp.int32, x.shape, 1) % period
    partner = jnp.where(lane < half, pltpu.roll(x, LANES - half, axis=1), pltpu.roll(x, half, axis=1))
    return x * cos + partner * sin


def _dsa_proj_kernel(x_ref, w_ref, ca_ref, sa_ref, ci_ref, si_ref, g_ref, b_ref,
                     q_ref, k_ref, v_ref, vt_ref, iq_ref, ikw_ref, ik2_ref):
    x = x_ref[...].astype(BF16)
    ca, sa, ci, si = ca_ref[...], sa_ref[...], ci_ref[...], si_ref[...]
    q_off, k_off = 0, ATT_HEADS * ATT_HEAD_DIM
    v_off = k_off + ATT_KV_HEADS * ATT_HEAD_DIM
    iq_off = v_off + ATT_KV_HEADS * ATT_HEAD_DIM
    ik_off = iq_off + IDX_HEADS * IDX_HEAD_DIM

    def proj(off):
        return _dot(x, w_ref[:, off:off + LANES])

    for h in range(ATT_HEADS):
        t = _rope_tile(proj(q_off + h * LANES), ca, sa, ATT_ROT_HALF, LANES)
        q_ref[:, h * LANES:(h + 1) * LANES] = (t * (ATT_HEAD_DIM ** -0.5)).astype(q_ref.dtype)
    for h in range(ATT_KV_HEADS):
        k_ref[:, h * LANES:(h + 1) * LANES] = _rope_tile(proj(k_off + h * LANES), ca, sa, ATT_ROT_HALF, LANES)
        v = proj(v_off + h * LANES)
        v_ref[:, h * LANES:(h + 1) * LANES] = v
        vt_ref[h * LANES:(h + 1) * LANES, :] = v.T.astype(vt_ref.dtype)
    for h in range(IDX_HEADS * IDX_HEAD_DIM // LANES):
        t = _rope_tile(proj(iq_off + h * LANES), ci, si, IDX_ROT_HALF, IDX_HEAD_DIM)
        iq_ref[:, h * LANES:(h + 1) * LANES] = t.astype(iq_ref.dtype)
    t = proj(ik_off)
    lane = lax.broadcasted_iota(jnp.int32, t.shape, 1)
    is_ik = lane < IDX_HEAD_DIM
    mu = jnp.sum(jnp.where(is_ik, t, 0.0), axis=-1, keepdims=True) / IDX_HEAD_DIM
    tc = jnp.where(is_ik, t - mu, 0.0)
    var = jnp.sum(tc * tc, axis=-1, keepdims=True) / IDX_HEAD_DIM
    ik = _rope_tile(tc * lax.rsqrt(var + LN_EPS) * g_ref[...] + b_ref[...], ci, si, IDX_ROT_HALF, IDX_HEAD_DIM)
    ikw = jnp.where(is_ik, ik, t * (IDX_HEADS ** -0.5))
    ikw_ref[...] = ikw
    ik_only = jnp.where(is_ik, ik, 0.0)
    ik2_ref[...] = (ik_only + pltpu.roll(ik_only, IDX_HEAD_DIM, axis=1)).astype(ik2_ref.dtype)


def _dsa_proj(x, w, tables, ik_g, ik_b):
    rows, d = x.shape
    n = w.shape[1]
    kv = ATT_KV_HEADS * ATT_HEAD_DIM
    row_spec = lambda width: pl.BlockSpec((ROW_TILE, width), lambda i: (i, 0))
    const_spec = lambda shape: pl.BlockSpec(shape, lambda i: (0, 0))
    return pl.pallas_call(
        _dsa_proj_kernel,
        grid=(rows // ROW_TILE,),
        in_specs=[row_spec(d), const_spec((d, n))] + [row_spec(LANES)] * 4 + [const_spec((1, LANES))] * 2,
        out_specs=[row_spec(ATT_HEADS * ATT_HEAD_DIM), row_spec(kv), row_spec(kv),
                   pl.BlockSpec((kv, ROW_TILE), lambda i: (0, i)),
                   row_spec(IDX_HEADS * IDX_HEAD_DIM), row_spec(LANES), row_spec(LANES)],
        out_shape=[jax.ShapeDtypeStruct((rows, ATT_HEADS * ATT_HEAD_DIM), BF16),
                   jax.ShapeDtypeStruct((rows, kv), F32),
                   jax.ShapeDtypeStruct((rows, kv), F32),
                   jax.ShapeDtypeStruct((kv, rows), BF16),
                   jax.ShapeDtypeStruct((rows, IDX_HEADS * IDX_HEAD_DIM), BF16),
                   jax.ShapeDtypeStruct((rows, LANES), F32),
                   jax.ShapeDtypeStruct((rows, LANES), BF16)],
        compiler_params=_cparams(("parallel",)),
        name="dsa_proj",
    )(x, w, *tables, ik_g, ik_b)


def _float_order_key(x):
    b = pltpu.bitcast(x, jnp.int32)
    return b ^ ((b >> 31) & jnp.int32(0x7FFFFFFF))


def _float_from_key(k):
    return pltpu.bitcast(k ^ ((k >> 31) & jnp.int32(0x7FFFFFFF)), F32)


def _np_order_key(x):
    b = int(np.float32(x).view(np.int32))
    return b ^ ((b >> 31) & 0x7FFFFFFF)


_KEY_NEG_MAX = _np_order_key(-F32_MAX)
_KEY_POS_INF = _np_order_key(np.inf)


def _kth_largest_bounds(count_ge, target, shape):
    lo = jnp.full(shape, _KEY_NEG_MAX, jnp.int32)
    hi = jnp.full(shape, _KEY_POS_INF, jnp.int32)

    def body(_, carry):
        lo, hi = carry
        mid = (lo >> 1) + (hi >> 1) + (lo & hi & 1)
        ok = count_ge(_float_from_key(mid)) >= target
        return jnp.where(ok, mid, lo), jnp.where(ok, hi, mid)

    lo, hi = lax.fori_loop(0, 32, body, (lo, hi))
    return _float_from_key(lo), _float_from_key(hi)


def _dsa_prompt_kernel(topk, q_ref, iq_ref, ikw_ref, k_ref, vt_ref, ik2_ref, o_ref,
                       s_ref, xh_ref, qg_ref):
    i = pl.program_id(1)
    nk = i + 1
    KT = KEY_TILE
    row = lax.broadcasted_iota(jnp.int32, (KT, KT), 0)
    lane = lax.broadcasted_iota(jnp.int32, (KT, KT), 1)
    qpos = i * KT + lax.broadcasted_iota(jnp.int32, (1, KT), 1)

    for h in range(IDX_HEADS):
        tile = iq_ref[:, (h // 2) * LANES:(h // 2 + 1) * LANES]
        mine = (lane // IDX_HEAD_DIM) == (h % 2)
        xh_ref[h] = jnp.where(mine, tile, jnp.zeros_like(tile))
    for g in range(ATT_KV_HEADS):
        for hq in range(ATT_GROUP):
            h = g * ATT_GROUP + hq
            qg_ref[g, hq * KT:(hq + 1) * KT, :] = q_ref[:, h * LANES:(h + 1) * LANES]
    w_rows = ikw_ref[...].T[IDX_HEAD_DIM:IDX_HEAD_DIM + IDX_HEADS, :]

    def score_tile(kt, _):
        ik = ik2_ref[pl.ds(pl.multiple_of(kt * KT, KT), KT), :]
        acc = jnp.zeros((KT, KT), F32)
        for h in range(IDX_HEADS):
            acc = acc + w_rows[h:h + 1, :] * jnp.maximum(_dot_nt(ik, xh_ref[h]), 0.0)
        acc = acc * (IDX_HEAD_DIM ** -0.5)
        kpos = kt * KT + row
        searchable = (kpos <= qpos) & (kpos >= N_META)
        s_ref[pl.ds(pl.multiple_of(kt * KT, KT), KT), :] = jnp.where(searchable, acc, NEG_INF)
        return 0

    lax.fori_loop(0, nk, score_tile, 0)

    n_causal = jnp.minimum(qpos + 1, topk)
    target = (n_causal - jnp.minimum(qpos + 1, N_META)).astype(F32)

    def count_ge(thr):
        def body(kt, acc):
            t = s_ref[pl.ds(pl.multiple_of(kt * KT, KT), KT), :]
            hit = jnp.where(t >= thr, 1.0, 0.0).reshape(KT // SUBLANES, SUBLANES, KT)
            return acc + jnp.sum(hit, axis=0)
        acc = lax.fori_loop(0, nk, body, jnp.zeros((SUBLANES, KT), F32))
        return jnp.sum(acc, axis=0, keepdims=True)

    lo, hi = _kth_largest_bounds(count_ge, target, (1, KT))
    need = target - count_ge(hi)
    tri = (lane <= row).astype(BF16)

    def bias_tile(kt, carry):
        sl = pl.ds(pl.multiple_of(kt * KT, KT), KT)
        t = s_ref[sl, :]
        above = t >= hi
        tie = (t >= lo) & jnp.logical_not(above)
        tie_f = jnp.where(tie, 1.0, 0.0)
        rank = carry + _dot(tri, tie_f.astype(BF16))
        kpos = kt * KT + row
        keep = above | (tie & (rank <= need)) | ((kpos < N_META) & (kpos <= qpos))
        s_ref[sl, :] = jnp.where(keep, 0.0, NEG_INF)
        return carry + jnp.sum(tie_f, axis=0, keepdims=True)

    lax.fori_loop(0, nk, bias_tile, jnp.zeros((1, KT), F32))

    for g in range(ATT_KV_HEADS):
        qg = qg_ref[g]

        def att_tile(kt, carry, g=g, qg=qg):
            m, l, acc = carry
            sl = pl.ds(pl.multiple_of(kt * KT, KT), KT)
            kt_k = k_ref[sl, g * LANES:(g + 1) * LANES]
            bias = s_ref[sl, :]
            sc = _dot_nt(kt_k, qg) + jnp.concatenate([bias] * ATT_GROUP, axis=1)
            m_new = jnp.maximum(m, jnp.max(sc, axis=0, keepdims=True))
            alpha = jnp.exp(m - m_new)
            p = jnp.exp(sc - m_new)
            l = l * alpha + jnp.sum(p, axis=0, keepdims=True)
            vt = vt_ref[g * LANES:(g + 1) * LANES, sl]
            acc = acc * alpha + _dot(vt, p.astype(BF16))
            return m_new, l, acc

        m0 = jnp.full((1, ATT_GROUP * KT), NEG_INF, F32)
        l0 = jnp.zeros((1, ATT_GROUP * KT), F32)
        a0 = jnp.zeros((ATT_HEAD_DIM, ATT_GROUP * KT), F32)
        _, l, acc = lax.fori_loop(0, nk, att_tile, (m0, l0, a0))
        out_t = acc / l
        for hq in range(ATT_GROUP):
            h = g * ATT_GROUP + hq
            o_ref[:, h * LANES:(h + 1) * LANES] = out_t[:, hq * KT:(hq + 1) * KT].T.astype(o_ref.dtype)


def _dsa_prompt(q, iq, ikw, k, vt, ik2, n_batch, lp, topk):
    KT = KEY_TILE
    nq = lp // KT
    kv = ATT_KV_HEADS * ATT_HEAD_DIM
    qspec = lambda width: pl.BlockSpec((KT, width), lambda b, i: (b * nq + i, 0))
    return pl.pallas_call(
        functools.partial(_dsa_prompt_kernel, topk),
        grid=(n_batch, nq),
        in_specs=[qspec(ATT_HEADS * ATT_HEAD_DIM), qspec(IDX_HEADS * IDX_HEAD_DIM), qspec(LANES),
                  pl.BlockSpec((lp, kv), lambda b, i: (b, 0)),
                  pl.BlockSpec((kv, lp), lambda b, i: (0, b)),
                  pl.BlockSpec((lp, LANES), lambda b, i: (b, 0))],
        out_specs=qspec(ATT_HEADS * ATT_HEAD_DIM),
        out_shape=jax.ShapeDtypeStruct((n_batch * lp, ATT_HEADS * ATT_HEAD_DIM), BF16),
        scratch_shapes=[pltpu.VMEM((lp, KT), F32),
                        pltpu.VMEM((IDX_HEADS, KT, LANES), BF16),
                        pltpu.VMEM((ATT_KV_HEADS, ATT_GROUP * KT, ATT_HEAD_DIM), BF16)],
        compiler_params=_cparams(("parallel", "arbitrary")),
        name="dsa_prompt",
    )(q, iq, ikw, k, vt, ik2)


def _dsa_sample_kernel(topk, pt_ref, q_ref, iq_ref, ikw_ref, kn_ref, vn_ref, cik_ref, ck_ref, cv_ref, o_ref,
                       s_ref, kall_ref, vall_ref, q8_ref, iq8_ref, ikw8_ref, newk_ref):
    del pt_ref
    p = pl.program_id(1)
    n_pages = pl.num_programs(1)
    T = q_ref.shape[1]
    R = 2 * T
    past = n_pages * PAGE
    n_keys = s_ref.shape[1]

    @pl.when(p == 0)
    def _():
        q8_ref[0:T, :] = q_ref[0].astype(F32)
        q8_ref[T:R, :] = q_ref[0].astype(F32)
        iq8_ref[0:T, :] = iq_ref[0].astype(F32)
        iq8_ref[T:R, :] = iq_ref[0].astype(F32)
        ikw8_ref[0:T, :] = ikw_ref[0]
        ikw8_ref[T:R, :] = ikw_ref[0]

    lane = lax.broadcasted_iota(jnp.int32, (R, LANES), 1)
    ikw8 = ikw8_ref[...]

    def index_scores(keys2):
        acc = jnp.zeros((R, LANES), F32)
        for h in range(IDX_HEADS):
            tile = iq8_ref[:, (h // 2) * LANES:(h // 2 + 1) * LANES]
            xh = jnp.where((lane // IDX_HEAD_DIM) == (h % 2), tile, 0.0).astype(BF16)
            wcol = ikw8[:, IDX_HEAD_DIM + h:IDX_HEAD_DIM + h + 1]
            acc = acc + wcol * jnp.maximum(_dot_nt(xh, keys2), 0.0)
        return acc * (IDX_HEAD_DIM ** -0.5)

    ikp = cik_ref[0].astype(BF16)
    sc = index_scores(jnp.concatenate([ikp, ikp], axis=1))
    kpos = p * PAGE + lane
    page_sl = pl.ds(pl.multiple_of(p * PAGE, PAGE), PAGE)
    s_ref[:, page_sl] = jnp.where(kpos >= N_META, sc, NEG_INF)
    kall_ref[page_sl, :] = ck_ref[0]
    vall_ref[page_sl, :] = cv_ref[0]

    @pl.when(p == n_pages - 1)
    def _():
        kv = ATT_KV_HEADS * ATT_HEAD_DIM
        tmod = lax.broadcasted_iota(jnp.int32, (R, 1), 0) % T
        kall_ref[past:n_keys, :] = jnp.zeros((n_keys - past, kv), F32)
        vall_ref[past:n_keys, :] = jnp.zeros((n_keys - past, kv), F32)
        kall_ref[past:past + T, :] = kn_ref[0]
        vall_ref[past:past + T, :] = vn_ref[0]
        newk_ref[...] = jnp.zeros_like(newk_ref)
        newk_ref[0:T, :] = ikw_ref[0]
        nk = newk_ref[...]
        lane_k = lax.broadcasted_iota(jnp.int32, nk.shape, 1)
        nk = jnp.where(lane_k < IDX_HEAD_DIM, nk, 0.0)
        nk2 = (nk + pltpu.roll(nk, IDX_HEAD_DIM, axis=1)).astype(BF16)
        sc_new = index_scores(nk2)
        s_ref[:, past:n_keys] = jnp.where(lane <= tmod, sc_new, NEG_INF)

        qpos = past + tmod
        target = (jnp.minimum(qpos + 1, topk) - jnp.minimum(qpos + 1, N_META)).astype(F32)

        def count_ge(thr):
            return jnp.sum(jnp.where(s_ref[...] >= thr, 1.0, 0.0), axis=1, keepdims=True)

        lo, hi = _kth_largest_bounds(count_ge, target, (R, 1))
        need = target - count_ge(hi)
        row_t = lax.broadcasted_iota(jnp.int32, (LANES, LANES), 0)
        lane_t = lax.broadcasted_iota(jnp.int32, (LANES, LANES), 1)
        triu = (row_t <= lane_t).astype(BF16)
        carry = jnp.zeros((R, 1), F32)
        for kt in range(n_keys // LANES):
            t = s_ref[:, kt * LANES:(kt + 1) * LANES]
            above = t >= hi
            tie = (t >= lo) & jnp.logical_not(above)
            tie_f = jnp.where(tie, 1.0, 0.0)
            rank = carry + _dot(tie_f.astype(BF16), triu)
            kp = kt * LANES + lane
            keep = above | (tie & (rank <= need)) | (kp < N_META)
            s_ref[:, kt * LANES:(kt + 1) * LANES] = jnp.where(keep, 0.0, NEG_INF)
            carry = carry + jnp.sum(tie_f, axis=1, keepdims=True)

        bias = jnp.concatenate([s_ref[...]] * ATT_GROUP, axis=0)
        for g in range(ATT_KV_HEADS):
            qg = jnp.concatenate([q8_ref[:, (g * ATT_GROUP + hq) * LANES:(g * ATT_GROUP + hq + 1) * LANES]
                                  for hq in range(ATT_GROUP)], axis=0).astype(BF16)
            kg = kall_ref[:, g * LANES:(g + 1) * LANES].astype(BF16)
            vg = vall_ref[:, g * LANES:(g + 1) * LANES].astype(BF16)
            sc_a = _dot_nt(qg, kg) + bias
            m = jnp.max(sc_a, axis=1, keepdims=True)
            pr = jnp.exp(sc_a - m)
            out = _dot(pr.astype(BF16), vg) / jnp.sum(pr, axis=1, keepdims=True)
            for hq in range(ATT_GROUP):
                h = g * ATT_GROUP + hq
                o_ref[0, :, h * LANES:(h + 1) * LANES] = out[hq * R:hq * R + T].astype(o_ref.dtype)


def _dsa_sample(page_table, q, iq, ikw, k_new, v_new, cache_ik, cache_k, cache_v, topk):
    nb, t, _ = q.shape
    n_pages = page_table.shape[1]
    kv = ATT_KV_HEADS * ATT_HEAD_DIM
    n_keys = n_pages * PAGE + LANES
    tok = lambda width: pl.BlockSpec((1, t, width), lambda b, p, pt: (b, 0, 0))
    page = lambda width: pl.BlockSpec((1, PAGE, width), lambda b, p, pt: (pt[b, p], 0, 0))
    grid_spec = pltpu.PrefetchScalarGridSpec(
        num_scalar_prefetch=1,
        grid=(nb, n_pages),
        in_specs=[tok(ATT_HEADS * ATT_HEAD_DIM), tok(IDX_HEADS * IDX_HEAD_DIM), tok(LANES), tok(kv), tok(kv),
                  page(IDX_HEAD_DIM), page(kv), page(kv)],
        out_specs=tok(ATT_HEADS * ATT_HEAD_DIM),
        scratch_shapes=[pltpu.VMEM((2 * t, n_keys), F32),
                        pltpu.VMEM((n_keys, kv), F32),
                        pltpu.VMEM((n_keys, kv), F32),
                        pltpu.VMEM((2 * t, ATT_HEADS * ATT_HEAD_DIM), F32),
                        pltpu.VMEM((2 * t, IDX_HEADS * IDX_HEAD_DIM), F32),
                        pltpu.VMEM((2 * t, LANES), F32),
                        pltpu.VMEM((LANES, LANES), F32)])
    return pl.pallas_call(
        functools.partial(_dsa_sample_kernel, topk),
        grid_spec=grid_spec,
        out_shape=jax.ShapeDtypeStruct((nb, t, ATT_HEADS * ATT_HEAD_DIM), BF16),
        compiler_params=_cparams(("parallel", "arbitrary")),
        name="dsa_sample",
    )(page_table, q, iq, ikw, k_new, v_new, cache_ik, cache_k, cache_v)


def kernel(x_prompt, x_sample, state_gdn, state_gdn_conv, cache_k, cache_v, cache_idx_k, page_table,
           meta_tokens, ln1_g, ln1_b, ln2_g, ln2_b, mlp_w1, mlp_w2,
           gdn_w_in, gdn_conv_w, gdn_a_log, gdn_dt_bias, gdn_norm_w, gdn_w_out,
           dsa_w_in, dsa_ik_norm_g, dsa_ik_norm_b, dsa_w_o):
    nb, seq, d = x_prompt.shape
    ns, ts, _ = x_sample.shape
    n_tok = N_META + seq
    lp = -(-n_tok // KEY_TILE) * KEY_TILE
    n_prompt_rows = nb * lp
    rows = n_prompt_rows + ns * ts
    assert rows % ROW_TILE == 0 and lp % GDN_CHUNK == 0 and d == D_MODEL
    kvd = ATT_KV_HEADS * ATT_HEAD_DIM

    meta = jnp.broadcast_to(meta_tokens[None].astype(x_prompt.dtype), (nb, N_META, d))
    pad = jnp.zeros((nb, lp - n_tok, d), x_prompt.dtype)
    h = jnp.concatenate([jnp.concatenate([meta, x_prompt, pad], 1).reshape(n_prompt_rows, d),
                         x_sample.reshape(ns * ts, d)], 0)

    def row2(x):
        return x.reshape(1, -1)

    def prompt_rows(x, width):
        return x[:n_prompt_rows].reshape(nb, lp, -1)[:, :n_tok, :width]

    w_in = gdn_w_in[0]
    split = GDN_CONV_DIM + GDN_VALUE_DIM
    proj = _matmul(h, w_in[:, :split].astype(BF16), 1024)
    ba = _matmul(h, w_in[:, split:].astype(BF16), 2 * GDN_V_HEADS)
    gdn_consts = (gdn_conv_w[0], row2(gdn_a_log[0]), row2(gdn_dt_bias[0]), row2(gdn_norm_w[0]))
    a_p, gdn_state_prompt = _gdn_prompt(proj, ba, *gdn_consts, nb, lp, n_tok)
    proj_s = proj[n_prompt_rows:].reshape(ns, ts, split)
    mixed_s = proj_s[:, :, :GDN_CONV_DIM]
    a_s, gdn_state_sample = _gdn_sample(state_gdn_conv[0], mixed_s, proj_s[:, :, GDN_CONV_DIM:],
                                        ba[n_prompt_rows:].reshape(ns, ts, -1), *gdn_consts, state_gdn[0])
    a = jnp.concatenate([a_p, a_s.reshape(ns * ts, -1)], 0)
    h = _outproj_ln(a, h, gdn_w_out[0].astype(BF16), row2(ln1_g[0]), row2(ln1_b[0]))
    h = _mlp_ln(h, mlp_w1[0].astype(BF16), mlp_w2[0].astype(BF16), row2(ln2_g[0]), row2(ln2_b[0]))
    keep = GDN_CONV_WIDTH - 1
    gdn_conv_prompt = prompt_rows(proj, GDN_CONV_DIM)[:, n_tok - keep:]
    gdn_conv_sample = jnp.concatenate([state_gdn_conv[0], mixed_s], 1)[:, -keep:]

    past = page_table.shape[1] * PAGE
    pos = jnp.concatenate([jnp.tile(jnp.arange(lp, dtype=jnp.int32), nb),
                           jnp.tile(past + jnp.arange(ts, dtype=jnp.int32), ns)])
    tables = _rope_tables(pos, ATT_ROT_HALF, LANES) + _rope_tables(pos, IDX_ROT_HALF, IDX_HEAD_DIM)
    n_in = dsa_w_in.shape[2]
    w_dsa = jnp.pad(dsa_w_in[0], ((0, 0), (0, -n_in % LANES))).astype(BF16)
    pad_lanes = lambda v: row2(jnp.pad(v, (0, LANES - v.shape[0])))
    q, k, v, vt, iq, ikw, ik2 = _dsa_proj(h, w_dsa, tables, pad_lanes(dsa_ik_norm_g[0]),
                                          pad_lanes(dsa_ik_norm_b[0]))
    o_p = _dsa_prompt(q, iq, ikw, k.astype(BF16), vt, ik2, nb, lp, min(TOPK_MAX, (n_tok - N_META) // 4))
    smp = lambda x: x[n_prompt_rows:].reshape(ns, ts, -1)
    n_pool = cache_k.shape[1]
    o_s = _dsa_sample(page_table, smp(q), smp(iq), smp(ikw), smp(k), smp(v),
                      cache_idx_k[0], cache_k[0].reshape(n_pool, PAGE, kvd), cache_v[0].reshape(n_pool, PAGE, kvd),
                      min(TOPK_MAX, (past + ts) // 4))
    a = jnp.concatenate([o_p, o_s.reshape(ns * ts, -1)], 0)
    h = _outproj_ln(a, h, dsa_w_o[0].astype(BF16), row2(ln1_g[1]), row2(ln1_b[1]))
    h = _mlp_ln(h, mlp_w1[1].astype(BF16), mlp_w2[1].astype(BF16), row2(ln2_g[1]), row2(ln2_b[1]))

    heads = lambda x: x.reshape(x.shape[:-1] + (ATT_KV_HEADS, ATT_HEAD_DIM))
    y_prompt = h[:n_prompt_rows].reshape(nb, lp, d)[:, N_META:n_tok]
    y_sample = h[n_prompt_rows:].reshape(ns, ts, d)
    return (y_prompt, y_sample,
            gdn_state_prompt[None], gdn_conv_prompt[None], gdn_state_sample[None], gdn_conv_sample[None],
            heads(prompt_rows(k, kvd))[None], heads(prompt_rows(v, kvd))[None],
            prompt_rows(ikw, IDX_HEAD_DIM)[None],
            heads(smp(k))[None], heads(smp(v))[None], smp(ikw)[:, :, :IDX_HEAD_DIM][None])
```

```python
import functools
import math

import jax
import jax.numpy as jnp
import numpy as np
from jax import lax
from jax.experimental import pallas as pl
from jax.experimental.pallas import tpu as pltpu

F32 = jnp.float32
BF16 = jnp.bfloat16
HIGHEST = lax.Precision.HIGHEST

D_MODEL = 1024
N_META = 16
DEPTH = 2
LN_EPS = 1e-5
DEEPNORM_ALPHA = (2 * DEPTH) ** 0.25
GDN_K_HEADS = 8
GDN_V_HEADS = 16
GDN_HEAD = 128
GDN_KEY_DIM = GDN_K_HEADS * GDN_HEAD
GDN_VALUE_DIM = GDN_V_HEADS * GDN_HEAD
GDN_CONV_DIM = 2 * GDN_KEY_DIM + GDN_VALUE_DIM
GDN_CONV_WIDTH = 4
GDN_CHUNK = 64
L2_EPS = 1e-6
RMS_EPS = 1e-6
ATT_HEADS = 8
ATT_KV_HEADS = 2
ATT_HEAD_DIM = 128
ATT_GROUP = ATT_HEADS // ATT_KV_HEADS
IDX_HEADS = 8
IDX_HEAD_DIM = 64
TOPK_MAX = 256
ROPE_THETA = 500000.0
ATT_ROT_HALF = ATT_HEAD_DIM // 8
IDX_ROT_HALF = IDX_HEAD_DIM // 8
PAGE = 128

LANES = 128
SUBLANES = 8
ROW_TILE = 512
KEY_TILE = 128
KEY_CHUNK = 3 * KEY_TILE
VMEM_LIMIT = 56 * 1024 * 1024

NEG_INF = float("-inf")
F32_MAX = float(np.finfo(np.float32).max)


def _cparams(sem):
    return pltpu.CompilerParams(dimension_semantics=sem, vmem_limit_bytes=VMEM_LIMIT)


def _dot(a, b):
    return jnp.dot(a, b, preferred_element_type=F32)


def _dot_nt(a, b, precision=None):
    return lax.dot_general(a, b, (((1,), (1,)), ((), ())), preferred_element_type=F32,
                           precision=precision)


def _dot_tn(a, b):
    return lax.dot_general(a, b, (((0,), (0,)), ((), ())), preferred_element_type=F32)


def _layernorm_rows(x, g, b):
    mu = jnp.mean(x, axis=-1, keepdims=True)
    xc = x - mu
    var = jnp.mean(xc * xc, axis=-1, keepdims=True)
    return xc * lax.rsqrt(var + LN_EPS) * g + b


def _sigmoid(x):
    return 1.0 / (1.0 + jnp.exp(-x))


def _silu(x):
    return x * _sigmoid(x)


def _softplus(x):
    return jnp.maximum(x, 0.0) + jnp.log(1.0 + jnp.exp(-jnp.abs(x)))


def _matmul_kernel(x_ref, w_ref, o_ref):
    o_ref[...] = _dot(x_ref[...].astype(BF16), w_ref[...]).astype(o_ref.dtype)


def _matmul(x, w, tn):
    rows, k = x.shape
    n = w.shape[1]
    return pl.pallas_call(
        _matmul_kernel,
        grid=(rows // ROW_TILE, n // tn),
        in_specs=[pl.BlockSpec((ROW_TILE, k), lambda i, j: (i, 0)),
                  pl.BlockSpec((k, tn), lambda i, j: (0, j))],
        out_specs=pl.BlockSpec((ROW_TILE, tn), lambda i, j: (i, j)),
        out_shape=jax.ShapeDtypeStruct((rows, n), F32),
        compiler_params=_cparams(("parallel", "arbitrary")),
        name="matmul",
    )(x, w)


def _outproj_ln_kernel(a_ref, x_ref, w_ref, g_ref, b_ref, o_ref):
    y = DEEPNORM_ALPHA * x_ref[...] + _dot(a_ref[...], w_ref[...])
    o_ref[...] = _layernorm_rows(y, g_ref[...], b_ref[...])


def _outproj_ln(a, x, w, g, b):
    rows, k = a.shape
    d = x.shape[1]
    return pl.pallas_call(
        _outproj_ln_kernel,
        grid=(rows // ROW_TILE,),
        in_specs=[pl.BlockSpec((ROW_TILE, k), lambda i: (i, 0)),
                  pl.BlockSpec((ROW_TILE, d), lambda i: (i, 0)),
                  pl.BlockSpec((k, d), lambda i: (0, 0)),
                  pl.BlockSpec((1, d), lambda i: (0, 0)),
                  pl.BlockSpec((1, d), lambda i: (0, 0))],
        out_specs=pl.BlockSpec((ROW_TILE, d), lambda i: (i, 0)),
        out_shape=jax.ShapeDtypeStruct((rows, d), F32),
        compiler_params=_cparams(("parallel",)),
        name="outproj_ln",
    )(a, x, w, g, b)


def _mlp_ln_kernel(x_ref, w1_ref, w2_ref, g_ref, b_ref, o_ref):
    x = x_ref[...]
    h = jnp.maximum(_dot(x.astype(BF16), w1_ref[...]), 0.0)
    y = DEEPNORM_ALPHA * x + _dot((h * h).astype(BF16), w2_ref[...])
    o_ref[...] = _layernorm_rows(y, g_ref[...], b_ref[...])


def _mlp_ln(x, w1, w2, g, b):
    rows, d = x.shape
    f = w1.shape[1]
    return pl.pallas_call(
        _mlp_ln_kernel,
        grid=(rows // ROW_TILE,),
        in_specs=[pl.BlockSpec((ROW_TILE, d), lambda i: (i, 0)),
                  pl.BlockSpec((d, f), lambda i: (0, 0)),
                  pl.BlockSpec((f, d), lambda i: (0, 0)),
                  pl.BlockSpec((1, d), lambda i: (0, 0)),
                  pl.BlockSpec((1, d), lambda i: (0, 0))],
        out_specs=pl.BlockSpec((ROW_TILE, d), lambda i: (i, 0)),
        out_shape=jax.ShapeDtypeStruct((rows, d), F32),
        compiler_params=_cparams(("parallel",)),
        name="mlp_ln",
    )(x, w1, w2, g, b)


def _unit_lower_inverse(a, n_factors):
    c = a.shape[0]
    row = lax.broadcasted_iota(jnp.int32, (c, c), 0)
    col = lax.broadcasted_iota(jnp.int32, (c, c), 1)
    eye = (row == col).astype(F32)
    p = -a
    t = eye + p
    for _ in range(n_factors - 1):
        pb = p.astype(BF16)
        p = _dot(pb, pb)
        t = t + _dot(t.astype(BF16), p.astype(BF16))
    return t


def _bdot(a, b):
    return lax.dot_general(a, b, (((2,), (1,)), ((0,), (0,))), preferred_element_type=F32)


def _bdot_nt(a, b):
    return lax.dot_general(a, b, (((2,), (2,)), ((0,), (0,))), preferred_element_type=F32)


def _bdot_tn(a, b):
    return lax.dot_general(a, b, (((1,), (1,)), ((0,), (0,))), preferred_element_type=F32)


def _unit_lower_inverse_batched(a, n_factors):
    c = a.shape[-1]
    row = lax.broadcasted_iota(jnp.int32, (c, c), 0)
    col = lax.broadcasted_iota(jnp.int32, (c, c), 1)
    p = -a
    t = (row == col).astype(F32) + p
    for _ in range(n_factors - 1):
        pb = p.astype(BF16)
        p = _bdot(pb, pb)
        t = t + _bdot(t.astype(BF16), p.astype(BF16))
    return t


def _pairwise_diff(col):
    c = col.shape[0]
    lane = lax.broadcasted_iota(jnp.int32, (c, LANES), 1)
    left = jnp.where(lane == 0, col, jnp.where(lane == 1, 1.0, 0.0))
    right = jnp.where(lane == 0, 1.0, jnp.where(lane == 1, -col, 0.0))
    return _dot_nt(left, right, precision=HIGHEST)


def _l2norm_rows(x):
    return x * lax.rsqrt(jnp.sum(x * x, axis=-1, keepdims=True) + L2_EPS)


def _gated_rmsnorm(o, z, norm_w):
    on = o * lax.rsqrt(jnp.mean(o * o, axis=-1, keepdims=True) + RMS_EPS) * norm_w
    return on * _silu(z)


def _gdn_gates(ba, a_log, dt_bias):
    beta = _sigmoid(ba[:, :GDN_V_HEADS])
    g = -jnp.exp(a_log) * _softplus(ba[:, GDN_V_HEADS:] + dt_bias)
    return beta, g


def _gdn_prompt_kernel(n_tokens, mixed_ref, z_ref, ba_ref, convw_ref, alog_ref, dtb_ref, normw_ref,
                       o_ref, sfin_ref, xc_ref, s_ref):
    c = pl.program_id(1)
    C = GDN_CHUNK
    halo = SUBLANES

    @pl.when(c == 0)
    def _():
        xc_ref[0:halo, :] = jnp.zeros((halo, GDN_CONV_DIM), F32)
        s_ref[...] = jnp.zeros_like(s_ref)

    xc_ref[halo:halo + C, :] = mixed_ref[...]
    w = convw_ref[...]
    acc = xc_ref[halo - 3:halo - 3 + C, :] * w[0:1, :]
    for j in range(1, GDN_CONV_WIDTH):
        acc = acc + xc_ref[halo - 3 + j:halo - 3 + j + C, :] * w[j:j + 1, :]
    qkv = _silu(acc)
    xc_ref[halo - 3:halo, :] = xc_ref[halo + C - 3:halo + C, :]

    row = lax.broadcasted_iota(jnp.int32, (C, 1), 0)
    valid = (c * C + row) < n_tokens
    ba = ba_ref[...]
    beta = jnp.where(valid, _sigmoid(ba), 0.0)
    g = jnp.where(valid, -jnp.exp(alog_ref[...]) * _softplus(ba + dtb_ref[...]), 0.0)

    ri = lax.broadcasted_iota(jnp.int32, (C, C), 0)
    ci = lax.broadcasted_iota(jnp.int32, (C, C), 1)
    tril = ci <= ri
    strict = ci < ri
    gc = jnp.dot(tril.astype(F32), g, preferred_element_type=F32, precision=HIGHEST)
    gc_rows = gc.T
    egc = jnp.exp(gc)
    glast = gc[C - 1:C, :]
    ekd = jnp.exp(glast - gc)
    eglast = jnp.exp(glast)
    normw = normw_ref[...]
    NH = GDN_V_HEADS

    def gcol(x, h):
        return x[:, NH + h:NH + h + 1]

    q_l = [_l2norm_rows(qkv[:, kh * GDN_HEAD:(kh + 1) * GDN_HEAD]) * (GDN_HEAD ** -0.5)
           for kh in range(GDN_K_HEADS)]
    k_l = [_l2norm_rows(qkv[:, GDN_KEY_DIM + kh * GDN_HEAD:GDN_KEY_DIM + (kh + 1) * GDN_HEAD])
           for kh in range(GDN_K_HEADS)]
    k8 = jnp.stack(k_l).astype(BF16)
    kk8 = _bdot_nt(k8, k8)
    qk8 = _bdot_nt(jnp.stack(q_l).astype(BF16), k8)
    a_l, qkd_l, rhs_l, qg_l, kd_l = [], [], [], [], []
    for h in range(NH):
        kh = h // 2
        v = qkv[:, 2 * GDN_KEY_DIM + h * GDN_HEAD:2 * GDN_KEY_DIM + (h + 1) * GDN_HEAD]
        bcol = beta[:, h:h + 1]
        diff = gcol(gc, h) - gc_rows[NH + h:NH + h + 1, :]
        decay = jnp.where(tril, jnp.exp(jnp.where(tril, diff, 0.0)), 0.0)
        a_l.append(jnp.where(strict, kk8[kh] * decay * bcol, 0.0))
        qkd_l.append((qk8[kh] * decay).astype(BF16))
        rhs_l.append(jnp.concatenate([v * bcol, k_l[kh] * (bcol * gcol(egc, h))], axis=1).astype(BF16))
        qg_l.append(q_l[kh] * gcol(egc, h))
        kd_l.append((k_l[kh] * gcol(ekd, h)).astype(BF16))
    t_inv = _unit_lower_inverse_batched(jnp.stack(a_l), 6).astype(BF16)
    sol = _bdot(t_inv, jnp.stack(rhs_l))
    u, wm = sol[:, :, :GDN_HEAD], sol[:, :, GDN_HEAD:]
    s_old = s_ref[...]
    lhs = jnp.concatenate([wm, jnp.stack(qg_l)], axis=1).astype(BF16)
    ws_qs = _bdot(lhs, s_old.astype(BF16))
    v_new = u - ws_qs[:, :C]
    vnb = v_new.astype(BF16)
    o = ws_qs[:, C:] + _bdot(jnp.stack(qkd_l), vnb)
    upd = _bdot_tn(jnp.stack(kd_l), vnb)
    for h in range(NH):
        s_ref[h] = s_old[h] * gcol(eglast, h) + upd[h]
        zh = z_ref[:, h * GDN_HEAD:(h + 1) * GDN_HEAD]
        o_ref[:, h * GDN_HEAD:(h + 1) * GDN_HEAD] = _gated_rmsnorm(o[h], zh, normw).astype(o_ref.dtype)

    @pl.when(c == pl.num_programs(1) - 1)
    def _():
        sfin_ref[0] = s_ref[...]


def _gdn_prompt(proj, ba, conv_w, a_log, dt_bias, norm_w, n_batch, lp, n_tokens):
    C = GDN_CHUNK
    per_b = lp // C
    n_chunks = per_b
    z_off = GDN_CONV_DIM // GDN_VALUE_DIM
    kern = functools.partial(_gdn_prompt_kernel, n_tokens)
    return pl.pallas_call(
        kern,
        grid=(n_batch, n_chunks),
        in_specs=[pl.BlockSpec((C, GDN_CONV_DIM), lambda b, c: (b * per_b + c, 0)),
                  pl.BlockSpec((C, GDN_VALUE_DIM), lambda b, c: (b * per_b + c, z_off)),
                  pl.BlockSpec((C, LANES), lambda b, c: (b * per_b + c, 0)),
                  pl.BlockSpec((GDN_CONV_WIDTH, GDN_CONV_DIM), lambda b, c: (0, 0)),
                  pl.BlockSpec((1, LANES), lambda b, c: (0, 0)),
                  pl.BlockSpec((1, LANES), lambda b, c: (0, 0)),
                  pl.BlockSpec((1, GDN_HEAD), lambda b, c: (0, 0))],
        out_specs=[pl.BlockSpec((C, GDN_VALUE_DIM), lambda b, c: (b * per_b + c, 0)),
                   pl.BlockSpec((1, GDN_V_HEADS, GDN_HEAD, GDN_HEAD), lambda b, c: (b, 0, 0, 0))],
        out_shape=[jax.ShapeDtypeStruct((n_batch * lp, GDN_VALUE_DIM), BF16),
                   jax.ShapeDtypeStruct((n_batch, GDN_V_HEADS, GDN_HEAD, GDN_HEAD), F32)],
        scratch_shapes=[pltpu.VMEM((SUBLANES + C, GDN_CONV_DIM), F32),
                        pltpu.VMEM((GDN_V_HEADS, GDN_HEAD, GDN_HEAD), F32)],
        compiler_params=_cparams(("parallel", "arbitrary")),
        name="gdn_prompt",
    )(proj, proj, ba, conv_w, a_log, dt_bias, norm_w)


def _twice(dst_ref, x):
    t = x.shape[0]
    dst_ref[0:t, :] = x
    dst_ref[t:2 * t, :] = x
    return dst_ref[...]


def _gdn_sample_kernel(cs_ref, mixed_ref, z_ref, ba_ref, convw_ref, alog_ref, dtb_ref, normw_ref, s_ref,
                       o_ref, snew_ref, xc_ref, dq_ref, dz_ref, dba_ref):
    T = mixed_ref.shape[1]
    R = 2 * T
    W = GDN_CONV_WIDTH
    xc_ref[0:W - 1, :] = cs_ref[0]
    xc_ref[W - 1:W - 1 + T, :] = mixed_ref[0]
    w = convw_ref[...]
    acc = xc_ref[0:T, :] * w[0:1, :]
    for j in range(1, W):
        acc = acc + xc_ref[j:j + T, :] * w[j:j + 1, :]
    qkv = _twice(dq_ref, _silu(acc))
    z2 = _twice(dz_ref, z_ref[0])
    ba2 = _twice(dba_ref, ba_ref[0])
    beta, g = _gdn_gates(ba2, alog_ref[...], dtb_ref[...])

    rr = lax.broadcasted_iota(jnp.int32, (R, 1), 0)
    tmod = rr % T
    first = rr < T
    gc = jnp.zeros_like(g)
    for s in range(T):
        gc = gc + jnp.where(tmod >= s, g[s:s + 1, :], 0.0)
    glast = gc[T - 1:T, :]
    normw = normw_ref[...]

    NP = GDN_K_HEADS
    M = NP * R

    def pair_col(x, kh):
        return jnp.where(first, x[:, 2 * kh:2 * kh + 1], x[:, 2 * kh + 1:2 * kh + 2])

    def pair_tile(x, base, kh):
        a = x[:, base + (2 * kh) * GDN_HEAD:base + (2 * kh + 1) * GDN_HEAD]
        b = x[:, base + (2 * kh + 1) * GDN_HEAD:base + (2 * kh + 2) * GDN_HEAD]
        return jnp.where(first, a, b)

    q_t, k_t, v_t, z_t, b_c, gc_c, gl_c = [], [], [], [], [], [], []
    for kh in range(NP):
        q_t.append(_l2norm_rows(qkv[:, kh * GDN_HEAD:(kh + 1) * GDN_HEAD]) * (GDN_HEAD ** -0.5))
        k_t.append(_l2norm_rows(qkv[:, GDN_KEY_DIM + kh * GDN_HEAD:GDN_KEY_DIM + (kh + 1) * GDN_HEAD]))
        v_t.append(pair_tile(qkv, 2 * GDN_KEY_DIM, kh))
        z_t.append(pair_tile(z2, 0, kh))
        b_c.append(pair_col(beta, kh))
        gc_c.append(pair_col(gc, kh))
        gl_c.append(pair_col(jnp.broadcast_to(glast, (R, GDN_V_HEADS)), kh))
    q64 = jnp.concatenate(q_t, axis=0)
    k64 = jnp.concatenate(k_t, axis=0)
    v64 = jnp.concatenate(v_t, axis=0)
    bcol = jnp.concatenate(b_c, axis=0)
    gcol = jnp.concatenate(gc_c, axis=0)
    glcol = jnp.concatenate(gl_c, axis=0)

    ri = lax.broadcasted_iota(jnp.int32, (M, M), 0)
    ci = lax.broadcasted_iota(jnp.int32, (M, M), 1)
    same = (ri // T) == (ci // T)
    tril = same & ((ci % T) <= (ri % T))
    strict = same & ((ci % T) < (ri % T))
    kb = k64.astype(BF16)
    kk = _dot_nt(kb, kb)
    qk0 = _dot_nt(q64.astype(BF16), kb)
    diff = _pairwise_diff(gcol)
    decay = jnp.where(tril, jnp.exp(jnp.where(tril, diff, 0.0)), 0.0)
    a_mat = jnp.where(strict, kk * decay * bcol, 0.0)
    t_inv = _unit_lower_inverse(a_mat, max(1, (T - 1).bit_length())).astype(BF16)
    egc = jnp.exp(gcol)
    u = _dot(t_inv, (v64 * bcol).astype(BF16))
    wm = _dot(t_inv, (k64 * (bcol * egc)).astype(BF16))
    qg = q64 * egc
    kd = k64 * jnp.exp(glcol - gcol)
    qkd = (qk0 * decay).astype(BF16)

    v_new_t, qs_t = [], []
    for kh in range(NP):
        lhs = jnp.concatenate([wm[kh * R:(kh + 1) * R], qg[kh * R:(kh + 1) * R]], axis=0).astype(BF16)
        r0 = _dot(lhs, s_ref[0, 2 * kh].astype(BF16))
        r1 = _dot(lhs, s_ref[0, 2 * kh + 1].astype(BF16))
        ws = jnp.where(first, r0[:R], r1[:R])
        qs_t.append(jnp.where(first, r0[R:], r1[R:]))
        v_new_t.append(u[kh * R:(kh + 1) * R] - ws)
    v_new = jnp.concatenate(v_new_t, axis=0)
    o64 = jnp.concatenate(qs_t, axis=0) + _dot(qkd, v_new.astype(BF16))
    on = _gated_rmsnorm(o64, jnp.concatenate(z_t, axis=0), normw)

    for kh in range(NP):
        vn = v_new_t[kh].astype(BF16)
        kdp = kd[kh * R:(kh + 1) * R]
        for j in range(2):
            h = 2 * kh + j
            keep = first if j == 0 else jnp.logical_not(first)
            kdm = jnp.where(keep, kdp, 0.0).astype(BF16)
            eg = jnp.exp(glast[:, h:h + 1])
            snew_ref[0, h] = s_ref[0, h] * eg + _dot_tn(kdm, vn)
            tile = on[kh * R:(kh + 1) * R]
            if j == 1:
                tile = pltpu.roll(tile, T, axis=0)
            o_ref[0, :, h * GDN_HEAD:(h + 1) * GDN_HEAD] = tile[:T].astype(o_ref.dtype)


def _gdn_sample(conv_state, mixed, z, ba, conv_w, a_log, dt_bias, norm_w, state):
    nb, t, _ = mixed.shape
    return pl.pallas_call(
        _gdn_sample_kernel,
        grid=(nb,),
        in_specs=[pl.BlockSpec((1, GDN_CONV_WIDTH - 1, GDN_CONV_DIM), lambda b: (b, 0, 0)),
                  pl.BlockSpec((1, t, GDN_CONV_DIM), lambda b: (b, 0, 0)),
                  pl.BlockSpec((1, t, GDN_VALUE_DIM), lambda b: (b, 0, 0)),
                  pl.BlockSpec((1, t, 2 * GDN_V_HEADS), lambda b: (b, 0, 0)),
                  pl.BlockSpec((GDN_CONV_WIDTH, GDN_CONV_DIM), lambda b: (0, 0)),
                  pl.BlockSpec((1, GDN_V_HEADS), lambda b: (0, 0)),
                  pl.BlockSpec((1, GDN_V_HEADS), lambda b: (0, 0)),
                  pl.BlockSpec((1, GDN_HEAD), lambda b: (0, 0)),
                  pl.BlockSpec((1, GDN_V_HEADS, GDN_HEAD, GDN_HEAD), lambda b: (b, 0, 0, 0))],
        out_specs=[pl.BlockSpec((1, t, GDN_VALUE_DIM), lambda b: (b, 0, 0)),
                   pl.BlockSpec((1, GDN_V_HEADS, GDN_HEAD, GDN_HEAD), lambda b: (b, 0, 0, 0))],
        out_shape=[jax.ShapeDtypeStruct((nb, t, GDN_VALUE_DIM), BF16),
                   jax.ShapeDtypeStruct(state.shape, F32)],
        scratch_shapes=[pltpu.VMEM((GDN_CONV_WIDTH - 1 + t, GDN_CONV_DIM), F32),
                        pltpu.VMEM((2 * t, GDN_CONV_DIM), F32),
                        pltpu.VMEM((2 * t, GDN_VALUE_DIM), F32),
                        pltpu.VMEM((2 * t, 2 * GDN_V_HEADS), F32)],
        compiler_params=_cparams(("parallel",)),
        name="gdn_sample",
    )(conv_state, mixed, z, ba, conv_w, a_log, dt_bias, norm_w, state)


def _rope_tables(pos, half, period):
    inv_freq = ROPE_THETA ** (-jnp.arange(half, dtype=F32) * 2.0 / (2 * half))
    ang = pos.astype(F32)[:, None] * inv_freq[None, :]
    cos, sin = jnp.cos(ang), jnp.sin(ang)
    ones = jnp.ones((pos.shape[0], period - 2 * half), F32)
    cos_p = jnp.concatenate([cos, cos, ones], axis=1)
    sin_p = jnp.concatenate([-sin, sin, 0.0 * ones], axis=1)
    reps = LANES // period
    return jnp.tile(cos_p, (1, reps)), jnp.tile(sin_p, (1, reps))


def _rope_tile(x, cos, sin, half, period):
    lane = lax.broadcasted_iota(jnp.int32, x.shape, 1) % period
    partner = jnp.where(lane < half, pltpu.roll(x, LANES - half, axis=1), pltpu.roll(x, half, axis=1))
    return x * cos + partner * sin


def _dsa_proj_kernel(x_ref, w_ref, ca_ref, sa_ref, ci_ref, si_ref, g_ref, b_ref,
                     q_ref, k_ref, v_ref, vt_ref, iq_ref, ikw_ref, ik2_ref):
    x = x_ref[...].astype(BF16)
    ca, sa, ci, si = ca_ref[...], sa_ref[...], ci_ref[...], si_ref[...]
    q_off, k_off = 0, ATT_HEADS * ATT_HEAD_DIM
    v_off = k_off + ATT_KV_HEADS * ATT_HEAD_DIM
    iq_off = v_off + ATT_KV_HEADS * ATT_HEAD_DIM
    ik_off = iq_off + IDX_HEADS * IDX_HEAD_DIM

    def proj(off):
        return _dot(x, w_ref[:, off:off + LANES])

    for h in range(ATT_HEADS):
        t = _rope_tile(proj(q_off + h * LANES), ca, sa, ATT_ROT_HALF, LANES)
        q_ref[:, h * LANES:(h + 1) * LANES] = (t * (ATT_HEAD_DIM ** -0.5)).astype(q_ref.dtype)
    for h in range(ATT_KV_HEADS):
        k_ref[:, h * LANES:(h + 1) * LANES] = _rope_tile(proj(k_off + h * LANES), ca, sa, ATT_ROT_HALF, LANES)
        v = proj(v_off + h * LANES)
        v_ref[:, h * LANES:(h + 1) * LANES] = v
        vt_ref[h * LANES:(h + 1) * LANES, :] = v.T.astype(vt_ref.dtype)
    for h in range(IDX_HEADS * IDX_HEAD_DIM // LANES):
        t = _rope_tile(proj(iq_off + h * LANES), ci, si, IDX_ROT_HALF, IDX_HEAD_DIM)
        iq_ref[:, h * LANES:(h + 1) * LANES] = t.astype(iq_ref.dtype)
    t = proj(ik_off)
    lane = lax.broadcasted_iota(jnp.int32, t.shape, 1)
    is_ik = lane < IDX_HEAD_DIM
    mu = jnp.sum(jnp.where(is_ik, t, 0.0), axis=-1, keepdims=True) / IDX_HEAD_DIM
    tc = jnp.where(is_ik, t - mu, 0.0)
    var = jnp.sum(tc * tc, axis=-1, keepdims=True) / IDX_HEAD_DIM
    ik = _rope_tile(tc * lax.rsqrt(var + LN_EPS) * g_ref[...] + b_ref[...], ci, si, IDX_ROT_HALF, IDX_HEAD_DIM)
    ikw = jnp.where(is_ik, ik, t * (IDX_HEADS ** -0.5))
    ikw_ref[...] = ikw
    ik_only = jnp.where(is_ik, ik, 0.0)
    ik2_ref[...] = (ik_only + pltpu.roll(ik_only, IDX_HEAD_DIM, axis=1)).astype(ik2_ref.dtype)


def _dsa_proj(x, w, tables, ik_g, ik_b):
    rows, d = x.shape
    n = w.shape[1]
    kv = ATT_KV_HEADS * ATT_HEAD_DIM
    row_spec = lambda width: pl.BlockSpec((ROW_TILE, width), lambda i: (i, 0))
    const_spec = lambda shape: pl.BlockSpec(shape, lambda i: (0, 0))
    return pl.pallas_call(
        _dsa_proj_kernel,
        grid=(rows // ROW_TILE,),
        in_specs=[row_spec(d), const_spec((d, n))] + [row_spec(LANES)] * 4 + [const_spec((1, LANES))] * 2,
        out_specs=[row_spec(ATT_HEADS * ATT_HEAD_DIM), row_spec(kv), row_spec(kv),
                   pl.BlockSpec((kv, ROW_TILE), lambda i: (0, i)),
                   row_spec(IDX_HEADS * IDX_HEAD_DIM), row_spec(LANES), row_spec(LANES)],
        out_shape=[jax.ShapeDtypeStruct((rows, ATT_HEADS * ATT_HEAD_DIM), BF16),
                   jax.ShapeDtypeStruct((rows, kv), F32),
                   jax.ShapeDtypeStruct((rows, kv), F32),
                   jax.ShapeDtypeStruct((kv, rows), BF16),
                   jax.ShapeDtypeStruct((rows, IDX_HEADS * IDX_HEAD_DIM), BF16),
                   jax.ShapeDtypeStruct((rows, LANES), F32),
                   jax.ShapeDtypeStruct((rows, LANES), BF16)],
        compiler_params=_cparams(("parallel",)),
        name="dsa_proj",
    )(x, w, *tables, ik_g, ik_b)


def _float_order_key(x):
    b = pltpu.bitcast(x, jnp.int32)
    return b ^ ((b >> 31) & jnp.int32(0x7FFFFFFF))


def _float_from_key(k):
    return pltpu.bitcast(k ^ ((k >> 31) & jnp.int32(0x7FFFFFFF)), F32)


def _np_order_key(x):
    b = int(np.float32(x).view(np.int32))
    return b ^ ((b >> 31) & 0x7FFFFFFF)


_KEY_NEG_MAX = _np_order_key(-F32_MAX)
_KEY_POS_INF = _np_order_key(np.inf)


def _kth_largest_bounds(count_ge, target, shape):
    lo = jnp.full(shape, _KEY_NEG_MAX, jnp.int32)
    hi = jnp.full(shape, _KEY_POS_INF, jnp.int32)

    def body(_, carry):
        lo, hi = carry
        mid = (lo >> 1) + (hi >> 1) + (lo & hi & 1)
        ok = count_ge(_float_from_key(mid)) >= target
        return jnp.where(ok, mid, lo), jnp.where(ok, hi, mid)

    lo, hi = lax.fori_loop(0, 32, body, (lo, hi))
    return _float_from_key(lo), _float_from_key(hi)


def _dsa_prompt_kernel(topk, q_ref, iq_ref, ikw_ref, k_ref, vt_ref, ik2_ref, o_ref,
                       s_ref, xh_ref, qg_ref, sc_ref, acc_ref):
    i = pl.program_id(1)
    KT = KEY_TILE
    CH = KEY_CHUNK
    nc = i // (CH // KT) + 1
    row = lax.broadcasted_iota(jnp.int32, (CH, KT), 0)
    lane = lax.broadcasted_iota(jnp.int32, (KT, KT), 1)
    qpos = i * KT + lax.broadcasted_iota(jnp.int32, (1, KT), 1)
    GQ = ATT_GROUP * KT

    def chunk(c):
        return pl.ds(pl.multiple_of(c * CH, CH), CH)

    NACC = 8

    def fold_rows(x):
        return x.reshape(CH // (NACC * SUBLANES), NACC, SUBLANES, x.shape[-1])

    def unfold(x, op):
        return op(op(x, axis=0), axis=0, keepdims=True)

    for h in range(IDX_HEADS):
        tile = iq_ref[:, (h // 2) * LANES:(h // 2 + 1) * LANES]
        mine = (lane // IDX_HEAD_DIM) == (h % 2)
        xh_ref[h * KT:(h + 1) * KT, :] = jnp.where(mine, tile, jnp.zeros_like(tile))
    for g in range(ATT_KV_HEADS):
        for hq in range(ATT_GROUP):
            h = g * ATT_GROUP + hq
            qg_ref[g, hq * KT:(hq + 1) * KT, :] = q_ref[:, h * LANES:(h + 1) * LANES]
    w_rows = ikw_ref[...].T[IDX_HEAD_DIM:IDX_HEAD_DIM + IDX_HEADS, :]

    def score_chunk(c, _):
        d = _dot_nt(ik2_ref[chunk(c), :], xh_ref[...])
        acc = w_rows[0:1, :] * jnp.maximum(d[:, 0:KT], 0.0)
        for h in range(1, IDX_HEADS):
            acc = acc + w_rows[h:h + 1, :] * jnp.maximum(d[:, h * KT:(h + 1) * KT], 0.0)
        acc = acc * (IDX_HEAD_DIM ** -0.5)
        kpos = c * CH + row
        acc = jnp.where(kpos < N_META, jnp.inf, acc)
        s_ref[chunk(c), :] = jnp.where(kpos <= qpos, acc, NEG_INF)
        return 0

    lax.fori_loop(0, nc, score_chunk, 0)

    target = jnp.minimum(qpos + 1, topk).astype(F32)

    def count_ge(thr):
        def body(c, acc):
            hit = jnp.where(s_ref[chunk(c), :] >= thr, 1.0, 0.0)
            return acc + jnp.sum(fold_rows(hit), axis=0)
        return unfold(lax.fori_loop(0, nc, body, jnp.zeros((NACC, SUBLANES, KT), F32)), jnp.sum)

    lo, hi = _kth_largest_bounds(count_ge, target, (1, KT))
    surplus = count_ge(lo) - target

    @pl.when(jnp.max(surplus) > 0.0)
    def _():
        need = target - count_ge(hi)
        rr = lax.broadcasted_iota(jnp.int32, (CH, CH), 0)
        cc = lax.broadcasted_iota(jnp.int32, (CH, CH), 1)
        tri = (cc <= rr).astype(BF16)

        def drop(c, carry):
            t = s_ref[chunk(c), :]
            tie = (t >= lo) & jnp.logical_not(t >= hi)
            tie_f = jnp.where(tie, 1.0, 0.0)
            rank = carry + _dot(tri, tie_f.astype(BF16))
            s_ref[chunk(c), :] = jnp.where(tie & (rank > need), NEG_INF, t)
            return carry + jnp.sum(tie_f, axis=0, keepdims=True)

        lax.fori_loop(0, nc, drop, jnp.zeros((1, KT), F32))

    def bias_cols(c):
        b = jnp.where(s_ref[chunk(c), :] >= lo, 0.0, NEG_INF)
        return jnp.concatenate([b] * ATT_GROUP, axis=1)

    G = ATT_KV_HEADS

    def max_pass(c, macc):
        bias = bias_cols(c)
        tops = []
        for g in range(G):
            sc = _dot_nt(k_ref[chunk(c), g * LANES:(g + 1) * LANES], qg_ref[g]) + bias
            sc_ref[chunk(c), g * GQ:(g + 1) * GQ] = sc
            tops.append(jnp.max(fold_rows(sc), axis=0))
        return jnp.maximum(macc, jnp.concatenate(tops, axis=-1))

    macc = lax.fori_loop(0, nc, max_pass, jnp.full((NACC, SUBLANES, G * GQ), NEG_INF, F32))
    m = unfold(macc, jnp.max)

    ones_rows = jnp.ones((2 * SUBLANES, CH), BF16)
    acc_ref[...] = jnp.zeros_like(acc_ref)

    def sum_pass(c, _):
        for g in range(G):
            p = jnp.exp(sc_ref[chunk(c), g * GQ:(g + 1) * GQ] - m[:, g * GQ:(g + 1) * GQ]).astype(BF16)
            vt = jnp.concatenate([vt_ref[g * LANES:(g + 1) * LANES, chunk(c)], ones_rows], axis=0)
            acc_ref[g] += _dot(vt, p)
        return 0

    lax.fori_loop(0, nc, sum_pass, 0)
    for g in range(G):
        acc = acc_ref[g]
        out_t = acc[:ATT_HEAD_DIM] / acc[ATT_HEAD_DIM:ATT_HEAD_DIM + 1]
        for hq in range(ATT_GROUP):
            h = g * ATT_GROUP + hq
            o_ref[:, h * LANES:(h + 1) * LANES] = out_t[:, hq * KT:(hq + 1) * KT].T.astype(o_ref.dtype)


def _dsa_prompt(q, iq, ikw, k, vt, ik2, n_batch, lp, topk):
    KT = KEY_TILE
    nq = lp // KT
    kv = ATT_KV_HEADS * ATT_HEAD_DIM
    qspec = lambda width: pl.BlockSpec((KT, width), lambda b, i: (b * nq + i, 0))
    return pl.pallas_call(
        functools.partial(_dsa_prompt_kernel, topk),
        grid=(n_batch, nq),
        in_specs=[qspec(ATT_HEADS * ATT_HEAD_DIM), qspec(IDX_HEADS * IDX_HEAD_DIM), qspec(LANES),
                  pl.BlockSpec((lp, kv), lambda b, i: (b, 0)),
                  pl.BlockSpec((kv, lp), lambda b, i: (0, b)),
                  pl.BlockSpec((lp, LANES), lambda b, i: (b, 0))],
        out_specs=qspec(ATT_HEADS * ATT_HEAD_DIM),
        out_shape=jax.ShapeDtypeStruct((n_batch * lp, ATT_HEADS * ATT_HEAD_DIM), BF16),
        scratch_shapes=[pltpu.VMEM((lp, KT), F32),
                        pltpu.VMEM((IDX_HEADS * KT, LANES), BF16),
                        pltpu.VMEM((ATT_KV_HEADS, ATT_GROUP * KT, ATT_HEAD_DIM), BF16),
                        pltpu.VMEM((lp, ATT_HEADS * KT), F32),
                        pltpu.VMEM((ATT_KV_HEADS, ATT_HEAD_DIM + 2 * SUBLANES, ATT_GROUP * KT), F32)],
        compiler_params=_cparams(("parallel", "arbitrary")),
        name="dsa_prompt",
    )(q, iq, ikw, k, vt, ik2)


def _dsa_sample_kernel(topk, pt_ref, q_ref, iq_ref, ikw_ref, kn_ref, vn_ref, cik_ref, ck_ref, cv_ref, o_ref,
                       s_ref, kall_ref, vall_ref, q8_ref, iq8_ref, ikw8_ref, newk_ref):
    del pt_ref
    p = pl.program_id(1)
    n_pages = pl.num_programs(1)
    T = q_ref.shape[1]
    R = 2 * T
    past = n_pages * PAGE
    n_keys = s_ref.shape[1]

    @pl.when(p == 0)
    def _():
        q8_ref[0:T, :] = q_ref[0].astype(F32)
        q8_ref[T:R, :] = q_ref[0].astype(F32)
        iq8_ref[0:T, :] = iq_ref[0].astype(F32)
        iq8_ref[T:R, :] = iq_ref[0].astype(F32)
        ikw8_ref[0:T, :] = ikw_ref[0]
        ikw8_ref[T:R, :] = ikw_ref[0]

    lane = lax.broadcasted_iota(jnp.int32, (R, LANES), 1)
    ikw8 = ikw8_ref[...]

    def index_scores(keys2):
        acc = jnp.zeros((R, LANES), F32)
        for h in range(IDX_HEADS):
            tile = iq8_ref[:, (h // 2) * LANES:(h // 2 + 1) * LANES]
            xh = jnp.where((lane // IDX_HEAD_DIM) == (h % 2), tile, 0.0).astype(BF16)
            wcol = ikw8[:, IDX_HEAD_DIM + h:IDX_HEAD_DIM + h + 1]
            acc = acc + wcol * jnp.maximum(_dot_nt(xh, keys2), 0.0)
        return acc * (IDX_HEAD_DIM ** -0.5)

    ikp = cik_ref[0].astype(BF16)
    sc = index_scores(jnp.concatenate([ikp, ikp], axis=1))
    kpos = p * PAGE + lane
    page_sl = pl.ds(pl.multiple_of(p * PAGE, PAGE), PAGE)
    s_ref[:, page_sl] = jnp.where(kpos >= N_META, sc, NEG_INF)
    kall_ref[page_sl, :] = ck_ref[0]
    vall_ref[page_sl, :] = cv_ref[0]

    @pl.when(p == n_pages - 1)
    def _():
        kv = ATT_KV_HEADS * ATT_HEAD_DIM
        tmod = lax.broadcasted_iota(jnp.int32, (R, 1), 0) % T
        kall_ref[past:n_keys, :] = jnp.zeros((n_keys - past, kv), F32)
        vall_ref[past:n_keys, :] = jnp.zeros((n_keys - past, kv), F32)
        kall_ref[past:past + T, :] = kn_ref[0]
        vall_ref[past:past + T, :] = vn_ref[0]
        newk_ref[...] = jnp.zeros_like(newk_ref)
        newk_ref[0:T, :] = ikw_ref[0]
        nk = newk_ref[...]
        lane_k = lax.broadcasted_iota(jnp.int32, nk.shape, 1)
        nk = jnp.where(lane_k < IDX_HEAD_DIM, nk, 0.0)
        nk2 = (nk + pltpu.roll(nk, IDX_HEAD_DIM, axis=1)).astype(BF16)
        sc_new = index_scores(nk2)
        s_ref[:, past:n_keys] = jnp.where(lane <= tmod, sc_new, NEG_INF)

        qpos = past + tmod
        target = (jnp.minimum(qpos + 1, topk) - jnp.minimum(qpos + 1, N_META)).astype(F32)

        def count_ge(thr):
            return jnp.sum(jnp.where(s_ref[...] >= thr, 1.0, 0.0), axis=1, keepdims=True)

        lo, hi = _kth_largest_bounds(count_ge, target, (R, 1))
        need = target - count_ge(hi)
        row_t = lax.broadcasted_iota(jnp.int32, (LANES, LANES), 0)
        lane_t = lax.broadcasted_iota(jnp.int32, (LANES, LANES), 1)
        triu = (row_t <= lane_t).astype(BF16)
        carry = jnp.zeros((R, 1), F32)
        for kt in range(n_keys // LANES):
            t = s_ref[:, kt * LANES:(kt + 1) * LANES]
            above = t >= hi
            tie = (t >= lo) & jnp.logical_not(above)
            tie_f = jnp.where(tie, 1.0, 0.0)
            rank = carry + _dot(tie_f.astype(BF16), triu)
            kp = kt * LANES + lane
            keep = above | (tie & (rank <= need)) | (kp < N_META)
            s_ref[:, kt * LANES:(kt + 1) * LANES] = jnp.where(keep, 0.0, NEG_INF)
            carry = carry + jnp.sum(tie_f, axis=1, keepdims=True)

        bias = jnp.concatenate([s_ref[...]] * ATT_GROUP, axis=0)
        for g in range(ATT_KV_HEADS):
            qg = jnp.concatenate([q8_ref[:, (g * ATT_GROUP + hq) * LANES:(g * ATT_GROUP + hq + 1) * LANES]
                                  for hq in range(ATT_GROUP)], axis=0).astype(BF16)
            kg = kall_ref[:, g * LANES:(g + 1) * LANES].astype(BF16)
            vg = vall_ref[:, g * LANES:(g + 1) * LANES].astype(BF16)
            sc_a = _dot_nt(qg, kg) + bias
            m = jnp.max(sc_a, axis=1, keepdims=True)
            pr = jnp.exp(sc_a - m)
            out = _dot(pr.astype(BF16), vg) / jnp.sum(pr, axis=1, keepdims=True)
            for hq in range(ATT_GROUP):
                h = g * ATT_GROUP + hq
                o_ref[0, :, h * LANES:(h + 1) * LANES] = out[hq * R:hq * R + T].astype(o_ref.dtype)


def _dsa_sample(page_table, q, iq, ikw, k_new, v_new, cache_ik, cache_k, cache_v, topk):
    nb, t, _ = q.shape
    n_pages = page_table.shape[1]
    kv = ATT_KV_HEADS * ATT_HEAD_DIM
    n_keys = n_pages * PAGE + LANES
    tok = lambda width: pl.BlockSpec((1, t, width), lambda b, p, pt: (b, 0, 0))
    page = lambda width: pl.BlockSpec((1, PAGE, width), lambda b, p, pt: (pt[b, p], 0, 0))
    grid_spec = pltpu.PrefetchScalarGridSpec(
        num_scalar_prefetch=1,
        grid=(nb, n_pages),
        in_specs=[tok(ATT_HEADS * ATT_HEAD_DIM), tok(IDX_HEADS * IDX_HEAD_DIM), tok(LANES), tok(kv), tok(kv),
                  page(IDX_HEAD_DIM), page(kv), page(kv)],
        out_specs=tok(ATT_HEADS * ATT_HEAD_DIM),
        scratch_shapes=[pltpu.VMEM((2 * t, n_keys), F32),
                        pltpu.VMEM((n_keys, kv), F32),
                        pltpu.VMEM((n_keys, kv), F32),
                        pltpu.VMEM((2 * t, ATT_HEADS * ATT_HEAD_DIM), F32),
                        pltpu.VMEM((2 * t, IDX_HEADS * IDX_HEAD_DIM), F32),
                        pltpu.VMEM((2 * t, LANES), F32),
                        pltpu.VMEM((LANES, LANES), F32)])
    return pl.pallas_call(
        functools.partial(_dsa_sample_kernel, topk),
        grid_spec=grid_spec,
        out_shape=jax.ShapeDtypeStruct((nb, t, ATT_HEADS * ATT_HEAD_DIM), BF16),
        compiler_params=_cparams(("parallel", "arbitrary")),
        name="dsa_sample",
    )(page_table, q, iq, ikw, k_new, v_new, cache_ik, cache_k, cache_v)


def kernel(x_prompt, x_sample, state_gdn, state_gdn_conv, cache_k, cache_v, cache_idx_k, page_table,
           meta_tokens, ln1_g, ln1_b, ln2_g, ln2_b, mlp_w1, mlp_w2,
           gdn_w_in, gdn_conv_w, gdn_a_log, gdn_dt_bias, gdn_norm_w, gdn_w_out,
           dsa_w_in, dsa_ik_norm_g, dsa_ik_norm_b, dsa_w_o):
    nb, seq, d = x_prompt.shape
    ns, ts, _ = x_sample.shape
    n_tok = N_META + seq
    lp = -(-n_tok // KEY_TILE) * KEY_TILE
    n_prompt_rows = nb * lp
    rows = n_prompt_rows + ns * ts
    assert rows % ROW_TILE == 0 and lp % GDN_CHUNK == 0 and lp % KEY_CHUNK == 0 and d == D_MODEL
    kvd = ATT_KV_HEADS * ATT_HEAD_DIM

    meta = jnp.broadcast_to(meta_tokens[None].astype(x_prompt.dtype), (nb, N_META, d))
    pad = jnp.zeros((nb, lp - n_tok, d), x_prompt.dtype)
    h = jnp.concatenate([jnp.concatenate([meta, x_prompt, pad], 1).reshape(n_prompt_rows, d),
                         x_sample.reshape(ns * ts, d)], 0)

    def row2(x):
        return x.reshape(1, -1)

    def prompt_rows(x, width):
        return x[:n_prompt_rows].reshape(nb, lp, -1)[:, :n_tok, :width]

    w_in = gdn_w_in[0]
    split = GDN_CONV_DIM + GDN_VALUE_DIM
    proj = _matmul(h, w_in[:, :split].astype(BF16), 1024)
    n_gate = 2 * GDN_V_HEADS
    ba = _matmul(h, jnp.pad(w_in[:, split:], ((0, 0), (0, LANES - n_gate))).astype(BF16), LANES)
    decay_lanes = lambda v: jnp.pad(row2(v), ((0, 0), (GDN_V_HEADS, LANES - n_gate)))
    a_p, gdn_state_prompt = _gdn_prompt(proj, ba, gdn_conv_w[0], decay_lanes(gdn_a_log[0]),
                                        decay_lanes(gdn_dt_bias[0]), row2(gdn_norm_w[0]), nb, lp, n_tok)
    proj_s = proj[n_prompt_rows:].reshape(ns, ts, split)
    mixed_s = proj_s[:, :, :GDN_CONV_DIM]
    a_s, gdn_state_sample = _gdn_sample(state_gdn_conv[0], mixed_s, proj_s[:, :, GDN_CONV_DIM:],
                                        ba[n_prompt_rows:, :n_gate].reshape(ns, ts, n_gate), gdn_conv_w[0],
                                        row2(gdn_a_log[0]), row2(gdn_dt_bias[0]), row2(gdn_norm_w[0]),
                                        state_gdn[0])
    a = jnp.concatenate([a_p, a_s.reshape(ns * ts, -1)], 0)
    h = _outproj_ln(a, h, gdn_w_out[0].astype(BF16), row2(ln1_g[0]), row2(ln1_b[0]))
    h = _mlp_ln(h, mlp_w1[0].astype(BF16), mlp_w2[0].astype(BF16), row2(ln2_g[0]), row2(ln2_b[0]))
    keep = GDN_CONV_WIDTH - 1
    gdn_conv_prompt = prompt_rows(proj, GDN_CONV_DIM)[:, n_tok - keep:]
    gdn_conv_sample = jnp.concatenate([state_gdn_conv[0], mixed_s], 1)[:, -keep:]

    past = page_table.shape[1] * PAGE
    pos = jnp.concatenate([jnp.tile(jnp.arange(lp, dtype=jnp.int32), nb),
                           jnp.tile(past + jnp.arange(ts, dtype=jnp.int32), ns)])
    tables = _rope_tables(pos, ATT_ROT_HALF, LANES) + _rope_tables(pos, IDX_ROT_HALF, IDX_HEAD_DIM)
    n_in = dsa_w_in.shape[2]
    w_dsa = jnp.pad(dsa_w_in[0], ((0, 0), (0, -n_in % LANES))).astype(BF16)
    pad_lanes = lambda v: row2(jnp.pad(v, (0, LANES - v.shape[0])))
    q, k, v, vt, iq, ikw, ik2 = _dsa_proj(h, w_dsa, tables, pad_lanes(dsa_ik_norm_g[0]),
                                          pad_lanes(dsa_ik_norm_b[0]))
    o_p = _dsa_prompt(q, iq, ikw, k.astype(BF16), vt, ik2, nb, lp, min(TOPK_MAX, (n_tok - N_META) // 4))
    smp = lambda x: x[n_prompt_rows:].reshape(ns, ts, -1)
    n_pool = cache_k.shape[1]
    o_s = _dsa_sample(page_table, smp(q), smp(iq), smp(ikw), smp(k), smp(v),
                      cache_idx_k[0], cache_k[0].reshape(n_pool, PAGE, kvd), cache_v[0].reshape(n_pool, PAGE, kvd),
                      min(TOPK_MAX, (past + ts) // 4))
    a = jnp.concatenate([o_p, o_s.reshape(ns * ts, -1)], 0)
    h = _outproj_ln(a, h, dsa_w_o[0].astype(BF16), row2(ln1_g[1]), row2(ln1_b[1]))
    h = _mlp_ln(h, mlp_w1[1].astype(BF16), mlp_w2[1].astype(BF16), row2(ln2_g[1]), row2(ln2_b[1]))

    heads = lambda x: x.reshape(x.shape[:-1] + (ATT_KV_HEADS, ATT_HEAD_DIM))
    y_prompt = h[:n_prompt_rows].reshape(nb, lp, d)[:, N_META:n_tok]
    y_sample = h[n_prompt_rows:].reshape(ns, ts, d)
    return (y_prompt, y_sample,
            gdn_state_prompt[None], gdn_conv_prompt[None], gdn_state_sample[None], gdn_conv_sample[None],
            heads(prompt_rows(k, kvd))[None], heads(prompt_rows(v, kvd))[None],
            prompt_rows(ikw, IDX_HEAD_DIM)[None],
            heads(smp(k))[None], heads(smp(v))[None], smp(ikw)[:, :, :IDX_HEAD_DIM][None])
```

```python
import functools
import math

import jax
import jax.numpy as jnp
import numpy as np
from jax import lax
from jax.experimental import pallas as pl
from jax.experimental.pallas import tpu as pltpu

F32 = jnp.float32
BF16 = jnp.bfloat16
HIGHEST = lax.Precision.HIGHEST

D_MODEL = 1024
N_META = 16
DEPTH = 2
LN_EPS = 1e-5
DEEPNORM_ALPHA = (2 * DEPTH) ** 0.25
GDN_K_HEADS = 8
GDN_V_HEADS = 16
GDN_HEAD = 128
GDN_KEY_DIM = GDN_K_HEADS * GDN_HEAD
GDN_VALUE_DIM = GDN_V_HEADS * GDN_HEAD
GDN_CONV_DIM = 2 * GDN_KEY_DIM + GDN_VALUE_DIM
GDN_CONV_WIDTH = 4
GDN_CHUNK = 64
L2_EPS = 1e-6
RMS_EPS = 1e-6
ATT_HEADS = 8
ATT_KV_HEADS = 2
ATT_HEAD_DIM = 128
ATT_GROUP = ATT_HEADS // ATT_KV_HEADS
IDX_HEADS = 8
IDX_HEAD_DIM = 64
TOPK_MAX = 256
ROPE_THETA = 500000.0
ATT_ROT_HALF = ATT_HEAD_DIM // 8
IDX_ROT_HALF = IDX_HEAD_DIM // 8
PAGE = 128

LANES = 128
SUBLANES = 8
ROW_TILE = 512
KEY_TILE = 128
KEY_CHUNK = 3 * KEY_TILE
VMEM_LIMIT = 56 * 1024 * 1024

NEG_INF = float("-inf")
F32_MAX = float(np.finfo(np.float32).max)


def _cparams(sem):
    return pltpu.CompilerParams(dimension_semantics=sem, vmem_limit_bytes=VMEM_LIMIT)


def _dot(a, b):
    return jnp.dot(a, b, preferred_element_type=F32)


def _dot_nt(a, b, precision=None):
    return lax.dot_general(a, b, (((1,), (1,)), ((), ())), preferred_element_type=F32,
                           precision=precision)


def _dot_tn(a, b):
    return lax.dot_general(a, b, (((0,), (0,)), ((), ())), preferred_element_type=F32)


def _layernorm_rows(x, g, b):
    mu = jnp.mean(x, axis=-1, keepdims=True)
    xc = x - mu
    var = jnp.mean(xc * xc, axis=-1, keepdims=True)
    return xc * lax.rsqrt(var + LN_EPS) * g + b


def _sigmoid(x):
    return 1.0 / (1.0 + jnp.exp(-x))


def _silu(x):
    return x * _sigmoid(x)


def _softplus(x):
    return jnp.maximum(x, 0.0) + jnp.log(1.0 + jnp.exp(-jnp.abs(x)))


def _matmul_kernel(x_ref, w_ref, o_ref):
    o_ref[...] = _dot(x_ref[...].astype(BF16), w_ref[...]).astype(o_ref.dtype)


def _matmul(x, w, tn):
    rows, k = x.shape
    n = w.shape[1]
    return pl.pallas_call(
        _matmul_kernel,
        grid=(rows // ROW_TILE, n // tn),
        in_specs=[pl.BlockSpec((ROW_TILE, k), lambda i, j: (i, 0)),
                  pl.BlockSpec((k, tn), lambda i, j: (0, j))],
        out_specs=pl.BlockSpec((ROW_TILE, tn), lambda i, j: (i, j)),
        out_shape=jax.ShapeDtypeStruct((rows, n), F32),
        compiler_params=_cparams(("parallel", "arbitrary")),
        name="matmul",
    )(x, w)


def _outproj_ln_kernel(a_ref, x_ref, w_ref, g_ref, b_ref, o_ref):
    y = DEEPNORM_ALPHA * x_ref[...] + _dot(a_ref[...], w_ref[...])
    o_ref[...] = _layernorm_rows(y, g_ref[...], b_ref[...])


def _outproj_ln(a, x, w, g, b):
    rows, k = a.shape
    d = x.shape[1]
    return pl.pallas_call(
        _outproj_ln_kernel,
        grid=(rows // ROW_TILE,),
        in_specs=[pl.BlockSpec((ROW_TILE, k), lambda i: (i, 0)),
                  pl.BlockSpec((ROW_TILE, d), lambda i: (i, 0)),
                  pl.BlockSpec((k, d), lambda i: (0, 0)),
                  pl.BlockSpec((1, d), lambda i: (0, 0)),
                  pl.BlockSpec((1, d), lambda i: (0, 0))],
        out_specs=pl.BlockSpec((ROW_TILE, d), lambda i: (i, 0)),
        out_shape=jax.ShapeDtypeStruct((rows, d), F32),
        compiler_params=_cparams(("parallel",)),
        name="outproj_ln",
    )(a, x, w, g, b)


def _mlp_ln_kernel(x_ref, w1_ref, w2_ref, g_ref, b_ref, o_ref):
    x = x_ref[...]
    h = jnp.maximum(_dot(x.astype(BF16), w1_ref[...]), 0.0)
    y = DEEPNORM_ALPHA * x + _dot((h * h).astype(BF16), w2_ref[...])
    o_ref[...] = _layernorm_rows(y, g_ref[...], b_ref[...])


def _mlp_ln(x, w1, w2, g, b):
    rows, d = x.shape
    f = w1.shape[1]
    return pl.pallas_call(
        _mlp_ln_kernel,
        grid=(rows // ROW_TILE,),
        in_specs=[pl.BlockSpec((ROW_TILE, d), lambda i: (i, 0)),
                  pl.BlockSpec((d, f), lambda i: (0, 0)),
                  pl.BlockSpec((f, d), lambda i: (0, 0)),
                  pl.BlockSpec((1, d), lambda i: (0, 0)),
                  pl.BlockSpec((1, d), lambda i: (0, 0))],
        out_specs=pl.BlockSpec((ROW_TILE, d), lambda i: (i, 0)),
        out_shape=jax.ShapeDtypeStruct((rows, d), F32),
        compiler_params=_cparams(("parallel",)),
        name="mlp_ln",
    )(x, w1, w2, g, b)


def _unit_lower_inverse(a, n_factors):
    c = a.shape[0]
    row = lax.broadcasted_iota(jnp.int32, (c, c), 0)
    col = lax.broadcasted_iota(jnp.int32, (c, c), 1)
    eye = (row == col).astype(F32)
    p = -a
    t = eye + p
    for _ in range(n_factors - 1):
        pb = p.astype(BF16)
        p = _dot(pb, pb)
        t = t + _dot(t.astype(BF16), p.astype(BF16))
    return t


def _bdot(a, b):
    return lax.dot_general(a, b, (((2,), (1,)), ((0,), (0,))), preferred_element_type=F32)


def _bdot_nt(a, b):
    return lax.dot_general(a, b, (((2,), (2,)), ((0,), (0,))), preferred_element_type=F32)


def _bdot_tn(a, b):
    return lax.dot_general(a, b, (((1,), (1,)), ((0,), (0,))), preferred_element_type=F32)


def _unit_lower_inverse_batched(a, n_factors):
    c = a.shape[-1]
    row = lax.broadcasted_iota(jnp.int32, (c, c), 0)
    col = lax.broadcasted_iota(jnp.int32, (c, c), 1)
    p = -a
    t = (row == col).astype(F32) + p
    for _ in range(n_factors - 1):
        pb = p.astype(BF16)
        p = _bdot(pb, pb)
        t = t + _bdot(t.astype(BF16), p.astype(BF16))
    return t


def _pairwise_diff(col):
    c = col.shape[0]
    lane = lax.broadcasted_iota(jnp.int32, (c, LANES), 1)
    left = jnp.where(lane == 0, col, jnp.where(lane == 1, 1.0, 0.0))
    right = jnp.where(lane == 0, 1.0, jnp.where(lane == 1, -col, 0.0))
    return _dot_nt(left, right, precision=HIGHEST)


def _l2norm_rows(x):
    return x * lax.rsqrt(jnp.sum(x * x, axis=-1, keepdims=True) + L2_EPS)


def _gated_rmsnorm(o, z, norm_w):
    on = o * lax.rsqrt(jnp.mean(o * o, axis=-1, keepdims=True) + RMS_EPS) * norm_w
    return on * _silu(z)


def _gdn_gates(ba, a_log, dt_bias):
    beta = _sigmoid(ba[:, :GDN_V_HEADS])
    g = -jnp.exp(a_log) * _softplus(ba[:, GDN_V_HEADS:] + dt_bias)
    return beta, g


def _gdn_prompt_kernel(n_tokens, mixed_ref, z_ref, ba_ref, convw_ref, alog_ref, dtb_ref, normw_ref,
                       o_ref, sfin_ref, xc_ref, s_ref):
    c = pl.program_id(1)
    C = GDN_CHUNK
    halo = SUBLANES

    @pl.when(c == 0)
    def _():
        xc_ref[0:halo, :] = jnp.zeros((halo, GDN_CONV_DIM), F32)
        s_ref[...] = jnp.zeros_like(s_ref)

    xc_ref[halo:halo + C, :] = mixed_ref[...]
    w = convw_ref[...]
    acc = xc_ref[halo - 3:halo - 3 + C, :] * w[0:1, :]
    for j in range(1, GDN_CONV_WIDTH):
        acc = acc + xc_ref[halo - 3 + j:halo - 3 + j + C, :] * w[j:j + 1, :]
    qkv = _silu(acc)
    xc_ref[halo - 3:halo, :] = xc_ref[halo + C - 3:halo + C, :]

    row = lax.broadcasted_iota(jnp.int32, (C, 1), 0)
    valid = (c * C + row) < n_tokens
    ba = ba_ref[...]
    beta = jnp.where(valid, _sigmoid(ba), 0.0)
    g = jnp.where(valid, -jnp.exp(alog_ref[...]) * _softplus(ba + dtb_ref[...]), 0.0)

    ri = lax.broadcasted_iota(jnp.int32, (C, C), 0)
    ci = lax.broadcasted_iota(jnp.int32, (C, C), 1)
    tril = ci <= ri
    strict = ci < ri
    gc = jnp.dot(tril.astype(F32), g, preferred_element_type=F32, precision=HIGHEST)
    gc_rows = gc.T
    egc = jnp.exp(gc)
    glast = gc[C - 1:C, :]
    ekd = jnp.exp(glast - gc)
    eglast = jnp.exp(glast)
    normw = normw_ref[...]
    NH = GDN_V_HEADS

    def gcol(x, h):
        return x[:, NH + h:NH + h + 1]

    q_l = [_l2norm_rows(qkv[:, kh * GDN_HEAD:(kh + 1) * GDN_HEAD]) * (GDN_HEAD ** -0.5)
           for kh in range(GDN_K_HEADS)]
    k_l = [_l2norm_rows(qkv[:, GDN_KEY_DIM + kh * GDN_HEAD:GDN_KEY_DIM + (kh + 1) * GDN_HEAD])
           for kh in range(GDN_K_HEADS)]
    k8 = jnp.stack(k_l).astype(BF16)
    kk8 = _bdot_nt(k8, k8)
    qk8 = _bdot_nt(jnp.stack(q_l).astype(BF16), k8)
    a_l, qkd_l, rhs_l, qg_l, kd_l = [], [], [], [], []
    for h in range(NH):
        kh = h // 2
        v = qkv[:, 2 * GDN_KEY_DIM + h * GDN_HEAD:2 * GDN_KEY_DIM + (h + 1) * GDN_HEAD]
        bcol = beta[:, h:h + 1]
        diff = gcol(gc, h) - gc_rows[NH + h:NH + h + 1, :]
        decay = jnp.where(tril, jnp.exp(jnp.where(tril, diff, 0.0)), 0.0)
        a_l.append(jnp.where(strict, kk8[kh] * decay * bcol, 0.0))
        qkd_l.append((qk8[kh] * decay).astype(BF16))
        rhs_l.append(jnp.concatenate([v * bcol, k_l[kh] * (bcol * gcol(egc, h))], axis=1).astype(BF16))
        qg_l.append(q_l[kh] * gcol(egc, h))
        kd_l.append((k_l[kh] * gcol(ekd, h)).astype(BF16))
    t_inv = _unit_lower_inverse_batched(jnp.stack(a_l), 6).astype(BF16)
    sol = _bdot(t_inv, jnp.stack(rhs_l))
    u, wm = sol[:, :, :GDN_HEAD], sol[:, :, GDN_HEAD:]
    s_old = s_ref[...]
    lhs = jnp.concatenate([wm, jnp.stack(qg_l)], axis=1).astype(BF16)
    ws_qs = _bdot(lhs, s_old.astype(BF16))
    v_new = u - ws_qs[:, :C]
    vnb = v_new.astype(BF16)
    o = ws_qs[:, C:] + _bdot(jnp.stack(qkd_l), vnb)
    upd = _bdot_tn(jnp.stack(kd_l), vnb)
    for h in range(NH):
        s_ref[h] = s_old[h] * gcol(eglast, h) + upd[h]
        zh = z_ref[:, h * GDN_HEAD:(h + 1) * GDN_HEAD]
        o_ref[:, h * GDN_HEAD:(h + 1) * GDN_HEAD] = _gated_rmsnorm(o[h], zh, normw).astype(o_ref.dtype)

    @pl.when(c == pl.num_programs(1) - 1)
    def _():
        sfin_ref[0] = s_ref[...]


def _gdn_prompt(proj, ba, conv_w, a_log, dt_bias, norm_w, n_batch, lp, n_tokens):
    C = GDN_CHUNK
    per_b = lp // C
    n_chunks = per_b
    z_off = GDN_CONV_DIM // GDN_VALUE_DIM
    kern = functools.partial(_gdn_prompt_kernel, n_tokens)
    return pl.pallas_call(
        kern,
        grid=(n_batch, n_chunks),
        in_specs=[pl.BlockSpec((C, GDN_CONV_DIM), lambda b, c: (b * per_b + c, 0)),
                  pl.BlockSpec((C, GDN_VALUE_DIM), lambda b, c: (b * per_b + c, z_off)),
                  pl.BlockSpec((C, LANES), lambda b, c: (b * per_b + c, 0)),
                  pl.BlockSpec((GDN_CONV_WIDTH, GDN_CONV_DIM), lambda b, c: (0, 0)),
                  pl.BlockSpec((1, LANES), lambda b, c: (0, 0)),
                  pl.BlockSpec((1, LANES), lambda b, c: (0, 0)),
                  pl.BlockSpec((1, GDN_HEAD), lambda b, c: (0, 0))],
        out_specs=[pl.BlockSpec((C, GDN_VALUE_DIM), lambda b, c: (b * per_b + c, 0)),
                   pl.BlockSpec((1, GDN_V_HEADS, GDN_HEAD, GDN_HEAD), lambda b, c: (b, 0, 0, 0))],
        out_shape=[jax.ShapeDtypeStruct((n_batch * lp, GDN_VALUE_DIM), BF16),
                   jax.ShapeDtypeStruct((n_batch, GDN_V_HEADS, GDN_HEAD, GDN_HEAD), F32)],
        scratch_shapes=[pltpu.VMEM((SUBLANES + C, GDN_CONV_DIM), F32),
                        pltpu.VMEM((GDN_V_HEADS, GDN_HEAD, GDN_HEAD), F32)],
        compiler_params=_cparams(("parallel", "arbitrary")),
        name="gdn_prompt",
    )(proj, proj, ba, conv_w, a_log, dt_bias, norm_w)


def _twice(dst_ref, x):
    t = x.shape[0]
    dst_ref[0:t, :] = x
    dst_ref[t:2 * t, :] = x
    return dst_ref[...]


def _gdn_sample_kernel(cs_ref, mixed_ref, z_ref, ba_ref, convw_ref, alog_ref, dtb_ref, normw_ref, s_ref,
                       o_ref, snew_ref, xc_ref, dq_ref, dz_ref, dba_ref):
    T = mixed_ref.shape[1]
    R = 2 * T
    W = GDN_CONV_WIDTH
    xc_ref[0:W - 1, :] = cs_ref[0]
    xc_ref[W - 1:W - 1 + T, :] = mixed_ref[0]
    w = convw_ref[...]
    acc = xc_ref[0:T, :] * w[0:1, :]
    for j in range(1, W):
        acc = acc + xc_ref[j:j + T, :] * w[j:j + 1, :]
    qkv = _twice(dq_ref, _silu(acc))
    z2 = _twice(dz_ref, z_ref[0])
    ba2 = _twice(dba_ref, ba_ref[0])
    beta, g = _gdn_gates(ba2, alog_ref[...], dtb_ref[...])

    rr = lax.broadcasted_iota(jnp.int32, (R, 1), 0)
    tmod = rr % T
    first = rr < T
    gc = jnp.zeros_like(g)
    for s in range(T):
        gc = gc + jnp.where(tmod >= s, g[s:s + 1, :], 0.0)
    glast = gc[T - 1:T, :]
    normw = normw_ref[...]

    NP = GDN_K_HEADS
    M = NP * R

    def pair_col(x, kh):
        return jnp.where(first, x[:, 2 * kh:2 * kh + 1], x[:, 2 * kh + 1:2 * kh + 2])

    def pair_tile(x, base, kh):
        a = x[:, base + (2 * kh) * GDN_HEAD:base + (2 * kh + 1) * GDN_HEAD]
        b = x[:, base + (2 * kh + 1) * GDN_HEAD:base + (2 * kh + 2) * GDN_HEAD]
        return jnp.where(first, a, b)

    q_t, k_t, v_t, z_t, b_c, gc_c, gl_c = [], [], [], [], [], [], []
    for kh in range(NP):
        q_t.append(_l2norm_rows(qkv[:, kh * GDN_HEAD:(kh + 1) * GDN_HEAD]) * (GDN_HEAD ** -0.5))
        k_t.append(_l2norm_rows(qkv[:, GDN_KEY_DIM + kh * GDN_HEAD:GDN_KEY_DIM + (kh + 1) * GDN_HEAD]))
        v_t.append(pair_tile(qkv, 2 * GDN_KEY_DIM, kh))
        z_t.append(pair_tile(z2, 0, kh))
        b_c.append(pair_col(beta, kh))
        gc_c.append(pair_col(gc, kh))
        gl_c.append(pair_col(jnp.broadcast_to(glast, (R, GDN_V_HEADS)), kh))
    q64 = jnp.concatenate(q_t, axis=0)
    k64 = jnp.concatenate(k_t, axis=0)
    v64 = jnp.concatenate(v_t, axis=0)
    bcol = jnp.concatenate(b_c, axis=0)
    gcol = jnp.concatenate(gc_c, axis=0)
    glcol = jnp.concatenate(gl_c, axis=0)

    ri = lax.broadcasted_iota(jnp.int32, (M, M), 0)
    ci = lax.broadcasted_iota(jnp.int32, (M, M), 1)
    same = (ri // T) == (ci // T)
    tril = same & ((ci % T) <= (ri % T))
    strict = same & ((ci % T) < (ri % T))
    kb = k64.astype(BF16)
    kk = _dot_nt(kb, kb)
    qk0 = _dot_nt(q64.astype(BF16), kb)
    diff = _pairwise_diff(gcol)
    decay = jnp.where(tril, jnp.exp(jnp.where(tril, diff, 0.0)), 0.0)
    a_mat = jnp.where(strict, kk * decay * bcol, 0.0)
    t_inv = _unit_lower_inverse(a_mat, max(1, (T - 1).bit_length())).astype(BF16)
    egc = jnp.exp(gcol)
    u = _dot(t_inv, (v64 * bcol).astype(BF16))
    wm = _dot(t_inv, (k64 * (bcol * egc)).astype(BF16))
    qg = q64 * egc
    kd = k64 * jnp.exp(glcol - gcol)
    qkd = (qk0 * decay).astype(BF16)

    v_new_t, qs_t = [], []
    for kh in range(NP):
        lhs = jnp.concatenate([wm[kh * R:(kh + 1) * R], qg[kh * R:(kh + 1) * R]], axis=0).astype(BF16)
        r0 = _dot(lhs, s_ref[0, 2 * kh].astype(BF16))
        r1 = _dot(lhs, s_ref[0, 2 * kh + 1].astype(BF16))
        ws = jnp.where(first, r0[:R], r1[:R])
        qs_t.append(jnp.where(first, r0[R:], r1[R:]))
        v_new_t.append(u[kh * R:(kh + 1) * R] - ws)
    v_new = jnp.concatenate(v_new_t, axis=0)
    o64 = jnp.concatenate(qs_t, axis=0) + _dot(qkd, v_new.astype(BF16))
    on = _gated_rmsnorm(o64, jnp.concatenate(z_t, axis=0), normw)

    for kh in range(NP):
        vn = v_new_t[kh].astype(BF16)
        kdp = kd[kh * R:(kh + 1) * R]
        for j in range(2):
            h = 2 * kh + j
            keep = first if j == 0 else jnp.logical_not(first)
            kdm = jnp.where(keep, kdp, 0.0).astype(BF16)
            eg = jnp.exp(glast[:, h:h + 1])
            snew_ref[0, h] = s_ref[0, h] * eg + _dot_tn(kdm, vn)
            tile = on[kh * R:(kh + 1) * R]
            if j == 1:
                tile = pltpu.roll(tile, T, axis=0)
            o_ref[0, :, h * GDN_HEAD:(h + 1) * GDN_HEAD] = tile[:T].astype(o_ref.dtype)


def _gdn_sample(conv_state, mixed, z, ba, conv_w, a_log, dt_bias, norm_w, state):
    nb, t, _ = mixed.shape
    return pl.pallas_call(
        _gdn_sample_kernel,
        grid=(nb,),
        in_specs=[pl.BlockSpec((1, GDN_CONV_WIDTH - 1, GDN_CONV_DIM), lambda b: (b, 0, 0)),
                  pl.BlockSpec((1, t, GDN_CONV_DIM), lambda b: (b, 0, 0)),
                  pl.BlockSpec((1, t, GDN_VALUE_DIM), lambda b: (b, 0, 0)),
                  pl.BlockSpec((1, t, 2 * GDN_V_HEADS), lambda b: (b, 0, 0)),
                  pl.BlockSpec((GDN_CONV_WIDTH, GDN_CONV_DIM), lambda b: (0, 0)),
                  pl.BlockSpec((1, GDN_V_HEADS), lambda b: (0, 0)),
                  pl.BlockSpec((1, GDN_V_HEADS), lambda b: (0, 0)),
                  pl.BlockSpec((1, GDN_HEAD), lambda b: (0, 0)),
                  pl.BlockSpec((1, GDN_V_HEADS, GDN_HEAD, GDN_HEAD), lambda b: (b, 0, 0, 0))],
        out_specs=[pl.BlockSpec((1, t, GDN_VALUE_DIM), lambda b: (b, 0, 0)),
                   pl.BlockSpec((1, GDN_V_HEADS, GDN_HEAD, GDN_HEAD), lambda b: (b, 0, 0, 0))],
        out_shape=[jax.ShapeDtypeStruct((nb, t, GDN_VALUE_DIM), BF16),
                   jax.ShapeDtypeStruct(state.shape, F32)],
        scratch_shapes=[pltpu.VMEM((GDN_CONV_WIDTH - 1 + t, GDN_CONV_DIM), F32),
                        pltpu.VMEM((2 * t, GDN_CONV_DIM), F32),
                        pltpu.VMEM((2 * t, GDN_VALUE_DIM), F32),
                        pltpu.VMEM((2 * t, 2 * GDN_V_HEADS), F32)],
        compiler_params=_cparams(("parallel",)),
        name="gdn_sample",
    )(conv_state, mixed, z, ba, conv_w, a_log, dt_bias, norm_w, state)


def _rope_tables(pos, half, period):
    inv_freq = ROPE_THETA ** (-jnp.arange(half, dtype=F32) * 2.0 / (2 * half))
    ang = pos.astype(F32)[:, None] * inv_freq[None, :]
    cos, sin = jnp.cos(ang), jnp.sin(ang)
    ones = jnp.ones((pos.shape[0], period - 2 * half), F32)
    cos_p = jnp.concatenate([cos, cos, ones], axis=1)
    sin_p = jnp.concatenate([-sin, sin, 0.0 * ones], axis=1)
    reps = LANES // period
    return jnp.tile(cos_p, (1, reps)), jnp.tile(sin_p, (1, reps))


def _rope_tile(x, cos, sin, half, period):
    lane = lax.broadcasted_iota(jnp.int32, x.shape, 1) % period
    partner = jnp.where(lane < half, pltpu.roll(x, LANES - half, axis=1), pltpu.roll(x, half, axis=1))
    return x * cos + partner * sin


def _dsa_proj_kernel(x_ref, w_ref, ca_ref, sa_ref, ci_ref, si_ref, g_ref, b_ref,
                     q_ref, k_ref, v_ref, vt_ref, iq_ref, ikw_ref, ik2_ref):
    x = x_ref[...].astype(BF16)
    ca, sa, ci, si = ca_ref[...], sa_ref[...], ci_ref[...], si_ref[...]
    q_off, k_off = 0, ATT_HEADS * ATT_HEAD_DIM
    v_off = k_off + ATT_KV_HEADS * ATT_HEAD_DIM
    iq_off = v_off + ATT_KV_HEADS * ATT_HEAD_DIM
    ik_off = iq_off + IDX_HEADS * IDX_HEAD_DIM

    def proj(off):
        return _dot(x, w_ref[:, off:off + LANES])

    for h in range(ATT_HEADS):
        t = _rope_tile(proj(q_off + h * LANES), ca, sa, ATT_ROT_HALF, LANES)
        q_ref[:, h * LANES:(h + 1) * LANES] = (t * (ATT_HEAD_DIM ** -0.5)).astype(q_ref.dtype)
    for h in range(ATT_KV_HEADS):
        k_ref[:, h * LANES:(h + 1) * LANES] = _rope_tile(proj(k_off + h * LANES), ca, sa, ATT_ROT_HALF, LANES)
        v = proj(v_off + h * LANES)
        v_ref[:, h * LANES:(h + 1) * LANES] = v
        vt_ref[h * LANES:(h + 1) * LANES, :] = v.T.astype(vt_ref.dtype)
    for h in range(IDX_HEADS * IDX_HEAD_DIM // LANES):
        t = _rope_tile(proj(iq_off + h * LANES), ci, si, IDX_ROT_HALF, IDX_HEAD_DIM)
        iq_ref[:, h * LANES:(h + 1) * LANES] = t.astype(iq_ref.dtype)
    t = proj(ik_off)
    lane = lax.broadcasted_iota(jnp.int32, t.shape, 1)
    is_ik = lane < IDX_HEAD_DIM
    mu = jnp.sum(jnp.where(is_ik, t, 0.0), axis=-1, keepdims=True) / IDX_HEAD_DIM
    tc = jnp.where(is_ik, t - mu, 0.0)
    var = jnp.sum(tc * tc, axis=-1, keepdims=True) / IDX_HEAD_DIM
    ik = _rope_tile(tc * lax.rsqrt(var + LN_EPS) * g_ref[...] + b_ref[...], ci, si, IDX_ROT_HALF, IDX_HEAD_DIM)
    ikw = jnp.where(is_ik, ik, t * (IDX_HEADS ** -0.5))
    ikw_ref[...] = ikw
    ik_only = jnp.where(is_ik, ik, 0.0)
    ik2_ref[...] = (ik_only + pltpu.roll(ik_only, IDX_HEAD_DIM, axis=1)).astype(ik2_ref.dtype)


def _dsa_proj(x, w, tables, ik_g, ik_b):
    rows, d = x.shape
    n = w.shape[1]
    kv = ATT_KV_HEADS * ATT_HEAD_DIM
    row_spec = lambda width: pl.BlockSpec((ROW_TILE, width), lambda i: (i, 0))
    const_spec = lambda shape: pl.BlockSpec(shape, lambda i: (0, 0))
    return pl.pallas_call(
        _dsa_proj_kernel,
        grid=(rows // ROW_TILE,),
        in_specs=[row_spec(d), const_spec((d, n))] + [row_spec(LANES)] * 4 + [const_spec((1, LANES))] * 2,
        out_specs=[row_spec(ATT_HEADS * ATT_HEAD_DIM), row_spec(kv), row_spec(kv),
                   pl.BlockSpec((kv, ROW_TILE), lambda i: (0, i)),
                   row_spec(IDX_HEADS * IDX_HEAD_DIM), row_spec(LANES), row_spec(LANES)],
        out_shape=[jax.ShapeDtypeStruct((rows, ATT_HEADS * ATT_HEAD_DIM), BF16),
                   jax.ShapeDtypeStruct((rows, kv), F32),
                   jax.ShapeDtypeStruct((rows, kv), F32),
                   jax.ShapeDtypeStruct((kv, rows), BF16),
                   jax.ShapeDtypeStruct((rows, IDX_HEADS * IDX_HEAD_DIM), BF16),
                   jax.ShapeDtypeStruct((rows, LANES), F32),
                   jax.ShapeDtypeStruct((rows, LANES), BF16)],
        compiler_params=_cparams(("parallel",)),
        name="dsa_proj",
    )(x, w, *tables, ik_g, ik_b)


def _float_order_key(x):
    b = pltpu.bitcast(x, jnp.int32)
    return b ^ ((b >> 31) & jnp.int32(0x7FFFFFFF))


def _float_from_key(k):
    return pltpu.bitcast(k ^ ((k >> 31) & jnp.int32(0x7FFFFFFF)), F32)


def _np_order_key(x):
    b = int(np.float32(x).view(np.int32))
    return b ^ ((b >> 31) & 0x7FFFFFFF)


_KEY_NEG_MAX = _np_order_key(-F32_MAX)
_KEY_POS_INF = _np_order_key(np.inf)


def _kth_largest_bounds(count_ge, target, shape):
    lo = jnp.full(shape, _KEY_NEG_MAX, jnp.int32)
    hi = jnp.full(shape, _KEY_POS_INF, jnp.int32)

    def body(_, carry):
        lo, hi = carry
        mid = (lo >> 1) + (hi >> 1) + (lo & hi & 1)
        ok = count_ge(_float_from_key(mid)) >= target
        return jnp.where(ok, mid, lo), jnp.where(ok, hi, mid)

    lo, hi = lax.fori_loop(0, 32, body, (lo, hi))
    return _float_from_key(lo), _float_from_key(hi)


def _dsa_prompt_kernel(topk, q_ref, iq_ref, ikw_ref, k_ref, vt_ref, ik2_ref, o_ref,
                       s_ref, xh_ref, qg_ref, sc_ref, acc_ref):
    i = pl.program_id(1)
    KT = KEY_TILE
    CH = KEY_CHUNK
    nc = i // (CH // KT) + 1
    row = lax.broadcasted_iota(jnp.int32, (CH, KT), 0)
    lane = lax.broadcasted_iota(jnp.int32, (KT, KT), 1)
    qpos = i * KT + lax.broadcasted_iota(jnp.int32, (1, KT), 1)
    GQ = ATT_GROUP * KT

    def chunk(c):
        return pl.ds(pl.multiple_of(c * CH, CH), CH)

    NACC = 8

    def fold_rows(x):
        return x.reshape(CH // (NACC * SUBLANES), NACC, SUBLANES, x.shape[-1])

    def unfold(x, op):
        return op(op(x, axis=0), axis=0, keepdims=True)

    for h in range(IDX_HEADS):
        tile = iq_ref[:, (h // 2) * LANES:(h // 2 + 1) * LANES]
        mine = (lane // IDX_HEAD_DIM) == (h % 2)
        xh_ref[h * KT:(h + 1) * KT, :] = jnp.where(mine, tile, jnp.zeros_like(tile))
    for g in range(ATT_KV_HEADS):
        for hq in range(ATT_GROUP):
            h = g * ATT_GROUP + hq
            qg_ref[g, hq * KT:(hq + 1) * KT, :] = q_ref[:, h * LANES:(h + 1) * LANES]
    w_rows = ikw_ref[...].T[IDX_HEAD_DIM:IDX_HEAD_DIM + IDX_HEADS, :]

    def score_chunk(c, _):
        d = _dot_nt(ik2_ref[chunk(c), :], xh_ref[...])
        acc = w_rows[0:1, :] * jnp.maximum(d[:, 0:KT], 0.0)
        for h in range(1, IDX_HEADS):
            acc = acc + w_rows[h:h + 1, :] * jnp.maximum(d[:, h * KT:(h + 1) * KT], 0.0)
        acc = acc * (IDX_HEAD_DIM ** -0.5)
        kpos = c * CH + row
        acc = jnp.where(kpos < N_META, jnp.inf, acc)
        s_ref[chunk(c), :] = jnp.where(kpos <= qpos, acc, NEG_INF)
        return 0

    lax.fori_loop(0, nc, score_chunk, 0)

    target = jnp.minimum(qpos + 1, topk).astype(F32)

    def count_ge(thr):
        def body(c, acc):
            hit = jnp.where(s_ref[chunk(c), :] >= thr, 1.0, 0.0)
            return acc + jnp.sum(fold_rows(hit), axis=0)
        return unfold(lax.fori_loop(0, nc, body, jnp.zeros((NACC, SUBLANES, KT), F32)), jnp.sum)

    lo, hi = _kth_largest_bounds(count_ge, target, (1, KT))
    surplus = count_ge(lo) - target

    @pl.when(jnp.max(surplus) > 0.0)
    def _():
        need = target - count_ge(hi)
        rr = lax.broadcasted_iota(jnp.int32, (CH, CH), 0)
        cc = lax.broadcasted_iota(jnp.int32, (CH, CH), 1)
        tri = (cc <= rr).astype(BF16)

        def drop(c, carry):
            t = s_ref[chunk(c), :]
            tie = (t >= lo) & jnp.logical_not(t >= hi)
            tie_f = jnp.where(tie, 1.0, 0.0)
            rank = carry + _dot(tri, tie_f.astype(BF16))
            s_ref[chunk(c), :] = jnp.where(tie & (rank > need), NEG_INF, t)
            return carry + jnp.sum(tie_f, axis=0, keepdims=True)

        lax.fori_loop(0, nc, drop, jnp.zeros((1, KT), F32))

    def bias_cols(c):
        b = jnp.where(s_ref[chunk(c), :] >= lo, 0.0, NEG_INF)
        return jnp.concatenate([b] * ATT_GROUP, axis=1)

    G = ATT_KV_HEADS

    def max_pass(c, macc):
        bias = bias_cols(c)
        tops = []
        for g in range(G):
            sc = _dot_nt(k_ref[chunk(c), g * LANES:(g + 1) * LANES], qg_ref[g]) + bias
            sc_ref[chunk(c), g * GQ:(g + 1) * GQ] = sc
            tops.append(jnp.max(fold_rows(sc), axis=0))
        return jnp.maximum(macc, jnp.concatenate(tops, axis=-1))

    macc = lax.fori_loop(0, nc, max_pass, jnp.full((NACC, SUBLANES, G * GQ), NEG_INF, F32))
    m = unfold(macc, jnp.max)

    ones_rows = jnp.ones((2 * SUBLANES, CH), BF16)
    acc_ref[...] = jnp.zeros_like(acc_ref)

    def sum_pass(c, _):
        for g in range(G):
            p = jnp.exp(sc_ref[chunk(c), g * GQ:(g + 1) * GQ] - m[:, g * GQ:(g + 1) * GQ]).astype(BF16)
            vt = jnp.concatenate([vt_ref[g * LANES:(g + 1) * LANES, chunk(c)], ones_rows], axis=0)
            acc_ref[g] += _dot(vt, p)
        return 0

    lax.fori_loop(0, nc, sum_pass, 0)
    for g in range(G):
        acc = acc_ref[g]
        out_t = acc[:ATT_HEAD_DIM] / acc[ATT_HEAD_DIM:ATT_HEAD_DIM + 1]
        for hq in range(ATT_GROUP):
            h = g * ATT_GROUP + hq
            o_ref[:, h * LANES:(h + 1) * LANES] = out_t[:, hq * KT:(hq + 1) * KT].T.astype(o_ref.dtype)


def _dsa_prompt(q, iq, ikw, k, vt, ik2, n_batch, lp, topk):
    KT = KEY_TILE
    nq = lp // KT
    kv = ATT_KV_HEADS * ATT_HEAD_DIM
    qspec = lambda width: pl.BlockSpec((KT, width), lambda b, i: (b * nq + i, 0))
    return pl.pallas_call(
        functools.partial(_dsa_prompt_kernel, topk),
        grid=(n_batch, nq),
        in_specs=[qspec(ATT_HEADS * ATT_HEAD_DIM), qspec(IDX_HEADS * IDX_HEAD_DIM), qspec(LANES),
                  pl.BlockSpec((lp, kv), lambda b, i: (b, 0)),
                  pl.BlockSpec((kv, lp), lambda b, i: (0, b)),
                  pl.BlockSpec((lp, LANES), lambda b, i: (b, 0))],
        out_specs=qspec(ATT_HEADS * ATT_HEAD_DIM),
        out_shape=jax.ShapeDtypeStruct((n_batch * lp, ATT_HEADS * ATT_HEAD_DIM), BF16),
        scratch_shapes=[pltpu.VMEM((lp, KT), F32),
                        pltpu.VMEM((IDX_HEADS * KT, LANES), BF16),
                        pltpu.VMEM((ATT_KV_HEADS, ATT_GROUP * KT, ATT_HEAD_DIM), BF16),
                        pltpu.VMEM((lp, ATT_HEADS * KT), F32),
                        pltpu.VMEM((ATT_KV_HEADS, ATT_HEAD_DIM + 2 * SUBLANES, ATT_GROUP * KT), F32)],
        compiler_params=_cparams(("parallel", "arbitrary")),
        name="dsa_prompt",
    )(q, iq, ikw, k, vt, ik2)


SAMPLE_GROUP = 16


def _dsa_sample_select_kernel(topk, n_pages, pt_ref, iq_ref, ikw_ref, *rest):
    del pt_ref
    page_refs = rest[:n_pages]
    sel_ref, lo_ref, stack_ref, st_ref, iq8_ref, ikw8_ref, newk_ref = rest[n_pages:]
    b = pl.program_id(0)
    j = b % SAMPLE_GROUP
    T = iq_ref.shape[1]
    R = 2 * T
    past = n_pages * PAGE
    n_keys = stack_ref.shape[1]
    GR = SAMPLE_GROUP * R

    iq8_ref[0:T, :] = iq_ref[0].astype(F32)
    iq8_ref[T:R, :] = iq_ref[0].astype(F32)
    ikw8_ref[0:T, :] = ikw_ref[0]
    ikw8_ref[T:R, :] = ikw_ref[0]
    lane = lax.broadcasted_iota(jnp.int32, (R, LANES), 1)
    ikw8 = ikw8_ref[...]
    xh = []
    for h in range(IDX_HEADS):
        tile = iq8_ref[:, (h // 2) * LANES:(h // 2 + 1) * LANES]
        if h % 2:
            tile = pltpu.roll(tile, IDX_HEAD_DIM, axis=1)
        xh.append(jnp.where(lane < IDX_HEAD_DIM, tile, 0.0))
    x_all = jnp.concatenate(xh, axis=0).astype(BF16)

    def index_scores(keys):
        d = _dot_nt(x_all, keys)
        acc = jnp.zeros((R, LANES), F32)
        for h in range(IDX_HEADS):
            wcol = ikw8[:, IDX_HEAD_DIM + h:IDX_HEAD_DIM + h + 1]
            acc = acc + wcol * jnp.maximum(d[h * R:(h + 1) * R], 0.0)
        return acc * (IDX_HEAD_DIM ** -0.5)

    rows = pl.ds(pl.multiple_of(j * R, R), R)
    zeros64 = jnp.zeros((PAGE, LANES - IDX_HEAD_DIM), BF16)
    for p in range(n_pages):
        keys = jnp.concatenate([page_refs[p][0].astype(BF16), zeros64], axis=1)
        sc = index_scores(keys)
        if p == 0:
            sc = jnp.where(lane < N_META, jnp.inf, sc)
        stack_ref[rows, p * PAGE:(p + 1) * PAGE] = sc
    tmod = lax.broadcasted_iota(jnp.int32, (R, 1), 0) % T
    newk_ref[...] = jnp.zeros_like(newk_ref)
    newk_ref[0:T, :] = ikw_ref[0]
    nk = newk_ref[...]
    lane_k = lax.broadcasted_iota(jnp.int32, nk.shape, 1)
    sc_new = index_scores(jnp.where(lane_k < IDX_HEAD_DIM, nk, 0.0).astype(BF16))
    stack_ref[rows, past:n_keys] = jnp.where(lane <= tmod, sc_new, NEG_INF)

    @pl.when(j == SAMPLE_GROUP - 1)
    def _():
        st_ref[...] = stack_ref[...].T
        qlane = lax.broadcasted_iota(jnp.int32, (1, GR), 1)
        qpos = past + (qlane % R) % T
        target = jnp.minimum(qpos + 1, topk).astype(F32)
        NACC = 8

        def count_ge(thr):
            hit = jnp.where(st_ref[...] >= thr, 1.0, 0.0)
            part = jnp.sum(hit.reshape(n_keys // (NACC * SUBLANES), NACC, SUBLANES, GR), axis=0)
            return jnp.sum(jnp.sum(part, axis=0), axis=0, keepdims=True)

        lo, hi = _kth_largest_bounds(count_ge, target, (1, GR))
        surplus = count_ge(lo) - target

        @pl.when(jnp.max(surplus) > 0.0)
        def _():
            need = target - count_ge(hi)
            rr = lax.broadcasted_iota(jnp.int32, (LANES, LANES), 0)
            cc = lax.broadcasted_iota(jnp.int32, (LANES, LANES), 1)
            tri = (cc <= rr).astype(BF16)
            carry = jnp.zeros((1, GR), F32)
            for kt in range(n_keys // LANES):
                t = st_ref[kt * LANES:(kt + 1) * LANES, :]
                tie = (t >= lo) & jnp.logical_not(t >= hi)
                tie_f = jnp.where(tie, 1.0, 0.0)
                rank = carry + _dot(tri, tie_f.astype(BF16))
                st_ref[kt * LANES:(kt + 1) * LANES, :] = jnp.where(tie & (rank > need), NEG_INF, t)
                carry = carry + jnp.sum(tie_f, axis=0, keepdims=True)

        sel_ref[...] = st_ref[...].T
        lo_ref[...] = jnp.broadcast_to(lo, (GR, GR)).T


def _dsa_sample_select(page_table, iq, ikw, cache_ik, topk):
    nb, t, _ = iq.shape
    n_pages = page_table.shape[1]
    n_keys = n_pages * PAGE + LANES
    gr = SAMPLE_GROUP * 2 * t
    assert nb % SAMPLE_GROUP == 0 and gr == LANES
    tok = lambda width: pl.BlockSpec((1, t, width), lambda b, pt: (b, 0, 0))
    page = lambda p: pl.BlockSpec((1, PAGE, IDX_HEAD_DIM), lambda b, pt: (pt[b, p], 0, 0))
    grid_spec = pltpu.PrefetchScalarGridSpec(
        num_scalar_prefetch=1,
        grid=(nb,),
        in_specs=[tok(IDX_HEADS * IDX_HEAD_DIM), tok(LANES)] + [page(p) for p in range(n_pages)],
        out_specs=[pl.BlockSpec((gr, n_keys), lambda b, pt: (b // SAMPLE_GROUP, 0)),
                   pl.BlockSpec((gr, LANES), lambda b, pt: (b // SAMPLE_GROUP, 0))],
        scratch_shapes=[pltpu.VMEM((gr, n_keys), F32),
                        pltpu.VMEM((n_keys, gr), F32),
                        pltpu.VMEM((2 * t, IDX_HEADS * IDX_HEAD_DIM), F32),
                        pltpu.VMEM((2 * t, LANES), F32),
                        pltpu.VMEM((LANES, LANES), F32)])
    return pl.pallas_call(
        functools.partial(_dsa_sample_select_kernel, topk, n_pages),
        grid_spec=grid_spec,
        out_shape=[jax.ShapeDtypeStruct((nb * 2 * t, n_keys), F32),
                   jax.ShapeDtypeStruct((nb * 2 * t, LANES), F32)],
        compiler_params=_cparams(("arbitrary",)),
        name="dsa_sample_select",
    )(page_table, iq, ikw, *([cache_ik] * n_pages))


def _dsa_sample_attend_kernel(n_pages, pt_ref, q_ref, kn_ref, vn_ref, sel_ref, lo_ref, *rest):
    del pt_ref
    k_refs = rest[:n_pages]
    v_refs = rest[n_pages:2 * n_pages]
    o_ref, q8_ref, newk_ref, newv_ref = rest[2 * n_pages:]
    T = q_ref.shape[1]
    R = 2 * T
    past = n_pages * PAGE
    q8_ref[0:T, :] = q_ref[0].astype(F32)
    q8_ref[T:R, :] = q_ref[0].astype(F32)
    newk_ref[...] = jnp.zeros_like(newk_ref)
    newv_ref[...] = jnp.zeros_like(newv_ref)
    newk_ref[0:T, :] = kn_ref[0]
    newv_ref[0:T, :] = vn_ref[0]
    bias8 = jnp.where(sel_ref[...] >= lo_ref[:, 0:1], 0.0, NEG_INF)
    bias = jnp.concatenate([bias8] * ATT_GROUP, axis=0)
    for g in range(ATT_KV_HEADS):
        qg = jnp.concatenate([q8_ref[:, (g * ATT_GROUP + hq) * LANES:(g * ATT_GROUP + hq + 1) * LANES]
                              for hq in range(ATT_GROUP)], axis=0).astype(BF16)
        head_rows = pl.ds(g, PAGE, stride=ATT_KV_HEADS)
        sc = [_dot_nt(qg, k_refs[p][0, head_rows, :].astype(BF16)) for p in range(n_pages)]
        sc.append(_dot_nt(qg, newk_ref[:, g * LANES:(g + 1) * LANES].astype(BF16)))
        sc = jnp.concatenate(sc, axis=1) + bias
        m = jnp.max(sc, axis=1, keepdims=True)
        pr = jnp.exp(sc - m)
        prb = pr.astype(BF16)
        acc = _dot(prb[:, past:], newv_ref[:, g * LANES:(g + 1) * LANES].astype(BF16))
        for p in range(n_pages):
            acc = acc + _dot(prb[:, p * PAGE:(p + 1) * PAGE], v_refs[p][0, head_rows, :].astype(BF16))
        out = acc / jnp.sum(pr, axis=1, keepdims=True)
        for hq in range(ATT_GROUP):
            h = g * ATT_GROUP + hq
            o_ref[0, :, h * LANES:(h + 1) * LANES] = out[hq * R:hq * R + T].astype(o_ref.dtype)


def _dsa_sample_attend(page_table, q, k_new, v_new, sel, lo, cache_k, cache_v):
    nb, t, _ = q.shape
    n_pages = page_table.shape[1]
    kv = ATT_KV_HEADS * ATT_HEAD_DIM
    n_keys = sel.shape[1]
    tok = lambda width: pl.BlockSpec((1, t, width), lambda b, pt: (b, 0, 0))
    page = lambda p: pl.BlockSpec((1, ATT_KV_HEADS * PAGE, ATT_HEAD_DIM), lambda b, pt: (pt[b, p], 0, 0))
    grid_spec = pltpu.PrefetchScalarGridSpec(
        num_scalar_prefetch=1,
        grid=(nb,),
        in_specs=[tok(ATT_HEADS * ATT_HEAD_DIM), tok(kv), tok(kv),
                  pl.BlockSpec((2 * t, n_keys), lambda b, pt: (b, 0)),
                  pl.BlockSpec((2 * t, LANES), lambda b, pt: (b, 0))]
                 + [page(p) for p in range(n_pages)] * 2,
        out_specs=tok(ATT_HEADS * ATT_HEAD_DIM),
        scratch_shapes=[pltpu.VMEM((2 * t, ATT_HEADS * ATT_HEAD_DIM), F32),
                        pltpu.VMEM((LANES, kv), F32),
                        pltpu.VMEM((LANES, kv), F32)])
    return pl.pallas_call(
        functools.partial(_dsa_sample_attend_kernel, n_pages),
        grid_spec=grid_spec,
        out_shape=jax.ShapeDtypeStruct((nb, t, ATT_HEADS * ATT_HEAD_DIM), BF16),
        compiler_params=_cparams(("parallel",)),
        name="dsa_sample_attend",
    )(page_table, q, k_new, v_new, sel, lo, *([cache_k] * n_pages), *([cache_v] * n_pages))


def kernel(x_prompt, x_sample, state_gdn, state_gdn_conv, cache_k, cache_v, cache_idx_k, page_table,
           meta_tokens, ln1_g, ln1_b, ln2_g, ln2_b, mlp_w1, mlp_w2,
           gdn_w_in, gdn_conv_w, gdn_a_log, gdn_dt_bias, gdn_norm_w, gdn_w_out,
           dsa_w_in, dsa_ik_norm_g, dsa_ik_norm_b, dsa_w_o):
    nb, seq, d = x_prompt.shape
    ns, ts, _ = x_sample.shape
    n_tok = N_META + seq
    lp = -(-n_tok // KEY_TILE) * KEY_TILE
    n_prompt_rows = nb * lp
    rows = n_prompt_rows + ns * ts
    assert rows % ROW_TILE == 0 and lp % GDN_CHUNK == 0 and lp % KEY_CHUNK == 0 and d == D_MODEL
    kvd = ATT_KV_HEADS * ATT_HEAD_DIM

    meta = jnp.broadcast_to(meta_tokens[None].astype(x_prompt.dtype), (nb, N_META, d))
    pad = jnp.zeros((nb, lp - n_tok, d), x_prompt.dtype)
    h = jnp.concatenate([jnp.concatenate([meta, x_prompt, pad], 1).reshape(n_prompt_rows, d),
                         x_sample.reshape(ns * ts, d)], 0)

    def row2(x):
        return x.reshape(1, -1)

    def prompt_rows(x, width):
        return x[:n_prompt_rows].reshape(nb, lp, -1)[:, :n_tok, :width]

    w_in = gdn_w_in[0]
    split = GDN_CONV_DIM + GDN_VALUE_DIM
    proj = _matmul(h, w_in[:, :split].astype(BF16), 1024)
    n_gate = 2 * GDN_V_HEADS
    ba = _matmul(h, jnp.pad(w_in[:, split:], ((0, 0), (0, LANES - n_gate))).astype(BF16), LANES)
    decay_lanes = lambda v: jnp.pad(row2(v), ((0, 0), (GDN_V_HEADS, LANES - n_gate)))
    a_p, gdn_state_prompt = _gdn_prompt(proj, ba, gdn_conv_w[0], decay_lanes(gdn_a_log[0]),
                                        decay_lanes(gdn_dt_bias[0]), row2(gdn_norm_w[0]), nb, lp, n_tok)
    proj_s = proj[n_prompt_rows:].reshape(ns, ts, split)
    mixed_s = proj_s[:, :, :GDN_CONV_DIM]
    a_s, gdn_state_sample = _gdn_sample(state_gdn_conv[0], mixed_s, proj_s[:, :, GDN_CONV_DIM:],
                                        ba[n_prompt_rows:, :n_gate].reshape(ns, ts, n_gate), gdn_conv_w[0],
                                        row2(gdn_a_log[0]), row2(gdn_dt_bias[0]), row2(gdn_norm_w[0]),
                                        state_gdn[0])
    a = jnp.concatenate([a_p, a_s.reshape(ns * ts, -1)], 0)
    h = _outproj_ln(a, h, gdn_w_out[0].astype(BF16), row2(ln1_g[0]), row2(ln1_b[0]))
    h = _mlp_ln(h, mlp_w1[0].astype(BF16), mlp_w2[0].astype(BF16), row2(ln2_g[0]), row2(ln2_b[0]))
    keep = GDN_CONV_WIDTH - 1
    gdn_conv_prompt = prompt_rows(proj, GDN_CONV_DIM)[:, n_tok - keep:]
    gdn_conv_sample = jnp.concatenate([state_gdn_conv[0], mixed_s], 1)[:, -keep:]

    past = page_table.shape[1] * PAGE
    pos = jnp.concatenate([jnp.tile(jnp.arange(lp, dtype=jnp.int32), nb),
                           jnp.tile(past + jnp.arange(ts, dtype=jnp.int32), ns)])
    tables = _rope_tables(pos, ATT_ROT_HALF, LANES) + _rope_tables(pos, IDX_ROT_HALF, IDX_HEAD_DIM)
    n_in = dsa_w_in.shape[2]
    w_dsa = jnp.pad(dsa_w_in[0], ((0, 0), (0, -n_in % LANES))).astype(BF16)
    pad_lanes = lambda v: row2(jnp.pad(v, (0, LANES - v.shape[0])))
    q, k, v, vt, iq, ikw, ik2 = _dsa_proj(h, w_dsa, tables, pad_lanes(dsa_ik_norm_g[0]),
                                          pad_lanes(dsa_ik_norm_b[0]))
    o_p = _dsa_prompt(q, iq, ikw, k.astype(BF16), vt, ik2, nb, lp, min(TOPK_MAX, (n_tok - N_META) // 4))
    smp = lambda x: x[n_prompt_rows:].reshape(ns, ts, -1)
    n_pool = cache_k.shape[1]
    sel, lo = _dsa_sample_select(page_table, smp(iq), smp(ikw), cache_idx_k[0], min(TOPK_MAX, (past + ts) // 4))
    rows_kh = lambda c: c[0].reshape(n_pool, PAGE * ATT_KV_HEADS, ATT_HEAD_DIM)
    o_s = _dsa_sample_attend(page_table, smp(q), smp(k), smp(v), sel, lo, rows_kh(cache_k), rows_kh(cache_v))
    a = jnp.concatenate([o_p, o_s.reshape(ns * ts, -1)], 0)
    h = _outproj_ln(a, h, dsa_w_o[0].astype(BF16), row2(ln1_g[1]), row2(ln1_b[1]))
    h = _mlp_ln(h, mlp_w1[1].astype(BF16), mlp_w2[1].astype(BF16), row2(ln2_g[1]), row2(ln2_b[1]))

    heads = lambda x: x.reshape(x.shape[:-1] + (ATT_KV_HEADS, ATT_HEAD_DIM))
    y_prompt = h[:n_prompt_rows].reshape(nb, lp, d)[:, N_META:n_tok]
    y_sample = h[n_prompt_rows:].reshape(ns, ts, d)
    return (y_prompt, y_sample,
            gdn_state_prompt[None], gdn_conv_prompt[None], gdn_state_sample[None], gdn_conv_sample[None],
            heads(prompt_rows(k, kvd))[None], heads(prompt_rows(v, kvd))[None],
            prompt_rows(ikw, IDX_HEAD_DIM)[None],
            heads(smp(k))[None], heads(smp(v))[None], smp(ikw)[:, :, :IDX_HEAD_DIM][None])
```

```python
import functools

import jax
import jax.numpy as jnp
from jax import lax
from jax.experimental import pallas as pl
from jax.experimental.pallas import tpu as pltpu

F32 = jnp.float32
BF16 = jnp.bfloat16
HIGHEST = lax.Precision.HIGHEST

D_MODEL = 1024
N_META = 16
DEPTH = 2
LN_EPS = 1e-5
DEEPNORM_ALPHA = (2 * DEPTH) ** 0.25
GDN_K_HEADS = 8
GDN_V_HEADS = 16
GDN_HEAD = 128
GDN_KEY_DIM = GDN_K_HEADS * GDN_HEAD
GDN_VALUE_DIM = GDN_V_HEADS * GDN_HEAD
GDN_CONV_DIM = 2 * GDN_KEY_DIM + GDN_VALUE_DIM
GDN_CONV_WIDTH = 4
GDN_CHUNK = 128
L2_EPS = 1e-6
RMS_EPS = 1e-6
ATT_HEADS = 8
ATT_KV_HEADS = 2
ATT_HEAD_DIM = 128
ATT_GROUP = ATT_HEADS // ATT_KV_HEADS
IDX_HEADS = 8
IDX_HEAD_DIM = 64
TOPK_MAX = 256
ROPE_THETA = 500000.0
ATT_ROT_HALF = ATT_HEAD_DIM // 8
IDX_ROT_HALF = IDX_HEAD_DIM // 8
PAGE = 128

LANES = 128
SUBLANES = 8
ROW_TILE = 512
KEY_TILE = 128
KEY_CHUNK = 3 * KEY_TILE
VMEM_LIMIT = 56 * 1024 * 1024

NEG_INF = float("-inf")


def _cparams(sem):
    return pltpu.CompilerParams(dimension_semantics=sem, vmem_limit_bytes=VMEM_LIMIT)


def _dot(a, b):
    return jnp.dot(a, b, preferred_element_type=F32)


def _dot_nt(a, b, precision=None):
    return lax.dot_general(a, b, (((1,), (1,)), ((), ())), preferred_element_type=F32,
                           precision=precision)


def _dot_tn(a, b):
    return lax.dot_general(a, b, (((0,), (0,)), ((), ())), preferred_element_type=F32)


def _layernorm_rows(x, g, b):
    mu = jnp.mean(x, axis=-1, keepdims=True)
    xc = x - mu
    var = jnp.mean(xc * xc, axis=-1, keepdims=True)
    return xc * lax.rsqrt(var + LN_EPS) * g + b


def _sigmoid(x):
    return 1.0 / (1.0 + jnp.exp(-x))


def _silu(x):
    return x * _sigmoid(x)


def _softplus(x):
    return jnp.maximum(x, 0.0) + jnp.log(1.0 + jnp.exp(-jnp.abs(x)))


def _resident(shape):
    return pl.BlockSpec(shape, lambda *_: (0,) * len(shape), pipeline_mode=pl.Buffered(1))


def _gdn_inproj_kernel(x_ref, w_ref, proj_ref, ba_ref):
    x = x_ref[...].astype(BF16)
    n_main = proj_ref.shape[1]
    slab = 1024
    for j in range(n_main // slab):
        proj_ref[:, j * slab:(j + 1) * slab] = _dot(x, w_ref[:, j * slab:(j + 1) * slab])
    ba_ref[...] = _dot(x, w_ref[:, n_main:])


def _gdn_inproj(x, w, n_main):
    rows, k = x.shape
    n = w.shape[1]
    return pl.pallas_call(
        _gdn_inproj_kernel,
        grid=(rows // ROW_TILE,),
        in_specs=[pl.BlockSpec((ROW_TILE, k), lambda i: (i, 0)), _resident((k, n))],
        out_specs=[pl.BlockSpec((ROW_TILE, n_main), lambda i: (i, 0)),
                   pl.BlockSpec((ROW_TILE, n - n_main), lambda i: (i, 0))],
        out_shape=[jax.ShapeDtypeStruct((rows, n_main), F32),
                   jax.ShapeDtypeStruct((rows, n - n_main), F32)],
        compiler_params=_cparams(("parallel",)),
        name="gdn_inproj",
    )(x, w)


def _outproj_ln_kernel(a_ref, x_ref, w_ref, g_ref, b_ref, o_ref):
    y = DEEPNORM_ALPHA * x_ref[...] + _dot(a_ref[...], w_ref[...])
    o_ref[...] = _layernorm_rows(y, g_ref[...], b_ref[...])


def _outproj_ln(a, x, w, g, b):
    rows, k = a.shape
    d = x.shape[1]
    return pl.pallas_call(
        _outproj_ln_kernel,
        grid=(rows // ROW_TILE,),
        in_specs=[pl.BlockSpec((ROW_TILE, k), lambda i: (i, 0)),
                  pl.BlockSpec((ROW_TILE, d), lambda i: (i, 0)),
                  _resident((k, d)), _resident((1, d)), _resident((1, d))],
        out_specs=pl.BlockSpec((ROW_TILE, d), lambda i: (i, 0)),
        out_shape=jax.ShapeDtypeStruct((rows, d), F32),
        compiler_params=_cparams(("parallel",)),
        name="outproj_ln",
    )(a, x, w, g, b)


def _mlp_ln_kernel(x_ref, w1_ref, w2_ref, g_ref, b_ref, o_ref):
    x = x_ref[...]
    h = jnp.maximum(_dot(x.astype(BF16), w1_ref[...]), 0.0)
    y = DEEPNORM_ALPHA * x + _dot((h * h).astype(BF16), w2_ref[...])
    o_ref[...] = _layernorm_rows(y, g_ref[...], b_ref[...])


def _mlp_ln(x, w1, w2, g, b):
    rows, d = x.shape
    f = w1.shape[1]
    return pl.pallas_call(
        _mlp_ln_kernel,
        grid=(rows // ROW_TILE,),
        in_specs=[pl.BlockSpec((ROW_TILE, d), lambda i: (i, 0)),
                  _resident((d, f)), _resident((f, d)), _resident((1, d)), _resident((1, d))],
        out_specs=pl.BlockSpec((ROW_TILE, d), lambda i: (i, 0)),
        out_shape=jax.ShapeDtypeStruct((rows, d), F32),
        compiler_params=_cparams(("parallel",)),
        name="mlp_ln",
    )(x, w1, w2, g, b)


def _unit_lower_inverse(a, n_factors):
    c = a.shape[0]
    row = lax.broadcasted_iota(jnp.int32, (c, c), 0)
    col = lax.broadcasted_iota(jnp.int32, (c, c), 1)
    eye = (row == col).astype(F32)
    p = -a
    t = eye + p
    for _ in range(n_factors - 1):
        pb = p.astype(BF16)
        p = _dot(pb, pb)
        t = t + _dot(t.astype(BF16), p.astype(BF16))
    return t


def _bdot(a, b):
    return lax.dot_general(a, b, (((2,), (1,)), ((0,), (0,))), preferred_element_type=F32)


def _bdot_nt(a, b):
    return lax.dot_general(a, b, (((2,), (2,)), ((0,), (0,))), preferred_element_type=F32)


def _bdot_tn(a, b):
    return lax.dot_general(a, b, (((1,), (1,)), ((0,), (0,))), preferred_element_type=F32)


def _unit_lower_inverse_batched(a):
    c = a.shape[-1]
    row = lax.broadcasted_iota(jnp.int32, (c, c), 0)
    col = lax.broadcasted_iota(jnp.int32, (c, c), 1)
    def coupling(s):
        couples = ((row // (2 * s)) == (col // (2 * s))) & ((row // s) != (col // s))
        return jnp.where(couples, a, 0.0)

    t = (row == col).astype(F32) - coupling(1)
    s = 2
    while s < c:
        tb = t.astype(BF16)
        t = t - _bdot(_bdot(tb, coupling(s).astype(BF16)).astype(BF16), tb)
        s *= 2
    return t


def _pairwise_diff(col):
    c = col.shape[0]
    lane = lax.broadcasted_iota(jnp.int32, (c, LANES), 1)
    left = jnp.where(lane == 0, col, jnp.where(lane == 1, 1.0, 0.0))
    right = jnp.where(lane == 0, 1.0, jnp.where(lane == 1, -col, 0.0))
    return _dot_nt(left, right, precision=HIGHEST)


def _l2norm_rows(x):
    return x * lax.rsqrt(jnp.sum(x * x, axis=-1, keepdims=True) + L2_EPS)


def _gated_rmsnorm(o, z, norm_w):
    on = o * lax.rsqrt(jnp.mean(o * o, axis=-1, keepdims=True) + RMS_EPS) * norm_w
    return on * _silu(z)


def _gdn_gates(ba, a_log, dt_bias):
    beta = _sigmoid(ba[:, :GDN_V_HEADS])
    g = -jnp.exp(a_log) * _softplus(ba[:, GDN_V_HEADS:] + dt_bias)
    return beta, g


def _gdn_prompt_kernel(n_tokens, mixed_ref, z_ref, ba_ref, convw_ref, alog_ref, dtb_ref, normw_ref,
                       o_ref, sfin_ref, xc_ref, s_ref):
    c = pl.program_id(1)
    C = GDN_CHUNK
    W = GDN_CONV_WIDTH

    @pl.when(c == 0)
    def _():
        xc_ref[...] = jnp.zeros_like(xc_ref)
        s_ref[...] = jnp.zeros_like(s_ref)

    x = mixed_ref[...]
    tail = xc_ref[...]
    w = convw_ref[...]
    row8 = lax.broadcasted_iota(jnp.int32, (SUBLANES, 1), 0)
    acc = x * w[W - 1:W, :]
    for s in range(1, W):
        shifted = pltpu.roll(x, s, axis=0)
        head = jnp.where(row8 < s, pltpu.roll(tail, s, axis=0), shifted[:SUBLANES])
        acc = acc + jnp.concatenate([head, shifted[SUBLANES:]], axis=0) * w[W - 1 - s:W - s, :]
    qkv = _silu(acc)
    xc_ref[...] = x[C - SUBLANES:]

    row = lax.broadcasted_iota(jnp.int32, (C, 1), 0)
    valid = (c * C + row) < n_tokens
    ba = ba_ref[...]
    beta = jnp.where(valid, _sigmoid(ba), 0.0)
    g = jnp.where(valid, -jnp.exp(alog_ref[...]) * _softplus(ba + dtb_ref[...]), 0.0)

    ri = lax.broadcasted_iota(jnp.int32, (C, C), 0)
    ci = lax.broadcasted_iota(jnp.int32, (C, C), 1)
    tril = ci <= ri
    strict = ci < ri
    gc = jnp.dot(tril.astype(F32), g, preferred_element_type=F32, precision=HIGHEST)
    gc_rows = gc.T
    egc = jnp.exp(gc)
    glast = gc[C - 1:C, :]
    ekd = jnp.exp(glast - gc)
    eglast = jnp.exp(glast)
    normw = normw_ref[...]
    NH = GDN_V_HEADS

    def gcol(x, h):
        return x[:, NH + h:NH + h + 1]

    qk_raw = [qkv[:, i * GDN_HEAD:(i + 1) * GDN_HEAD] for i in range(2 * GDN_K_HEADS)]
    sq = jnp.concatenate([t * t for t in qk_raw], axis=0)
    sq_hi = sq.astype(BF16)
    sq_lo = (sq - sq_hi.astype(F32)).astype(BF16)
    ones = jnp.ones((GDN_HEAD, GDN_HEAD), BF16)
    inv_norm = lax.rsqrt(_dot(sq_hi, ones) + _dot(sq_lo, ones) + L2_EPS)
    qk_n = [t * inv_norm[i * C:(i + 1) * C] for i, t in enumerate(qk_raw)]
    q_l = [t * (GDN_HEAD ** -0.5) for t in qk_n[:GDN_K_HEADS]]
    k_l = qk_n[GDN_K_HEADS:]
    k8 = jnp.stack(k_l).astype(BF16)
    kk8 = _bdot_nt(k8, k8)
    qk8 = _bdot_nt(jnp.stack(q_l).astype(BF16), k8)
    a_l, qkd_l, rhs_l, qg_l, kd_l = [], [], [], [], []
    for h in range(NH):
        kh = h // 2
        v = qkv[:, 2 * GDN_KEY_DIM + h * GDN_HEAD:2 * GDN_KEY_DIM + (h + 1) * GDN_HEAD]
        bcol = beta[:, h:h + 1]
        diff = gcol(gc, h) - gc_rows[NH + h:NH + h + 1, :]
        decay = jnp.where(tril, jnp.exp(jnp.where(tril, diff, 0.0)), 0.0)
        a_l.append(jnp.where(strict, kk8[kh] * decay * bcol, 0.0))
        qkd_l.append((qk8[kh] * decay).astype(BF16))
        rhs_l.append(jnp.concatenate([v * bcol, k_l[kh] * (bcol * gcol(egc, h))], axis=1).astype(BF16))
        qg_l.append(q_l[kh] * gcol(egc, h))
        kd_l.append((k_l[kh] * gcol(ekd, h)).astype(BF16))
    t_inv = _unit_lower_inverse_batched(jnp.stack(a_l)).astype(BF16)
    sol = _bdot(t_inv, jnp.stack(rhs_l))
    u, wm = sol[:, :, :GDN_HEAD], sol[:, :, GDN_HEAD:]
    s_old = s_ref[...]
    lhs = jnp.concatenate([wm, jnp.stack(qg_l)], axis=1).astype(BF16)
    ws_qs = _bdot(lhs, s_old.astype(BF16))
    v_new = u - ws_qs[:, :C]
    vnb = v_new.astype(BF16)
    o = ws_qs[:, C:] + _bdot(jnp.stack(qkd_l), vnb)
    upd = _bdot_tn(jnp.stack(kd_l), vnb)
    for h in range(NH):
        s_ref[h] = s_old[h] * gcol(eglast, h) + upd[h]
        zh = z_ref[:, h * GDN_HEAD:(h + 1) * GDN_HEAD]
        o_ref[:, h * GDN_HEAD:(h + 1) * GDN_HEAD] = _gated_rmsnorm(o[h], zh, normw).astype(o_ref.dtype)

    @pl.when(c == pl.num_programs(1) - 1)
    def _():
        sfin_ref[0] = s_ref[...]


def _gdn_prompt(proj, ba, conv_w, a_log, dt_bias, norm_w, n_batch, lp, n_tokens):
    C = GDN_CHUNK
    per_b = lp // C
    n_chunks = per_b
    z_off = GDN_CONV_DIM // GDN_VALUE_DIM
    kern = functools.partial(_gdn_prompt_kernel, n_tokens)
    return pl.pallas_call(
        kern,
        grid=(n_batch, n_chunks),
        in_specs=[pl.BlockSpec((C, GDN_CONV_DIM), lambda b, c: (b * per_b + c, 0)),
                  pl.BlockSpec((C, GDN_VALUE_DIM), lambda b, c: (b * per_b + c, z_off)),
                  pl.BlockSpec((C, LANES), lambda b, c: (b * per_b + c, 0)),
                  pl.BlockSpec((GDN_CONV_WIDTH, GDN_CONV_DIM), lambda b, c: (0, 0)),
                  pl.BlockSpec((1, LANES), lambda b, c: (0, 0)),
                  pl.BlockSpec((1, LANES), lambda b, c: (0, 0)),
                  pl.BlockSpec((1, GDN_HEAD), lambda b, c: (0, 0))],
        out_specs=[pl.BlockSpec((C, GDN_VALUE_DIM), lambda b, c: (b * per_b + c, 0)),
                   pl.BlockSpec((1, GDN_V_HEADS, GDN_HEAD, GDN_HEAD), lambda b, c: (b, 0, 0, 0))],
        out_shape=[jax.ShapeDtypeStruct((proj.shape[0], GDN_VALUE_DIM), BF16),
                   jax.ShapeDtypeStruct((n_batch, GDN_V_HEADS, GDN_HEAD, GDN_HEAD), F32)],
        scratch_shapes=[pltpu.VMEM((SUBLANES, GDN_CONV_DIM), F32),
                        pltpu.VMEM((GDN_V_HEADS, GDN_HEAD, GDN_HEAD), F32)],
        compiler_params=_cparams(("parallel", "arbitrary")),
        name="gdn_prompt",
    )(proj, proj, ba, conv_w, a_log, dt_bias, norm_w)


def _twice(dst_ref, x):
    t = x.shape[0]
    dst_ref[0:t, :] = x
    dst_ref[t:2 * t, :] = x
    return dst_ref[...]


def _gdn_sample_kernel(cs_ref, mixed_ref, z_ref, ba_ref, convw_ref, alog_ref, dtb_ref, normw_ref, s_ref,
                       o_ref, snew_ref, xc_ref, dq_ref, dz_ref, dba_ref):
    T = mixed_ref.shape[1]
    R = 2 * T
    W = GDN_CONV_WIDTH
    xc_ref[0:W - 1, :] = cs_ref[0]
    xc_ref[W - 1:W - 1 + T, :] = mixed_ref[0]
    w = convw_ref[...]
    acc = xc_ref[0:T, :] * w[0:1, :]
    for j in range(1, W):
        acc = acc + xc_ref[j:j + T, :] * w[j:j + 1, :]
    qkv = _twice(dq_ref, _silu(acc))
    z2 = _twice(dz_ref, z_ref[0])
    ba2 = _twice(dba_ref, ba_ref[0])
    beta, g = _gdn_gates(ba2, alog_ref[...], dtb_ref[...])

    rr = lax.broadcasted_iota(jnp.int32, (R, 1), 0)
    tmod = rr % T
    first = rr < T
    gc = jnp.zeros_like(g)
    for s in range(T):
        gc = gc + jnp.where(tmod >= s, g[s:s + 1, :], 0.0)
    glast = gc[T - 1:T, :]
    normw = normw_ref[...]

    NP = GDN_K_HEADS
    M = NP * R

    def pair_col(x, kh):
        return jnp.where(first, x[:, 2 * kh:2 * kh + 1], x[:, 2 * kh + 1:2 * kh + 2])

    def pair_tile(x, base, kh):
        a = x[:, base + (2 * kh) * GDN_HEAD:base + (2 * kh + 1) * GDN_HEAD]
        b = x[:, base + (2 * kh + 1) * GDN_HEAD:base + (2 * kh + 2) * GDN_HEAD]
        return jnp.where(first, a, b)

    q_t, k_t, v_t, z_t, b_c, gc_c, gl_c = [], [], [], [], [], [], []
    for kh in range(NP):
        q_t.append(_l2norm_rows(qkv[:, kh * GDN_HEAD:(kh + 1) * GDN_HEAD]) * (GDN_HEAD ** -0.5))
        k_t.append(_l2norm_rows(qkv[:, GDN_KEY_DIM + kh * GDN_HEAD:GDN_KEY_DIM + (kh + 1) * GDN_HEAD]))
        v_t.append(pair_tile(qkv, 2 * GDN_KEY_DIM, kh))
        z_t.append(pair_tile(z2, 0, kh))
        b_c.append(pair_col(beta, kh))
        gc_c.append(pair_col(gc, kh))
        gl_c.append(pair_col(jnp.broadcast_to(glast, (R, GDN_V_HEADS)), kh))
    q64 = jnp.concatenate(q_t, axis=0)
    k64 = jnp.concatenate(k_t, axis=0)
    v64 = jnp.concatenate(v_t, axis=0)
    bcol = jnp.concatenate(b_c, axis=0)
    gcol = jnp.concatenate(gc_c, axis=0)
    glcol = jnp.concatenate(gl_c, axis=0)

    ri = lax.broadcasted_iota(jnp.int32, (M, M), 0)
    ci = lax.broadcasted_iota(jnp.int32, (M, M), 1)
    same = (ri // T) == (ci // T)
    tril = same & ((ci % T) <= (ri % T))
    strict = same & ((ci % T) < (ri % T))
    kb = k64.astype(BF16)
    kk = _dot_nt(kb, kb)
    qk0 = _dot_nt(q64.astype(BF16), kb)
    diff = _pairwise_diff(gcol)
    decay = jnp.where(tril, jnp.exp(jnp.where(tril, diff, 0.0)), 0.0)
    a_mat = jnp.where(strict, kk * decay * bcol, 0.0)
    t_inv = _unit_lower_inverse(a_mat, max(1, (T - 1).bit_length())).astype(BF16)
    egc = jnp.exp(gcol)
    u = _dot(t_inv, (v64 * bcol).astype(BF16))
    wm = _dot(t_inv, (k64 * (bcol * egc)).astype(BF16))
    qg = q64 * egc
    kd = k64 * jnp.exp(glcol - gcol)
    qkd = (qk0 * decay).astype(BF16)

    v_new_t, qs_t = [], []
    for kh in range(NP):
        lhs = jnp.concatenate([wm[kh * R:(kh + 1) * R], qg[kh * R:(kh + 1) * R]], axis=0).astype(BF16)
        r0 = _dot(lhs, s_ref[0, 2 * kh].astype(BF16))
        r1 = _dot(lhs, s_ref[0, 2 * kh + 1].astype(BF16))
        ws = jnp.where(first, r0[:R], r1[:R])
        qs_t.append(jnp.where(first, r0[R:], r1[R:]))
        v_new_t.append(u[kh * R:(kh + 1) * R] - ws)
    v_new = jnp.concatenate(v_new_t, axis=0)
    o64 = jnp.concatenate(qs_t, axis=0) + _dot(qkd, v_new.astype(BF16))
    on = _gated_rmsnorm(o64, jnp.concatenate(z_t, axis=0), normw)

    for kh in range(NP):
        vn = v_new_t[kh].astype(BF16)
        kdp = kd[kh * R:(kh + 1) * R]
        for j in range(2):
            h = 2 * kh + j
            keep = first if j == 0 else jnp.logical_not(first)
            kdm = jnp.where(keep, kdp, 0.0).astype(BF16)
            eg = jnp.exp(glast[:, h:h + 1])
            snew_ref[0, h] = s_ref[0, h] * eg + _dot_tn(kdm, vn)
            tile = on[kh * R:(kh + 1) * R]
            if j == 1:
                tile = pltpu.roll(tile, T, axis=0)
            o_ref[0, :, h * GDN_HEAD:(h + 1) * GDN_HEAD] = tile[:T].astype(o_ref.dtype)


def _gdn_sample(conv_state, mixed, z, ba, conv_w, a_log, dt_bias, norm_w, state):
    nb, t, _ = mixed.shape
    return pl.pallas_call(
        _gdn_sample_kernel,
        grid=(nb,),
        in_specs=[pl.BlockSpec((1, GDN_CONV_WIDTH - 1, GDN_CONV_DIM), lambda b: (b, 0, 0)),
                  pl.BlockSpec((1, t, GDN_CONV_DIM), lambda b: (b, 0, 0)),
                  pl.BlockSpec((1, t, GDN_VALUE_DIM), lambda b: (b, 0, 0)),
                  pl.BlockSpec((1, t, 2 * GDN_V_HEADS), lambda b: (b, 0, 0)),
                  pl.BlockSpec((GDN_CONV_WIDTH, GDN_CONV_DIM), lambda b: (0, 0)),
                  pl.BlockSpec((1, GDN_V_HEADS), lambda b: (0, 0)),
                  pl.BlockSpec((1, GDN_V_HEADS), lambda b: (0, 0)),
                  pl.BlockSpec((1, GDN_HEAD), lambda b: (0, 0)),
                  pl.BlockSpec((1, GDN_V_HEADS, GDN_HEAD, GDN_HEAD), lambda b: (b, 0, 0, 0))],
        out_specs=[pl.BlockSpec((1, t, GDN_VALUE_DIM), lambda b: (b, 0, 0)),
                   pl.BlockSpec((1, GDN_V_HEADS, GDN_HEAD, GDN_HEAD), lambda b: (b, 0, 0, 0))],
        out_shape=[jax.ShapeDtypeStruct((nb, t, GDN_VALUE_DIM), BF16),
                   jax.ShapeDtypeStruct(state.shape, F32)],
        scratch_shapes=[pltpu.VMEM((GDN_CONV_WIDTH - 1 + t, GDN_CONV_DIM), F32),
                        pltpu.VMEM((2 * t, GDN_CONV_DIM), F32),
                        pltpu.VMEM((2 * t, GDN_VALUE_DIM), F32),
                        pltpu.VMEM((2 * t, 2 * GDN_V_HEADS), F32)],
        compiler_params=_cparams(("parallel",)),
        name="gdn_sample",
    )(conv_state, mixed, z, ba, conv_w, a_log, dt_bias, norm_w, state)


def _rope_tables(pos, half, period):
    inv_freq = ROPE_THETA ** (-jnp.arange(half, dtype=F32) * 2.0 / (2 * half))
    ang = pos.astype(F32)[:, None] * inv_freq[None, :]
    cos, sin = jnp.cos(ang), jnp.sin(ang)
    ones = jnp.ones((pos.shape[0], period - 2 * half), F32)
    cos_p = jnp.concatenate([cos, cos, ones], axis=1)
    sin_p = jnp.concatenate([-sin, sin, 0.0 * ones], axis=1)
    reps = LANES // period
    return jnp.tile(cos_p, (1, reps)), jnp.tile(sin_p, (1, reps))


def _rope_tile(x, cos, sin, half, period):
    lane = lax.broadcasted_iota(jnp.int32, x.shape, 1) % period
    partner = jnp.where(lane < half, pltpu.roll(x, LANES - half, axis=1), pltpu.roll(x, half, axis=1))
    return x * cos + partner * sin


def _dsa_proj_kernel(x_ref, w_ref, ca_ref, sa_ref, ci_ref, si_ref, g_ref, b_ref,
                     q_ref, k_ref, v_ref, vt_ref, iq_ref, ikw_ref, ik2_ref):
    x = x_ref[...].astype(BF16)
    ca, sa, ci, si = ca_ref[...], sa_ref[...], ci_ref[...], si_ref[...]
    q_off, k_off = 0, ATT_HEADS * ATT_HEAD_DIM
    v_off = k_off + ATT_KV_HEADS * ATT_HEAD_DIM
    iq_off = v_off + ATT_KV_HEADS * ATT_HEAD_DIM
    ik_off = iq_off + IDX_HEADS * IDX_HEAD_DIM

    def proj(off):
        return _dot(x, w_ref[:, off:off + LANES])

    for h in range(ATT_HEADS):
        t = _rope_tile(proj(q_off + h * LANES), ca, sa, ATT_ROT_HALF, LANES)
        q_ref[:, h * LANES:(h + 1) * LANES] = (t * (ATT_HEAD_DIM ** -0.5)).astype(q_ref.dtype)
    for h in range(ATT_KV_HEADS):
        k_ref[:, h * LANES:(h + 1) * LANES] = _rope_tile(proj(k_off + h * LANES), ca, sa, ATT_ROT_HALF, LANES)
        v = proj(v_off + h * LANES)
        v_ref[:, h * LANES:(h + 1) * LANES] = v
        vt_ref[h * LANES:(h + 1) * LANES, :] = v.T.astype(vt_ref.dtype)
    for h in range(IDX_HEADS * IDX_HEAD_DIM // LANES):
        t = _rope_tile(proj(iq_off + h * LANES), ci, si, IDX_ROT_HALF, IDX_HEAD_DIM)
        iq_ref[:, h * LANES:(h + 1) * LANES] = t.astype(iq_ref.dtype)
    t = proj(ik_off)
    lane = lax.broadcasted_iota(jnp.int32, t.shape, 1)
    is_ik = lane < IDX_HEAD_DIM
    mu = jnp.sum(jnp.where(is_ik, t, 0.0), axis=-1, keepdims=True) / IDX_HEAD_DIM
    tc = jnp.where(is_ik, t - mu, 0.0)
    var = jnp.sum(tc * tc, axis=-1, keepdims=True) / IDX_HEAD_DIM
    ik = _rope_tile(tc * lax.rsqrt(var + LN_EPS) * g_ref[...] + b_ref[...], ci, si, IDX_ROT_HALF, IDX_HEAD_DIM)
    ikw = jnp.where(is_ik, ik, t * (IDX_HEADS ** -0.5))
    ikw_ref[...] = ikw
    ik_only = jnp.where(is_ik, ik, 0.0)
    ik2_ref[...] = (ik_only + pltpu.roll(ik_only, IDX_HEAD_DIM, axis=1)).astype(ik2_ref.dtype)


def _dsa_proj(x, w, tables, ik_g, ik_b):
    rows, d = x.shape
    n = w.shape[1]
    kv = ATT_KV_HEADS * ATT_HEAD_DIM
    row_spec = lambda width: pl.BlockSpec((ROW_TILE, width), lambda i: (i, 0))
    const_spec = _resident
    return pl.pallas_call(
        _dsa_proj_kernel,
        grid=(rows // ROW_TILE,),
        in_specs=[row_spec(d), const_spec((d, n))] + [row_spec(LANES)] * 4 + [const_spec((1, LANES))] * 2,
        out_specs=[row_spec(ATT_HEADS * ATT_HEAD_DIM), row_spec(kv), row_spec(kv),
                   pl.BlockSpec((kv, ROW_TILE), lambda i: (0, i)),
                   row_spec(IDX_HEADS * IDX_HEAD_DIM), row_spec(LANES), row_spec(LANES)],
        out_shape=[jax.ShapeDtypeStruct((rows, ATT_HEADS * ATT_HEAD_DIM), BF16),
                   jax.ShapeDtypeStruct((rows, kv), F32),
                   jax.ShapeDtypeStruct((rows, kv), F32),
                   jax.ShapeDtypeStruct((kv, rows), BF16),
                   jax.ShapeDtypeStruct((rows, IDX_HEADS * IDX_HEAD_DIM), BF16),
                   jax.ShapeDtypeStruct((rows, LANES), F32),
                   jax.ShapeDtypeStruct((rows, LANES), BF16)],
        compiler_params=_cparams(("parallel",)),
        name="dsa_proj",
    )(x, w, *tables, ik_g, ik_b)


def _float_order_key(x):
    b = pltpu.bitcast(x, jnp.int32)
    return b ^ ((b >> 31) & jnp.int32(0x7FFFFFFF))


def _float_from_key(k):
    return pltpu.bitcast(k ^ ((k >> 31) & jnp.int32(0x7FFFFFFF)), F32)


BISECT_UNROLL = 8
VALUE_ROUNDS = 2
KEY_ROUNDS = 32 // BISECT_UNROLL
F32_MAX = 3.4028234663852886e38


def _kth_largest_bounds(count_ge, target, amax):
    bound = jnp.minimum(2.0 * amax, F32_MAX)
    lo = _float_order_key(-bound)
    hi = _float_order_key(bound) + 1
    c_lo = jnp.full(target.shape, -1.0, F32)

    def step(carry, split_values):
        lo, hi, c_lo = carry
        mid = (lo >> 1) + (hi >> 1) + (lo & hi & 1)
        if split_values:
            vmid = _float_order_key(_float_from_key(lo) * 0.5 + _float_from_key(hi) * 0.5)
            mid = jnp.where((vmid > lo) & (vmid < hi), vmid, mid)
        cnt = count_ge(_float_from_key(mid))
        ok = cnt >= target
        return jnp.where(ok, mid, lo), jnp.where(ok, hi, mid), jnp.where(ok, cnt, c_lo)

    def rounds(carry, n_rounds, split_values):
        def cond(state):
            it, (lo, hi, c_lo) = state
            pending = (c_lo != target) & (hi != lo + 1)
            return (it < n_rounds) & (jnp.max(jnp.where(pending, 1.0, 0.0)) > 0.0)

        def body(state):
            it, carry = state
            for _ in range(BISECT_UNROLL):
                carry = step(carry, split_values)
            return it + 1, carry

        return lax.while_loop(cond, body, (jnp.int32(0), carry))[1]

    carry = rounds((lo, hi, c_lo), VALUE_ROUNDS, True)
    lo, hi, _ = rounds(carry, KEY_ROUNDS, False)
    return _float_from_key(lo), _float_from_key(hi)


def _dsa_prompt_kernel(topk, q_ref, iq_ref, ikw_ref, k_ref, vt_ref, ik2_ref, o_ref,
                       s_ref, xh_ref, qg_ref, sc_ref, acc_ref):
    i = pl.program_id(1)
    KT = KEY_TILE
    CH = KEY_CHUNK
    nc = i // (CH // KT) + 1
    row = lax.broadcasted_iota(jnp.int32, (CH, KT), 0)
    lane = lax.broadcasted_iota(jnp.int32, (KT, KT), 1)
    qpos = i * KT + lax.broadcasted_iota(jnp.int32, (1, KT), 1)
    GQ = ATT_GROUP * KT

    def chunk(c):
        return pl.ds(pl.multiple_of(c * CH, CH), CH)

    NACC = 8

    def fold_rows(x):
        return x.reshape(CH // (NACC * SUBLANES), NACC, SUBLANES, x.shape[-1])

    def unfold(x, op):
        return op(op(x, axis=0), axis=0, keepdims=True)

    for h in range(IDX_HEADS):
        tile = iq_ref[:, (h // 2) * LANES:(h // 2 + 1) * LANES]
        mine = (lane // IDX_HEAD_DIM) == (h % 2)
        xh_ref[h * KT:(h + 1) * KT, :] = jnp.where(mine, tile, jnp.zeros_like(tile))
    for g in range(ATT_KV_HEADS):
        for hq in range(ATT_GROUP):
            h = g * ATT_GROUP + hq
            qg_ref[g, hq * KT:(hq + 1) * KT, :] = q_ref[:, h * LANES:(h + 1) * LANES]
    w_rows = ikw_ref[...].T[IDX_HEAD_DIM:IDX_HEAD_DIM + IDX_HEADS, :]

    def score_chunk(c, amax):
        d = _dot_nt(ik2_ref[chunk(c), :], xh_ref[...])
        acc = w_rows[0:1, :] * jnp.maximum(d[:, 0:KT], 0.0)
        for h in range(1, IDX_HEADS):
            acc = acc + w_rows[h:h + 1, :] * jnp.maximum(d[:, h * KT:(h + 1) * KT], 0.0)
        acc = acc * (IDX_HEAD_DIM ** -0.5)
        kpos = c * CH + row
        causal = kpos <= qpos
        meta = kpos < N_META
        s_ref[chunk(c), :] = jnp.where(causal, jnp.where(meta, jnp.inf, acc), NEG_INF)
        finite = jnp.where(causal & jnp.logical_not(meta), jnp.abs(acc), 0.0)
        return jnp.maximum(amax, jnp.max(fold_rows(finite), axis=0))

    amax = unfold(lax.fori_loop(0, nc, score_chunk, jnp.zeros((NACC, SUBLANES, KT), F32)), jnp.max)

    target = jnp.minimum(qpos + 1, topk).astype(F32)

    def count_ge(thr):
        def body(c, acc):
            hit = jnp.where(s_ref[chunk(c), :] >= thr, 1.0, 0.0)
            return acc + jnp.sum(fold_rows(hit), axis=0)
        return unfold(lax.fori_loop(0, nc, body, jnp.zeros((NACC, SUBLANES, KT), F32)), jnp.sum)

    lo, hi = _kth_largest_bounds(count_ge, target, amax)
    surplus = count_ge(lo) - target

    @pl.when(jnp.max(surplus) > 0.0)
    def _():
        need = target - count_ge(hi)
        rr = lax.broadcasted_iota(jnp.int32, (CH, CH), 0)
        cc = lax.broadcasted_iota(jnp.int32, (CH, CH), 1)
        tri = (cc <= rr).astype(BF16)

        def drop(c, carry):
            t = s_ref[chunk(c), :]
            tie = (t >= lo) & jnp.logical_not(t >= hi)
            tie_f = jnp.where(tie, 1.0, 0.0)
            rank = carry + _dot(tri, tie_f.astype(BF16))
            s_ref[chunk(c), :] = jnp.where(tie & (rank > need), NEG_INF, t)
            return carry + jnp.sum(tie_f, axis=0, keepdims=True)

        lax.fori_loop(0, nc, drop, jnp.zeros((1, KT), F32))

    def bias_cols(c):
        b = jnp.where(s_ref[chunk(c), :] >= lo, 0.0, NEG_INF)
        return jnp.concatenate([b] * ATT_GROUP, axis=1)

    G = ATT_KV_HEADS

    def max_pass(c, macc):
        bias = bias_cols(c)
        tops = []
        for g in range(G):
            sc = _dot_nt(k_ref[chunk(c), g * LANES:(g + 1) * LANES], qg_ref[g]) + bias
            sc_ref[chunk(c), g * GQ:(g + 1) * GQ] = sc
            tops.append(jnp.max(fold_rows(sc), axis=0))
        return jnp.maximum(macc, jnp.concatenate(tops, axis=-1))

    macc = lax.fori_loop(0, nc, max_pass, jnp.full((NACC, SUBLANES, G * GQ), NEG_INF, F32))
    m = unfold(macc, jnp.max)

    ones_rows = jnp.ones((2 * SUBLANES, CH), BF16)
    acc_ref[...] = jnp.zeros_like(acc_ref)

    def sum_pass(c, _):
        for g in range(G):
            p = jnp.exp(sc_ref[chunk(c), g * GQ:(g + 1) * GQ] - m[:, g * GQ:(g + 1) * GQ]).astype(BF16)
            vt = jnp.concatenate([vt_ref[g * LANES:(g + 1) * LANES, chunk(c)], ones_rows], axis=0)
            acc_ref[g] += _dot(vt, p)
        return 0

    lax.fori_loop(0, nc, sum_pass, 0)
    for g in range(G):
        acc = acc_ref[g]
        out_t = acc[:ATT_HEAD_DIM] / acc[ATT_HEAD_DIM:ATT_HEAD_DIM + 1]
        for hq in range(ATT_GROUP):
            h = g * ATT_GROUP + hq
            o_ref[:, h * LANES:(h + 1) * LANES] = out_t[:, hq * KT:(hq + 1) * KT].T.astype(o_ref.dtype)


def _dsa_prompt(q, iq, ikw, k, vt, ik2, n_batch, lp, topk):
    KT = KEY_TILE
    nq = lp // KT
    kv = ATT_KV_HEADS * ATT_HEAD_DIM
    qspec = lambda width: pl.BlockSpec((KT, width), lambda b, i: (b * nq + i, 0))
    return pl.pallas_call(
        functools.partial(_dsa_prompt_kernel, topk),
        grid=(n_batch, nq),
        in_specs=[qspec(ATT_HEADS * ATT_HEAD_DIM), qspec(IDX_HEADS * IDX_HEAD_DIM), qspec(LANES),
                  pl.BlockSpec((lp, kv), lambda b, i: (b, 0)),
                  pl.BlockSpec((kv, lp), lambda b, i: (0, b)),
                  pl.BlockSpec((lp, LANES), lambda b, i: (b, 0))],
        out_specs=qspec(ATT_HEADS * ATT_HEAD_DIM),
        out_shape=jax.ShapeDtypeStruct(q.shape, BF16),
        scratch_shapes=[pltpu.VMEM((lp, KT), F32),
                        pltpu.VMEM((IDX_HEADS * KT, LANES), BF16),
                        pltpu.VMEM((ATT_KV_HEADS, ATT_GROUP * KT, ATT_HEAD_DIM), BF16),
                        pltpu.VMEM((lp, ATT_HEADS * KT), F32),
                        pltpu.VMEM((ATT_KV_HEADS, ATT_HEAD_DIM + 2 * SUBLANES, ATT_GROUP * KT), F32)],
        compiler_params=_cparams(("parallel", "arbitrary")),
        name="dsa_prompt",
    )(q, iq, ikw, k, vt, ik2)


SAMPLE_GROUP = 16


def _dsa_sample_select_kernel(topk, n_pages, pt_ref, iq_ref, ikw_ref, *rest):
    del pt_ref
    page_refs = rest[:n_pages]
    sel_ref, lo_ref, stack_ref, st_ref, iq8_ref, ikw8_ref, newk_ref = rest[n_pages:]
    b = pl.program_id(0)
    j = b % SAMPLE_GROUP
    T = iq_ref.shape[1]
    R = 2 * T
    past = n_pages * PAGE
    n_keys = stack_ref.shape[1]
    GR = SAMPLE_GROUP * R

    iq8_ref[0:T, :] = iq_ref[0].astype(F32)
    iq8_ref[T:R, :] = iq_ref[0].astype(F32)
    ikw8_ref[0:T, :] = ikw_ref[0]
    ikw8_ref[T:R, :] = ikw_ref[0]
    lane = lax.broadcasted_iota(jnp.int32, (R, LANES), 1)
    ikw8 = ikw8_ref[...]
    xh = []
    for h in range(IDX_HEADS):
        tile = iq8_ref[:, (h // 2) * LANES:(h // 2 + 1) * LANES]
        if h % 2:
            tile = pltpu.roll(tile, IDX_HEAD_DIM, axis=1)
        xh.append(jnp.where(lane < IDX_HEAD_DIM, tile, 0.0))
    x_all = jnp.concatenate(xh, axis=0).astype(BF16)

    def index_scores(d):
        acc = jnp.zeros((R, LANES), F32)
        for h in range(IDX_HEADS):
            wcol = ikw8[:, IDX_HEAD_DIM + h:IDX_HEAD_DIM + h + 1]
            acc = acc + wcol * jnp.maximum(d[h * R:(h + 1) * R], 0.0)
        return acc * (IDX_HEAD_DIM ** -0.5)

    rows = pl.ds(pl.multiple_of(j * R, R), R)
    zeros64 = jnp.zeros((LANES - IDX_HEAD_DIM, PAGE), BF16)
    for p in range(n_pages):
        keys_t = jnp.concatenate([page_refs[p][0].astype(BF16), zeros64], axis=0)
        sc = index_scores(_dot(x_all, keys_t))
        if p == 0:
            sc = jnp.where(lane < N_META, jnp.inf, sc)
        stack_ref[rows, p * PAGE:(p + 1) * PAGE] = sc
    tmod = lax.broadcasted_iota(jnp.int32, (R, 1), 0) % T
    newk_ref[...] = jnp.zeros_like(newk_ref)
    newk_ref[0:T, :] = ikw_ref[0]
    nk = newk_ref[...]
    lane_k = lax.broadcasted_iota(jnp.int32, nk.shape, 1)
    sc_new = index_scores(_dot_nt(x_all, jnp.where(lane_k < IDX_HEAD_DIM, nk, 0.0).astype(BF16)))
    stack_ref[rows, past:n_keys] = jnp.where(lane <= tmod, sc_new, NEG_INF)

    @pl.when(j == SAMPLE_GROUP - 1)
    def _():
        st_ref[...] = stack_ref[...].T
        qlane = lax.broadcasted_iota(jnp.int32, (1, GR), 1)
        qpos = past + (qlane % R) % T
        target = jnp.minimum(qpos + 1, topk).astype(F32)
        NACC = 8

        def count_ge(thr):
            hit = jnp.where(st_ref[...] >= thr, 1.0, 0.0)
            part = jnp.sum(hit.reshape(n_keys // (NACC * SUBLANES), NACC, SUBLANES, GR), axis=0)
            return jnp.sum(jnp.sum(part, axis=0), axis=0, keepdims=True)

        mag = jnp.abs(st_ref[...])
        mag = jnp.where(mag < jnp.inf, mag, 0.0)
        amax = jnp.max(jnp.max(mag.reshape(n_keys // SUBLANES, SUBLANES, GR), axis=0), axis=0, keepdims=True)
        lo, hi = _kth_largest_bounds(count_ge, target, amax)
        surplus = count_ge(lo) - target

        @pl.when(jnp.max(surplus) > 0.0)
        def _():
            need = target - count_ge(hi)
            rr = lax.broadcasted_iota(jnp.int32, (LANES, LANES), 0)
            cc = lax.broadcasted_iota(jnp.int32, (LANES, LANES), 1)
            tri = (cc <= rr).astype(BF16)
            carry = jnp.zeros((1, GR), F32)
            for kt in range(n_keys // LANES):
                t = st_ref[kt * LANES:(kt + 1) * LANES, :]
                tie = (t >= lo) & jnp.logical_not(t >= hi)
                tie_f = jnp.where(tie, 1.0, 0.0)
                rank = carry + _dot(tri, tie_f.astype(BF16))
                st_ref[kt * LANES:(kt + 1) * LANES, :] = jnp.where(tie & (rank > need), NEG_INF, t)
                carry = carry + jnp.sum(tie_f, axis=0, keepdims=True)

        sel_ref[...] = st_ref[...].T
        lo_ref[...] = jnp.broadcast_to(lo, (GR, GR)).T


def _dsa_sample_select(page_table, iq, ikw, cache_ik, topk):
    nb, t, _ = iq.shape
    n_pages = page_table.shape[1]
    n_keys = n_pages * PAGE + LANES
    gr = SAMPLE_GROUP * 2 * t
    assert nb % SAMPLE_GROUP == 0 and gr == LANES
    tok = lambda width: pl.BlockSpec((1, t, width), lambda b, pt: (b, 0, 0))
    page = lambda p: pl.BlockSpec((1, IDX_HEAD_DIM, PAGE), lambda b, pt: (pt[b, p], 0, 0))
    grid_spec = pltpu.PrefetchScalarGridSpec(
        num_scalar_prefetch=1,
        grid=(nb,),
        in_specs=[tok(IDX_HEADS * IDX_HEAD_DIM), tok(LANES)] + [page(p) for p in range(n_pages)],
        out_specs=[pl.BlockSpec((gr, n_keys), lambda b, pt: (b // SAMPLE_GROUP, 0)),
                   pl.BlockSpec((gr, LANES), lambda b, pt: (b // SAMPLE_GROUP, 0))],
        scratch_shapes=[pltpu.VMEM((gr, n_keys), F32),
                        pltpu.VMEM((n_keys, gr), F32),
                        pltpu.VMEM((2 * t, IDX_HEADS * IDX_HEAD_DIM), F32),
                        pltpu.VMEM((2 * t, LANES), F32),
                        pltpu.VMEM((LANES, LANES), F32)])
    return pl.pallas_call(
        functools.partial(_dsa_sample_select_kernel, topk, n_pages),
        grid_spec=grid_spec,
        out_shape=[jax.ShapeDtypeStruct((nb * 2 * t, n_keys), F32),
                   jax.ShapeDtypeStruct((nb * 2 * t, LANES), F32)],
        compiler_params=_cparams(("arbitrary",)),
        name="dsa_sample_select",
    )(page_table, iq, ikw, *([cache_ik] * n_pages))


def _dsa_sample_attend_kernel(n_pages, pt_ref, q_ref, kn_ref, vn_ref, sel_ref, lo_ref, *rest):
    del pt_ref
    k_refs = rest[:n_pages]
    v_refs = rest[n_pages:2 * n_pages]
    o_ref, q8_ref, newk_ref, newv_ref = rest[2 * n_pages:]
    T = q_ref.shape[1]
    R = 2 * T
    past = n_pages * PAGE
    q8_ref[0:T, :] = q_ref[0].astype(F32)
    q8_ref[T:R, :] = q_ref[0].astype(F32)
    newk_ref[...] = jnp.zeros_like(newk_ref)
    newv_ref[...] = jnp.zeros_like(newv_ref)
    newk_ref[0:T, :] = kn_ref[0]
    newv_ref[0:T, :] = vn_ref[0]
    bias8 = jnp.where(sel_ref[...] >= lo_ref[:, 0:1], 0.0, NEG_INF)
    bias = jnp.concatenate([bias8] * ATT_GROUP, axis=0)
    for g in range(ATT_KV_HEADS):
        qg = jnp.concatenate([q8_ref[:, (g * ATT_GROUP + hq) * LANES:(g * ATT_GROUP + hq + 1) * LANES]
                              for hq in range(ATT_GROUP)], axis=0).astype(BF16)
        head_rows = pl.ds(g, PAGE, stride=ATT_KV_HEADS)
        sc = [_dot_nt(qg, k_refs[p][0, head_rows, :].astype(BF16)) for p in range(n_pages)]
        sc.append(_dot_nt(qg, newk_ref[:, g * LANES:(g + 1) * LANES].astype(BF16)))
        sc = jnp.concatenate(sc, axis=1) + bias
        m = jnp.max(sc, axis=1, keepdims=True)
        pr = jnp.exp(sc - m)
        prb = pr.astype(BF16)
        acc = _dot(prb[:, past:], newv_ref[:, g * LANES:(g + 1) * LANES].astype(BF16))
        for p in range(n_pages):
            acc = acc + _dot(prb[:, p * PAGE:(p + 1) * PAGE], v_refs[p][0, head_rows, :].astype(BF16))
        out = acc / jnp.sum(pr, axis=1, keepdims=True)
        for hq in range(ATT_GROUP):
            h = g * ATT_GROUP + hq
            o_ref[0, :, h * LANES:(h + 1) * LANES] = out[hq * R:hq * R + T].astype(o_ref.dtype)


def _dsa_sample_attend(page_table, q, k_new, v_new, sel, lo, cache_k, cache_v):
    nb, t, _ = q.shape
    n_pages = page_table.shape[1]
    kv = ATT_KV_HEADS * ATT_HEAD_DIM
    n_keys = sel.shape[1]
    tok = lambda width: pl.BlockSpec((1, t, width), lambda b, pt: (b, 0, 0))
    page = lambda p: pl.BlockSpec((1, ATT_KV_HEADS * PAGE, ATT_HEAD_DIM), lambda b, pt: (pt[b, p], 0, 0))
    grid_spec = pltpu.PrefetchScalarGridSpec(
        num_scalar_prefetch=1,
        grid=(nb,),
        in_specs=[tok(ATT_HEADS * ATT_HEAD_DIM), tok(kv), tok(kv),
                  pl.BlockSpec((2 * t, n_keys), lambda b, pt: (b, 0)),
                  pl.BlockSpec((2 * t, LANES), lambda b, pt: (b, 0))]
                 + [page(p) for p in range(n_pages)] * 2,
        out_specs=tok(ATT_HEADS * ATT_HEAD_DIM),
        scratch_shapes=[pltpu.VMEM((2 * t, ATT_HEADS * ATT_HEAD_DIM), F32),
                        pltpu.VMEM((LANES, kv), F32),
                        pltpu.VMEM((LANES, kv), F32)])
    return pl.pallas_call(
        functools.partial(_dsa_sample_attend_kernel, n_pages),
        grid_spec=grid_spec,
        out_shape=jax.ShapeDtypeStruct((nb, t, ATT_HEADS * ATT_HEAD_DIM), BF16),
        compiler_params=_cparams(("parallel",)),
        name="dsa_sample_attend",
    )(page_table, q, k_new, v_new, sel, lo, *([cache_k] * n_pages), *([cache_v] * n_pages))


def kernel(x_prompt, x_sample, state_gdn, state_gdn_conv, cache_k, cache_v, cache_idx_k, page_table,
           meta_tokens, ln1_g, ln1_b, ln2_g, ln2_b, mlp_w1, mlp_w2,
           gdn_w_in, gdn_conv_w, gdn_a_log, gdn_dt_bias, gdn_norm_w, gdn_w_out,
           dsa_w_in, dsa_ik_norm_g, dsa_ik_norm_b, dsa_w_o):
    nb, seq, d = x_prompt.shape
    ns, ts, _ = x_sample.shape
    n_tok = N_META + seq
    lp = -(-n_tok // KEY_TILE) * KEY_TILE
    n_prompt_rows = nb * lp
    rows = n_prompt_rows + ns * ts
    assert rows % ROW_TILE == 0 and lp % GDN_CHUNK == 0 and lp % KEY_CHUNK == 0 and d == D_MODEL
    kvd = ATT_KV_HEADS * ATT_HEAD_DIM

    meta = meta_tokens.astype(x_prompt.dtype)
    pad = jnp.zeros((lp - n_tok, d), x_prompt.dtype)
    h = jnp.concatenate([piece for b in range(nb) for piece in (meta, x_prompt[b], pad)]
                        + [x_sample.reshape(ns * ts, d)], 0)

    def row2(x):
        return x.reshape(1, -1)

    def prompt_rows(x, width, first=0):
        return jnp.stack([x[b * lp + first:b * lp + n_tok, :width] for b in range(nb)])

    def with_sample_rows(a_prompt, a_sample):
        return lax.dynamic_update_slice(a_prompt, a_sample.reshape(ns * ts, -1), (n_prompt_rows, 0))

    w_in = gdn_w_in[0]
    split = GDN_CONV_DIM + GDN_VALUE_DIM
    n_gate = 2 * GDN_V_HEADS
    w_cat = jnp.concatenate([w_in, jnp.zeros((d, LANES - n_gate), w_in.dtype)], 1).astype(BF16)
    proj, ba = _gdn_inproj(h, w_cat, split)
    decay_lanes = lambda v: jnp.pad(row2(v), ((0, 0), (GDN_V_HEADS, LANES - n_gate)))
    a_p, gdn_state_prompt = _gdn_prompt(proj, ba, gdn_conv_w[0], decay_lanes(gdn_a_log[0]),
                                        decay_lanes(gdn_dt_bias[0]), row2(gdn_norm_w[0]), nb, lp, n_tok)
    proj_s = proj[n_prompt_rows:].reshape(ns, ts, split)
    mixed_s = proj_s[:, :, :GDN_CONV_DIM]
    a_s, gdn_state_sample = _gdn_sample(state_gdn_conv[0], mixed_s, proj_s[:, :, GDN_CONV_DIM:],
                                        ba[n_prompt_rows:, :n_gate].reshape(ns, ts, n_gate), gdn_conv_w[0],
                                        row2(gdn_a_log[0]), row2(gdn_dt_bias[0]), row2(gdn_norm_w[0]),
                                        state_gdn[0])
    h = _outproj_ln(with_sample_rows(a_p, a_s), h, gdn_w_out[0].astype(BF16), row2(ln1_g[0]), row2(ln1_b[0]))
    h = _mlp_ln(h, mlp_w1[0].astype(BF16), mlp_w2[0].astype(BF16), row2(ln2_g[0]), row2(ln2_b[0]))
    keep = GDN_CONV_WIDTH - 1
    gdn_conv_prompt = prompt_rows(proj, GDN_CONV_DIM, first=n_tok - keep)
    gdn_conv_sample = jnp.concatenate([state_gdn_conv[0], mixed_s], 1)[:, -keep:]

    past = page_table.shape[1] * PAGE
    pos = jnp.concatenate([jnp.tile(jnp.arange(lp, dtype=jnp.int32), nb),
                           jnp.tile(past + jnp.arange(ts, dtype=jnp.int32), ns)])
    tables = _rope_tables(pos, ATT_ROT_HALF, LANES) + _rope_tables(pos, IDX_ROT_HALF, IDX_HEAD_DIM)
    n_in = dsa_w_in.shape[2]
    w_dsa = jnp.pad(dsa_w_in[0], ((0, 0), (0, -n_in % LANES))).astype(BF16)
    pad_lanes = lambda v: row2(jnp.pad(v, (0, LANES - v.shape[0])))
    q, k, v, vt, iq, ikw, ik2 = _dsa_proj(h, w_dsa, tables, pad_lanes(dsa_ik_norm_g[0]),
                                          pad_lanes(dsa_ik_norm_b[0]))
    o_p = _dsa_prompt(q, iq, ikw, k.astype(BF16), vt, ik2, nb, lp, min(TOPK_MAX, (n_tok - N_META) // 4))
    smp = lambda x: x[n_prompt_rows:].reshape(ns, ts, -1)
    n_pool = cache_k.shape[1]
    sel, lo = _dsa_sample_select(page_table, smp(iq), smp(ikw), jnp.swapaxes(cache_idx_k[0], 1, 2),
                                 min(TOPK_MAX, (past + ts) // 4))
    rows_kh = lambda c: c[0].reshape(n_pool, PAGE * ATT_KV_HEADS, ATT_HEAD_DIM)
    o_s = _dsa_sample_attend(page_table, smp(q), smp(k), smp(v), sel, lo, rows_kh(cache_k), rows_kh(cache_v))
    h = _outproj_ln(with_sample_rows(o_p, o_s), h, dsa_w_o[0].astype(BF16), row2(ln1_g[1]), row2(ln1_b[1]))
    h = _mlp_ln(h, mlp_w1[1].astype(BF16), mlp_w2[1].astype(BF16), row2(ln2_g[1]), row2(ln2_b[1]))

    heads = lambda x: x.reshape(x.shape[:-1] + (ATT_KV_HEADS, ATT_HEAD_DIM))
    y_prompt = prompt_rows(h, d, first=N_META)
    y_sample = h[n_prompt_rows:].reshape(ns, ts, d)
    return (y_prompt, y_sample,
            gdn_state_prompt[None], gdn_conv_prompt[None], gdn_state_sample[None], gdn_conv_sample[None],
            heads(prompt_rows(k, kvd))[None], heads(prompt_rows(v, kvd))[None],
            prompt_rows(ikw, IDX_HEAD_DIM)[None],
            heads(smp(k))[None], heads(smp(v))[None], smp(ikw)[:, :, :IDX_HEAD_DIM][None])
```

```python
import functools

import jax
import jax.numpy as jnp
from jax import lax
from jax.experimental import pallas as pl
from jax.experimental.pallas import tpu as pltpu

F32 = jnp.float32
BF16 = jnp.bfloat16
HIGHEST = lax.Precision.HIGHEST

D_MODEL = 1024
N_META = 16
DEPTH = 2
LN_EPS = 1e-5
DEEPNORM_ALPHA = (2 * DEPTH) ** 0.25
GDN_K_HEADS = 8
GDN_V_HEADS = 16
GDN_HEAD = 128
GDN_KEY_DIM = GDN_K_HEADS * GDN_HEAD
GDN_VALUE_DIM = GDN_V_HEADS * GDN_HEAD
GDN_CONV_DIM = 2 * GDN_KEY_DIM + GDN_VALUE_DIM
GDN_CONV_WIDTH = 4
GDN_CHUNK = 128
L2_EPS = 1e-6
RMS_EPS = 1e-6
ATT_HEADS = 8
ATT_KV_HEADS = 2
ATT_HEAD_DIM = 128
ATT_GROUP = ATT_HEADS // ATT_KV_HEADS
IDX_HEADS = 8
IDX_HEAD_DIM = 64
TOPK_MAX = 256
ROPE_THETA = 500000.0
ATT_ROT_HALF = ATT_HEAD_DIM // 8
IDX_ROT_HALF = IDX_HEAD_DIM // 8
PAGE = 128

LANES = 128
SUBLANES = 8
ROW_TILE = 512
KEY_TILE = 128
KEY_CHUNK = 3 * KEY_TILE
VMEM_LIMIT = 56 * 1024 * 1024

NEG_INF = float("-inf")


def _cparams(sem):
    return pltpu.CompilerParams(dimension_semantics=sem, vmem_limit_bytes=VMEM_LIMIT)


def _dot(a, b):
    return jnp.dot(a, b, preferred_element_type=F32)


def _dot_nt(a, b, precision=None):
    return lax.dot_general(a, b, (((1,), (1,)), ((), ())), preferred_element_type=F32,
                           precision=precision)


def _dot_tn(a, b):
    return lax.dot_general(a, b, (((0,), (0,)), ((), ())), preferred_element_type=F32)


def _layernorm_rows(x, g, b):
    mu = jnp.mean(x, axis=-1, keepdims=True)
    xc = x - mu
    var = jnp.mean(xc * xc, axis=-1, keepdims=True)
    return xc * lax.rsqrt(var + LN_EPS) * g + b


def _sigmoid(x):
    return 1.0 / (1.0 + jnp.exp(-x))


def _silu(x):
    return x * _sigmoid(x)


def _softplus(x):
    return jnp.maximum(x, 0.0) + jnp.log(1.0 + jnp.exp(-jnp.abs(x)))


def _resident(shape):
    return pl.BlockSpec(shape, lambda *_: (0,) * len(shape), pipeline_mode=pl.Buffered(1))


def _gdn_inproj_kernel(x_ref, w_ref, proj_ref, ba_ref):
    x = x_ref[...].astype(BF16)
    n_main = proj_ref.shape[1]
    slab = 1024
    for j in range(n_main // slab):
        proj_ref[:, j * slab:(j + 1) * slab] = _dot(x, w_ref[:, j * slab:(j + 1) * slab])
    ba_ref[...] = _dot(x, w_ref[:, n_main:])


def _gdn_inproj(x, w, n_main):
    rows, k = x.shape
    n = w.shape[1]
    return pl.pallas_call(
        _gdn_inproj_kernel,
        grid=(rows // ROW_TILE,),
        in_specs=[pl.BlockSpec((ROW_TILE, k), lambda i: (i, 0)), _resident((k, n))],
        out_specs=[pl.BlockSpec((ROW_TILE, n_main), lambda i: (i, 0)),
                   pl.BlockSpec((ROW_TILE, n - n_main), lambda i: (i, 0))],
        out_shape=[jax.ShapeDtypeStruct((rows, n_main), F32),
                   jax.ShapeDtypeStruct((rows, n - n_main), F32)],
        compiler_params=_cparams(("parallel",)),
        name="gdn_inproj",
    )(x, w)


def _outproj_ln_kernel(a_ref, x_ref, w_ref, g_ref, b_ref, o_ref):
    y = DEEPNORM_ALPHA * x_ref[...] + _dot(a_ref[...], w_ref[...])
    o_ref[...] = _layernorm_rows(y, g_ref[...], b_ref[...])


def _outproj_ln(a, x, w, g, b):
    rows, k = a.shape
    d = x.shape[1]
    return pl.pallas_call(
        _outproj_ln_kernel,
        grid=(rows // ROW_TILE,),
        in_specs=[pl.BlockSpec((ROW_TILE, k), lambda i: (i, 0)),
                  pl.BlockSpec((ROW_TILE, d), lambda i: (i, 0)),
                  _resident((k, d)), _resident((1, d)), _resident((1, d))],
        out_specs=pl.BlockSpec((ROW_TILE, d), lambda i: (i, 0)),
        out_shape=jax.ShapeDtypeStruct((rows, d), F32),
        compiler_params=_cparams(("parallel",)),
        name="outproj_ln",
    )(a, x, w, g, b)


def _mlp_ln_kernel(x_ref, w1_ref, w2_ref, g_ref, b_ref, o_ref):
    x = x_ref[...]
    h = jnp.maximum(_dot(x.astype(BF16), w1_ref[...]), 0.0)
    y = DEEPNORM_ALPHA * x + _dot((h * h).astype(BF16), w2_ref[...])
    o_ref[...] = _layernorm_rows(y, g_ref[...], b_ref[...])


def _mlp_ln(x, w1, w2, g, b):
    rows, d = x.shape
    f = w1.shape[1]
    return pl.pallas_call(
        _mlp_ln_kernel,
        grid=(rows // ROW_TILE,),
        in_specs=[pl.BlockSpec((ROW_TILE, d), lambda i: (i, 0)),
                  _resident((d, f)), _resident((f, d)), _resident((1, d)), _resident((1, d))],
        out_specs=pl.BlockSpec((ROW_TILE, d), lambda i: (i, 0)),
        out_shape=jax.ShapeDtypeStruct((rows, d), F32),
        compiler_params=_cparams(("parallel",)),
        name="mlp_ln",
    )(x, w1, w2, g, b)


def _unit_lower_inverse(a, n_factors):
    c = a.shape[0]
    row = lax.broadcasted_iota(jnp.int32, (c, c), 0)
    col = lax.broadcasted_iota(jnp.int32, (c, c), 1)
    eye = (row == col).astype(F32)
    p = -a
    t = eye + p
    for _ in range(n_factors - 1):
        pb = p.astype(BF16)
        p = _dot(pb, pb)
        t = t + _dot(t.astype(BF16), p.astype(BF16))
    return t


def _bdot(a, b):
    return lax.dot_general(a, b, (((2,), (1,)), ((0,), (0,))), preferred_element_type=F32)


def _bdot_nt(a, b):
    return lax.dot_general(a, b, (((2,), (2,)), ((0,), (0,))), preferred_element_type=F32)


def _bdot_tn(a, b):
    return lax.dot_general(a, b, (((1,), (1,)), ((0,), (0,))), preferred_element_type=F32)


def _unit_lower_inverse_batched(a):
    c = a.shape[-1]
    row = lax.broadcasted_iota(jnp.int32, (c, c), 0)
    col = lax.broadcasted_iota(jnp.int32, (c, c), 1)
    def coupling(s):
        couples = ((row // (2 * s)) == (col // (2 * s))) & ((row // s) != (col // s))
        return jnp.where(couples, a, 0.0)

    t = (row == col).astype(F32) - coupling(1)
    s = 2
    while s < c:
        tb = t.astype(BF16)
        t = t - _bdot(_bdot(tb, coupling(s).astype(BF16)).astype(BF16), tb)
        s *= 2
    return t


def _pairwise_diff(col):
    c = col.shape[0]
    lane = lax.broadcasted_iota(jnp.int32, (c, LANES), 1)
    left = jnp.where(lane == 0, col, jnp.where(lane == 1, 1.0, 0.0))
    right = jnp.where(lane == 0, 1.0, jnp.where(lane == 1, -col, 0.0))
    return _dot_nt(left, right, precision=HIGHEST)


def _l2norm_rows(x):
    return x * lax.rsqrt(jnp.sum(x * x, axis=-1, keepdims=True) + L2_EPS)


def _gated_rmsnorm(o, z, norm_w):
    on = o * lax.rsqrt(jnp.mean(o * o, axis=-1, keepdims=True) + RMS_EPS) * norm_w
    return on * _silu(z)


def _gdn_gates(ba, a_log, dt_bias):
    beta = _sigmoid(ba[:, :GDN_V_HEADS])
    g = -jnp.exp(a_log) * _softplus(ba[:, GDN_V_HEADS:] + dt_bias)
    return beta, g


def _gdn_prompt_kernel(n_tokens, mixed_ref, z_ref, ba_ref, convw_ref, alog_ref, dtb_ref, normw_ref,
                       o_ref, sfin_ref, xc_ref, s_ref):
    c = pl.program_id(1)
    C = GDN_CHUNK
    W = GDN_CONV_WIDTH

    @pl.when(c == 0)
    def _():
        xc_ref[...] = jnp.zeros_like(xc_ref)
        s_ref[...] = jnp.zeros_like(s_ref)

    x = mixed_ref[...]
    tail = xc_ref[...]
    w = convw_ref[...]
    row8 = lax.broadcasted_iota(jnp.int32, (SUBLANES, 1), 0)
    acc = x * w[W - 1:W, :]
    for s in range(1, W):
        shifted = pltpu.roll(x, s, axis=0)
        head = jnp.where(row8 < s, pltpu.roll(tail, s, axis=0), shifted[:SUBLANES])
        acc = acc + jnp.concatenate([head, shifted[SUBLANES:]], axis=0) * w[W - 1 - s:W - s, :]
    qkv = _silu(acc)
    xc_ref[...] = x[C - SUBLANES:]

    row = lax.broadcasted_iota(jnp.int32, (C, 1), 0)
    valid = (c * C + row) < n_tokens
    ba = ba_ref[...]
    beta = jnp.where(valid, _sigmoid(ba), 0.0)
    g = jnp.where(valid, -jnp.exp(alog_ref[...]) * _softplus(ba + dtb_ref[...]), 0.0)

    ri = lax.broadcasted_iota(jnp.int32, (C, C), 0)
    ci = lax.broadcasted_iota(jnp.int32, (C, C), 1)
    tril = ci <= ri
    strict = ci < ri
    gc = jnp.dot(tril.astype(F32), g, preferred_element_type=F32, precision=HIGHEST)
    gc_rows = gc.T
    egc = jnp.exp(gc)
    glast = gc[C - 1:C, :]
    ekd = jnp.exp(glast - gc)
    eglast = jnp.exp(glast)
    normw = normw_ref[...]
    NH = GDN_V_HEADS

    def gcol(x, h):
        return x[:, NH + h:NH + h + 1]

    qk_raw = [qkv[:, i * GDN_HEAD:(i + 1) * GDN_HEAD] for i in range(2 * GDN_K_HEADS)]
    sq = jnp.concatenate([t * t for t in qk_raw], axis=0)
    sq_hi = sq.astype(BF16)
    sq_lo = (sq - sq_hi.astype(F32)).astype(BF16)
    ones = jnp.ones((GDN_HEAD, GDN_HEAD), BF16)
    inv_norm = lax.rsqrt(_dot(sq_hi, ones) + _dot(sq_lo, ones) + L2_EPS)
    qk_n = [t * inv_norm[i * C:(i + 1) * C] for i, t in enumerate(qk_raw)]
    q_l = [t * (GDN_HEAD ** -0.5) for t in qk_n[:GDN_K_HEADS]]
    k_l = qk_n[GDN_K_HEADS:]
    k8 = jnp.stack(k_l).astype(BF16)
    kk8 = _bdot_nt(k8, k8)
    qk8 = _bdot_nt(jnp.stack(q_l).astype(BF16), k8)
    a_l, qkd_l, rhs_l, qg_l, kd_l = [], [], [], [], []
    for h in range(NH):
        kh = h // 2
        v = qkv[:, 2 * GDN_KEY_DIM + h * GDN_HEAD:2 * GDN_KEY_DIM + (h + 1) * GDN_HEAD]
        bcol = beta[:, h:h + 1]
        diff = gcol(gc, h) - gc_rows[NH + h:NH + h + 1, :]
        decay = jnp.where(tril, jnp.exp(jnp.where(tril, diff, 0.0)), 0.0)
        a_l.append(jnp.where(strict, kk8[kh] * decay * bcol, 0.0))
        qkd_l.append((qk8[kh] * decay).astype(BF16))
        rhs_l.append(jnp.concatenate([v * bcol, k_l[kh] * (bcol * gcol(egc, h))], axis=1).astype(BF16))
        qg_l.append(q_l[kh] * gcol(egc, h))
        kd_l.append((k_l[kh] * gcol(ekd, h)).astype(BF16))
    t_inv = _unit_lower_inverse_batched(jnp.stack(a_l)).astype(BF16)
    sol = _bdot(t_inv, jnp.stack(rhs_l))
    u, wm = sol[:, :, :GDN_HEAD], sol[:, :, GDN_HEAD:]
    s_old = s_ref[...]
    lhs = jnp.concatenate([wm, jnp.stack(qg_l)], axis=1).astype(BF16)
    ws_qs = _bdot(lhs, s_old.astype(BF16))
    v_new = u - ws_qs[:, :C]
    vnb = v_new.astype(BF16)
    o = ws_qs[:, C:] + _bdot(jnp.stack(qkd_l), vnb)
    upd = _bdot_tn(jnp.stack(kd_l), vnb)
    for h in range(NH):
        s_ref[h] = s_old[h] * gcol(eglast, h) + upd[h]
        zh = z_ref[:, h * GDN_HEAD:(h + 1) * GDN_HEAD]
        o_ref[:, h * GDN_HEAD:(h + 1) * GDN_HEAD] = _gated_rmsnorm(o[h], zh, normw).astype(o_ref.dtype)

    @pl.when(c == pl.num_programs(1) - 1)
    def _():
        sfin_ref[0] = s_ref[...]


def _gdn_prompt(proj, ba, conv_w, a_log, dt_bias, norm_w, n_batch, lp, n_tokens):
    C = GDN_CHUNK
    per_b = lp // C
    n_chunks = per_b
    z_off = GDN_CONV_DIM // GDN_VALUE_DIM
    kern = functools.partial(_gdn_prompt_kernel, n_tokens)
    return pl.pallas_call(
        kern,
        grid=(n_batch, n_chunks),
        in_specs=[pl.BlockSpec((C, GDN_CONV_DIM), lambda b, c: (b * per_b + c, 0)),
                  pl.BlockSpec((C, GDN_VALUE_DIM), lambda b, c: (b * per_b + c, z_off)),
                  pl.BlockSpec((C, LANES), lambda b, c: (b * per_b + c, 0)),
                  pl.BlockSpec((GDN_CONV_WIDTH, GDN_CONV_DIM), lambda b, c: (0, 0)),
                  pl.BlockSpec((1, LANES), lambda b, c: (0, 0)),
                  pl.BlockSpec((1, LANES), lambda b, c: (0, 0)),
                  pl.BlockSpec((1, GDN_HEAD), lambda b, c: (0, 0))],
        out_specs=[pl.BlockSpec((C, GDN_VALUE_DIM), lambda b, c: (b * per_b + c, 0)),
                   pl.BlockSpec((1, GDN_V_HEADS, GDN_HEAD, GDN_HEAD), lambda b, c: (b, 0, 0, 0))],
        out_shape=[jax.ShapeDtypeStruct((proj.shape[0], GDN_VALUE_DIM), BF16),
                   jax.ShapeDtypeStruct((n_batch, GDN_V_HEADS, GDN_HEAD, GDN_HEAD), F32)],
        scratch_shapes=[pltpu.VMEM((SUBLANES, GDN_CONV_DIM), F32),
                        pltpu.VMEM((GDN_V_HEADS, GDN_HEAD, GDN_HEAD), F32)],
        compiler_params=_cparams(("parallel", "arbitrary")),
        name="gdn_prompt",
    )(proj, proj, ba, conv_w, a_log, dt_bias, norm_w)


def _twice(dst_ref, x):
    t = x.shape[0]
    dst_ref[0:t, :] = x
    dst_ref[t:2 * t, :] = x
    return dst_ref[...]


SAMPLE_SEQS_PER_STEP = 2


def _gdn_sample_kernel(cs_ref, mixed_ref, z_ref, ba_ref, convw_ref, alog_ref, dtb_ref, normw_ref, s_ref,
                       o_ref, snew_ref, xc_ref, dq_ref, dz_ref, dba_ref):
    for i in range(mixed_ref.shape[0]):
        one = pl.ds(i, 1)
        _gdn_sample_seq(cs_ref.at[one], mixed_ref.at[one], z_ref.at[one], ba_ref.at[one], convw_ref, alog_ref,
                        dtb_ref, normw_ref, s_ref.at[one], o_ref.at[one], snew_ref.at[one],
                        xc_ref.at[i], dq_ref.at[i], dz_ref.at[i], dba_ref.at[i])


def _gdn_sample_seq(cs_ref, mixed_ref, z_ref, ba_ref, convw_ref, alog_ref, dtb_ref, normw_ref, s_ref,
                    o_ref, snew_ref, xc_ref, dq_ref, dz_ref, dba_ref):
    T = mixed_ref.shape[1]
    R = 2 * T
    W = GDN_CONV_WIDTH
    xc_ref[0:W - 1, :] = cs_ref[0]
    xc_ref[W - 1:W - 1 + T, :] = mixed_ref[0]
    w = convw_ref[...]
    acc = xc_ref[0:T, :] * w[0:1, :]
    for j in range(1, W):
        acc = acc + xc_ref[j:j + T, :] * w[j:j + 1, :]
    qkv = _twice(dq_ref, _silu(acc))
    z2 = _twice(dz_ref, z_ref[0])
    ba2 = _twice(dba_ref, ba_ref[0])
    beta, g = _gdn_gates(ba2, alog_ref[...], dtb_ref[...])

    rr = lax.broadcasted_iota(jnp.int32, (R, 1), 0)
    tmod = rr % T
    first = rr < T
    gc = jnp.zeros_like(g)
    for s in range(T):
        gc = gc + jnp.where(tmod >= s, g[s:s + 1, :], 0.0)
    glast = gc[T - 1:T, :]
    normw = normw_ref[...]

    NP = GDN_K_HEADS
    M = NP * R

    def pair_col(x, kh):
        return jnp.where(first, x[:, 2 * kh:2 * kh + 1], x[:, 2 * kh + 1:2 * kh + 2])

    def pair_tile(x, base, kh):
        a = x[:, base + (2 * kh) * GDN_HEAD:base + (2 * kh + 1) * GDN_HEAD]
        b = x[:, base + (2 * kh + 1) * GDN_HEAD:base + (2 * kh + 2) * GDN_HEAD]
        return jnp.where(first, a, b)

    q_t, k_t, v_t, z_t, b_c, gc_c, gl_c = [], [], [], [], [], [], []
    for kh in range(NP):
        q_t.append(_l2norm_rows(qkv[:, kh * GDN_HEAD:(kh + 1) * GDN_HEAD]) * (GDN_HEAD ** -0.5))
        k_t.append(_l2norm_rows(qkv[:, GDN_KEY_DIM + kh * GDN_HEAD:GDN_KEY_DIM + (kh + 1) * GDN_HEAD]))
        v_t.append(pair_tile(qkv, 2 * GDN_KEY_DIM, kh))
        z_t.append(pair_tile(z2, 0, kh))
        b_c.append(pair_col(beta, kh))
        gc_c.append(pair_col(gc, kh))
        gl_c.append(pair_col(jnp.broadcast_to(glast, (R, GDN_V_HEADS)), kh))
    q64 = jnp.concatenate(q_t, axis=0)
    k64 = jnp.concatenate(k_t, axis=0)
    v64 = jnp.concatenate(v_t, axis=0)
    bcol = jnp.concatenate(b_c, axis=0)
    gcol = jnp.concatenate(gc_c, axis=0)
    glcol = jnp.concatenate(gl_c, axis=0)

    ri = lax.broadcasted_iota(jnp.int32, (M, M), 0)
    ci = lax.broadcasted_iota(jnp.int32, (M, M), 1)
    same = (ri // T) == (ci // T)
    tril = same & ((ci % T) <= (ri % T))
    strict = same & ((ci % T) < (ri % T))
    kb = k64.astype(BF16)
    kk = _dot_nt(kb, kb)
    qk0 = _dot_nt(q64.astype(BF16), kb)
    diff = _pairwise_diff(gcol)
    decay = jnp.where(tril, jnp.exp(jnp.where(tril, diff, 0.0)), 0.0)
    a_mat = jnp.where(strict, kk * decay * bcol, 0.0)
    t_inv = _unit_lower_inverse(a_mat, max(1, (T - 1).bit_length())).astype(BF16)
    egc = jnp.exp(gcol)
    u = _dot(t_inv, (v64 * bcol).astype(BF16))
    wm = _dot(t_inv, (k64 * (bcol * egc)).astype(BF16))
    qg = q64 * egc
    kd = k64 * jnp.exp(glcol - gcol)
    qkd = (qk0 * decay).astype(BF16)

    v_new_t, qs_t = [], []
    for kh in range(NP):
        lhs = jnp.concatenate([wm[kh * R:(kh + 1) * R], qg[kh * R:(kh + 1) * R]], axis=0).astype(BF16)
        r0 = _dot(lhs, s_ref[0, 2 * kh].astype(BF16))
        r1 = _dot(lhs, s_ref[0, 2 * kh + 1].astype(BF16))
        ws = jnp.where(first, r0[:R], r1[:R])
        qs_t.append(jnp.where(first, r0[R:], r1[R:]))
        v_new_t.append(u[kh * R:(kh + 1) * R] - ws)
    v_new = jnp.concatenate(v_new_t, axis=0)
    o64 = jnp.concatenate(qs_t, axis=0) + _dot(qkd, v_new.astype(BF16))
    on = _gated_rmsnorm(o64, jnp.concatenate(z_t, axis=0), normw)

    for kh in range(NP):
        vn = v_new_t[kh].astype(BF16)
        kdp = kd[kh * R:(kh + 1) * R]
        for j in range(2):
            h = 2 * kh + j
            keep = first if j == 0 else jnp.logical_not(first)
            kdm = jnp.where(keep, kdp, 0.0).astype(BF16)
            eg = jnp.exp(glast[:, h:h + 1])
            snew_ref[0, h] = s_ref[0, h] * eg + _dot_tn(kdm, vn)
            tile = on[kh * R:(kh + 1) * R]
            if j == 1:
                tile = pltpu.roll(tile, T, axis=0)
            o_ref[0, :, h * GDN_HEAD:(h + 1) * GDN_HEAD] = tile[:T].astype(o_ref.dtype)


def _gdn_sample(conv_state, mixed, z, ba, conv_w, a_log, dt_bias, norm_w, state):
    nb, t, _ = mixed.shape
    sq = SAMPLE_SEQS_PER_STEP
    assert nb % sq == 0
    seq = lambda *tail: pl.BlockSpec((sq,) + tail, lambda b: (b,) + (0,) * len(tail))
    return pl.pallas_call(
        _gdn_sample_kernel,
        grid=(nb // sq,),
        in_specs=[seq(GDN_CONV_WIDTH - 1, GDN_CONV_DIM), seq(t, GDN_CONV_DIM), seq(t, GDN_VALUE_DIM),
                  seq(t, 2 * GDN_V_HEADS),
                  _resident((GDN_CONV_WIDTH, GDN_CONV_DIM)), _resident((1, GDN_V_HEADS)),
                  _resident((1, GDN_V_HEADS)), _resident((1, GDN_HEAD)),
                  seq(GDN_V_HEADS, GDN_HEAD, GDN_HEAD)],
        out_specs=[seq(t, GDN_VALUE_DIM), seq(GDN_V_HEADS, GDN_HEAD, GDN_HEAD)],
        out_shape=[jax.ShapeDtypeStruct((nb, t, GDN_VALUE_DIM), BF16),
                   jax.ShapeDtypeStruct(state.shape, F32)],
        scratch_shapes=[pltpu.VMEM((sq, GDN_CONV_WIDTH - 1 + t, GDN_CONV_DIM), F32),
                        pltpu.VMEM((sq, 2 * t, GDN_CONV_DIM), F32),
                        pltpu.VMEM((sq, 2 * t, GDN_VALUE_DIM), F32),
                        pltpu.VMEM((sq, 2 * t, 2 * GDN_V_HEADS), F32)],
        compiler_params=_cparams(("parallel",)),
        name="gdn_sample",
    )(conv_state, mixed, z, ba, conv_w, a_log, dt_bias, norm_w, state)


def _rope_tables(pos, half, period):
    inv_freq = ROPE_THETA ** (-jnp.arange(half, dtype=F32) * 2.0 / (2 * half))
    ang = pos.astype(F32)[:, None] * inv_freq[None, :]
    cos, sin = jnp.cos(ang), jnp.sin(ang)
    ones = jnp.ones((pos.shape[0], period - 2 * half), F32)
    cos_p = jnp.concatenate([cos, cos, ones], axis=1)
    sin_p = jnp.concatenate([-sin, sin, 0.0 * ones], axis=1)
    reps = LANES // period
    return jnp.tile(cos_p, (1, reps)), jnp.tile(sin_p, (1, reps))


def _rope_tile(x, cos, sin, half, period):
    lane = lax.broadcasted_iota(jnp.int32, x.shape, 1) % period
    partner = jnp.where(lane < half, pltpu.roll(x, LANES - half, axis=1), pltpu.roll(x, half, axis=1))
    return x * cos + partner * sin


def _dsa_proj_kernel(x_ref, w_ref, ca_ref, sa_ref, ci_ref, si_ref, g_ref, b_ref,
                     q_ref, k_ref, v_ref, vt_ref, iq_ref, ikw_ref, ik2_ref):
    x = x_ref[...].astype(BF16)
    ca, sa, ci, si = ca_ref[...], sa_ref[...], ci_ref[...], si_ref[...]
    q_off, k_off = 0, ATT_HEADS * ATT_HEAD_DIM
    v_off = k_off + ATT_KV_HEADS * ATT_HEAD_DIM
    iq_off = v_off + ATT_KV_HEADS * ATT_HEAD_DIM
    ik_off = iq_off + IDX_HEADS * IDX_HEAD_DIM

    def proj(off):
        return _dot(x, w_ref[:, off:off + LANES])

    for h in range(ATT_HEADS):
        t = _rope_tile(proj(q_off + h * LANES), ca, sa, ATT_ROT_HALF, LANES)
        q_ref[:, h * LANES:(h + 1) * LANES] = (t * (ATT_HEAD_DIM ** -0.5)).astype(q_ref.dtype)
    for h in range(ATT_KV_HEADS):
        k_ref[:, h * LANES:(h + 1) * LANES] = _rope_tile(proj(k_off + h * LANES), ca, sa, ATT_ROT_HALF, LANES)
        v = proj(v_off + h * LANES)
        v_ref[:, h * LANES:(h + 1) * LANES] = v
        vt_ref[h * LANES:(h + 1) * LANES, :] = v.T.astype(vt_ref.dtype)
    for h in range(IDX_HEADS * IDX_HEAD_DIM // LANES):
        t = _rope_tile(proj(iq_off + h * LANES), ci, si, IDX_ROT_HALF, IDX_HEAD_DIM)
        iq_ref[:, h * LANES:(h + 1) * LANES] = t.astype(iq_ref.dtype)
    t = proj(ik_off)
    lane = lax.broadcasted_iota(jnp.int32, t.shape, 1)
    is_ik = lane < IDX_HEAD_DIM
    mu = jnp.sum(jnp.where(is_ik, t, 0.0), axis=-1, keepdims=True) / IDX_HEAD_DIM
    tc = jnp.where(is_ik, t - mu, 0.0)
    var = jnp.sum(tc * tc, axis=-1, keepdims=True) / IDX_HEAD_DIM
    ik = _rope_tile(tc * lax.rsqrt(var + LN_EPS) * g_ref[...] + b_ref[...], ci, si, IDX_ROT_HALF, IDX_HEAD_DIM)
    ikw = jnp.where(is_ik, ik, t * (IDX_HEADS ** -0.5))
    ikw_ref[...] = ikw
    ik_only = jnp.where(is_ik, ik, 0.0)
    ik2_ref[...] = (ik_only + pltpu.roll(ik_only, IDX_HEAD_DIM, axis=1)).astype(ik2_ref.dtype)


def _dsa_proj(x, w, tables, ik_g, ik_b):
    rows, d = x.shape
    n = w.shape[1]
    kv = ATT_KV_HEADS * ATT_HEAD_DIM
    row_spec = lambda width: pl.BlockSpec((ROW_TILE, width), lambda i: (i, 0))
    const_spec = _resident
    return pl.pallas_call(
        _dsa_proj_kernel,
        grid=(rows // ROW_TILE,),
        in_specs=[row_spec(d), const_spec((d, n))] + [row_spec(LANES)] * 4 + [const_spec((1, LANES))] * 2,
        out_specs=[row_spec(ATT_HEADS * ATT_HEAD_DIM), row_spec(kv), row_spec(kv),
                   pl.BlockSpec((kv, ROW_TILE), lambda i: (0, i)),
                   row_spec(IDX_HEADS * IDX_HEAD_DIM), row_spec(LANES), row_spec(LANES)],
        out_shape=[jax.ShapeDtypeStruct((rows, ATT_HEADS * ATT_HEAD_DIM), BF16),
                   jax.ShapeDtypeStruct((rows, kv), F32),
                   jax.ShapeDtypeStruct((rows, kv), F32),
                   jax.ShapeDtypeStruct((kv, rows), BF16),
                   jax.ShapeDtypeStruct((rows, IDX_HEADS * IDX_HEAD_DIM), BF16),
                   jax.ShapeDtypeStruct((rows, LANES), F32),
                   jax.ShapeDtypeStruct((rows, LANES), BF16)],
        compiler_params=_cparams(("parallel",)),
        name="dsa_proj",
    )(x, w, *tables, ik_g, ik_b)


def _float_order_key(x):
    b = pltpu.bitcast(x, jnp.int32)
    return b ^ ((b >> 31) & jnp.int32(0x7FFFFFFF))


def _float_from_key(k):
    return pltpu.bitcast(k ^ ((k >> 31) & jnp.int32(0x7FFFFFFF)), F32)


BISECT_UNROLL = 8
VALUE_ROUNDS = 2
KEY_ROUNDS = 32 // BISECT_UNROLL
F32_MAX = 3.4028234663852886e38


def _kth_largest_bounds(count_ge, target, amax):
    bound = jnp.minimum(2.0 * amax, F32_MAX)
    lo = _float_order_key(-bound)
    hi = _float_order_key(bound) + 1
    c_lo = jnp.full(target.shape, -1.0, F32)

    def step(carry, split_values):
        lo, hi, c_lo = carry
        mid = (lo >> 1) + (hi >> 1) + (lo & hi & 1)
        if split_values:
            vmid = _float_order_key(_float_from_key(lo) * 0.5 + _float_from_key(hi) * 0.5)
            mid = jnp.where((vmid > lo) & (vmid < hi), vmid, mid)
        cnt = count_ge(_float_from_key(mid))
        ok = cnt >= target
        return jnp.where(ok, mid, lo), jnp.where(ok, hi, mid), jnp.where(ok, cnt, c_lo)

    def rounds(carry, n_rounds, split_values):
        def cond(state):
            it, (lo, hi, c_lo) = state
            pending = (c_lo != target) & (hi != lo + 1)
            return (it < n_rounds) & (jnp.max(jnp.where(pending, 1.0, 0.0)) > 0.0)

        def body(state):
            it, carry = state
            for _ in range(BISECT_UNROLL):
                carry = step(carry, split_values)
            return it + 1, carry

        return lax.while_loop(cond, body, (jnp.int32(0), carry))[1]

    carry = rounds((lo, hi, c_lo), VALUE_ROUNDS, True)
    lo, hi, _ = rounds(carry, KEY_ROUNDS, False)
    return _float_from_key(lo), _float_from_key(hi)


def _dsa_prompt_kernel(topk, q_ref, iq_ref, ikw_ref, k_ref, vt_ref, ik2_ref, o_ref,
                       s_ref, xh_ref, qg_ref, sc_ref, acc_ref):
    i = pl.program_id(1)
    KT = KEY_TILE
    CH = KEY_CHUNK
    nc = i // (CH // KT) + 1
    row = lax.broadcasted_iota(jnp.int32, (CH, KT), 0)
    lane = lax.broadcasted_iota(jnp.int32, (KT, KT), 1)
    qpos = i * KT + lax.broadcasted_iota(jnp.int32, (1, KT), 1)
    GQ = ATT_GROUP * KT

    def chunk(c):
        return pl.ds(pl.multiple_of(c * CH, CH), CH)

    NACC = 8

    def fold_rows(x):
        return x.reshape(CH // (NACC * SUBLANES), NACC, SUBLANES, x.shape[-1])

    def unfold(x, op):
        return op(op(x, axis=0), axis=0, keepdims=True)

    for h in range(IDX_HEADS):
        tile = iq_ref[:, (h // 2) * LANES:(h // 2 + 1) * LANES]
        mine = (lane // IDX_HEAD_DIM) == (h % 2)
        xh_ref[h * KT:(h + 1) * KT, :] = jnp.where(mine, tile, jnp.zeros_like(tile))
    for g in range(ATT_KV_HEADS):
        for hq in range(ATT_GROUP):
            h = g * ATT_GROUP + hq
            qg_ref[g, hq * KT:(hq + 1) * KT, :] = q_ref[:, h * LANES:(h + 1) * LANES]
    w_rows = ikw_ref[...].T[IDX_HEAD_DIM:IDX_HEAD_DIM + IDX_HEADS, :]

    n_pairs = (nc + 1) // 2

    def pair(j):
        return 2 * j, jnp.minimum(2 * j + 1, nc - 1)

    def score_chunk(c, amax):
        d = _dot_nt(ik2_ref[chunk(c), :], xh_ref[...])
        acc = w_rows[0:1, :] * jnp.maximum(d[:, 0:KT], 0.0)
        for h in range(1, IDX_HEADS):
            acc = acc + w_rows[h:h + 1, :] * jnp.maximum(d[:, h * KT:(h + 1) * KT], 0.0)
        acc = acc * (IDX_HEAD_DIM ** -0.5)
        kpos = c * CH + row
        causal = kpos <= qpos
        meta = kpos < N_META
        s_ref[chunk(c), :] = jnp.where(causal, jnp.where(meta, jnp.inf, acc), NEG_INF)
        finite = jnp.where(causal & jnp.logical_not(meta), jnp.abs(acc), 0.0)
        return jnp.maximum(amax, jnp.max(fold_rows(finite), axis=0))

    def score_pair(j, amax):
        c0, c1 = pair(j)
        return score_chunk(c1, score_chunk(c0, amax))

    amax = unfold(lax.fori_loop(0, n_pairs, score_pair, jnp.zeros((NACC, SUBLANES, KT), F32)), jnp.max)

    target = jnp.minimum(qpos + 1, topk).astype(F32)

    def count_ge(thr):
        def body(c, acc):
            hit = jnp.where(s_ref[chunk(c), :] >= thr, 1.0, 0.0)
            return acc + jnp.sum(fold_rows(hit), axis=0)
        return unfold(lax.fori_loop(0, nc, body, jnp.zeros((NACC, SUBLANES, KT), F32)), jnp.sum)

    lo, hi = _kth_largest_bounds(count_ge, target, amax)
    surplus = count_ge(lo) - target

    @pl.when(jnp.max(surplus) > 0.0)
    def _():
        need = target - count_ge(hi)
        rr = lax.broadcasted_iota(jnp.int32, (CH, CH), 0)
        cc = lax.broadcasted_iota(jnp.int32, (CH, CH), 1)
        tri = (cc <= rr).astype(BF16)

        def drop(c, carry):
            t = s_ref[chunk(c), :]
            tie = (t >= lo) & jnp.logical_not(t >= hi)
            tie_f = jnp.where(tie, 1.0, 0.0)
            rank = carry + _dot(tri, tie_f.astype(BF16))
            s_ref[chunk(c), :] = jnp.where(tie & (rank > need), NEG_INF, t)
            return carry + jnp.sum(tie_f, axis=0, keepdims=True)

        lax.fori_loop(0, nc, drop, jnp.zeros((1, KT), F32))

    def bias_cols(c):
        b = jnp.where(s_ref[chunk(c), :] >= lo, 0.0, NEG_INF)
        return jnp.concatenate([b] * ATT_GROUP, axis=1)

    G = ATT_KV_HEADS

    def max_chunk(c, macc):
        bias = bias_cols(c)
        tops = []
        for g in range(G):
            sc = _dot_nt(k_ref[chunk(c), g * LANES:(g + 1) * LANES], qg_ref[g]) + bias
            sc_ref[chunk(c), g * GQ:(g + 1) * GQ] = sc
            tops.append(jnp.max(fold_rows(sc), axis=0))
        return jnp.maximum(macc, jnp.concatenate(tops, axis=-1))

    def max_pair(j, macc):
        c0, c1 = pair(j)
        return max_chunk(c1, max_chunk(c0, macc))

    macc = lax.fori_loop(0, n_pairs, max_pair, jnp.full((NACC, SUBLANES, G * GQ), NEG_INF, F32))
    m = unfold(macc, jnp.max)

    ones_rows = jnp.ones((2 * SUBLANES, CH), BF16)
    acc_ref[...] = jnp.zeros_like(acc_ref)

    def sum_chunk(c, shift):
        for g in range(G):
            p = jnp.exp(sc_ref[chunk(c), g * GQ:(g + 1) * GQ] - shift[:, g * GQ:(g + 1) * GQ]).astype(BF16)
            vt = jnp.concatenate([vt_ref[g * LANES:(g + 1) * LANES, chunk(c)], ones_rows], axis=0)
            acc_ref[g] += _dot(vt, p)

    def sum_pair(j, _):
        c0, c1 = pair(j)
        sum_chunk(c0, m)
        sum_chunk(c1, jnp.where(2 * j + 1 < nc, m, jnp.inf))
        return 0

    lax.fori_loop(0, n_pairs, sum_pair, 0)
    for g in range(G):
        acc = acc_ref[g]
        out_t = acc[:ATT_HEAD_DIM] / acc[ATT_HEAD_DIM:ATT_HEAD_DIM + 1]
        for hq in range(ATT_GROUP):
            h = g * ATT_GROUP + hq
            o_ref[:, h * LANES:(h + 1) * LANES] = out_t[:, hq * KT:(hq + 1) * KT].T.astype(o_ref.dtype)


def _dsa_prompt(q, iq, ikw, k, vt, ik2, n_batch, lp, topk):
    KT = KEY_TILE
    nq = lp // KT
    kv = ATT_KV_HEADS * ATT_HEAD_DIM
    qspec = lambda width: pl.BlockSpec((KT, width), lambda b, i: (b * nq + i, 0))
    return pl.pallas_call(
        functools.partial(_dsa_prompt_kernel, topk),
        grid=(n_batch, nq),
        in_specs=[qspec(ATT_HEADS * ATT_HEAD_DIM), qspec(IDX_HEADS * IDX_HEAD_DIM), qspec(LANES),
                  pl.BlockSpec((lp, kv), lambda b, i: (b, 0)),
                  pl.BlockSpec((kv, lp), lambda b, i: (0, b)),
                  pl.BlockSpec((lp, LANES), lambda b, i: (b, 0))],
        out_specs=qspec(ATT_HEADS * ATT_HEAD_DIM),
        out_shape=jax.ShapeDtypeStruct(q.shape, BF16),
        scratch_shapes=[pltpu.VMEM((lp, KT), F32),
                        pltpu.VMEM((IDX_HEADS * KT, LANES), BF16),
                        pltpu.VMEM((ATT_KV_HEADS, ATT_GROUP * KT, ATT_HEAD_DIM), BF16),
                        pltpu.VMEM((lp, ATT_HEADS * KT), F32),
                        pltpu.VMEM((ATT_KV_HEADS, ATT_HEAD_DIM + 2 * SUBLANES, ATT_GROUP * KT), F32)],
        compiler_params=_cparams(("parallel", "arbitrary")),
        name="dsa_prompt",
    )(q, iq, ikw, k, vt, ik2)


SAMPLE_GROUP = 16


def _dsa_sample_select_kernel(topk, n_pages, pt_ref, iq_ref, ikw_ref, *rest):
    del pt_ref
    page_refs = rest[:n_pages]
    sel_ref, lo_ref, stack_ref, st_ref, iq8_ref, ikw8_ref, newk_ref = rest[n_pages:]
    b = pl.program_id(0)
    j = b % SAMPLE_GROUP
    T = iq_ref.shape[1]
    R = 2 * T
    past = n_pages * PAGE
    n_keys = stack_ref.shape[1]
    GR = SAMPLE_GROUP * R

    iq8_ref[0:T, :] = iq_ref[0].astype(F32)
    iq8_ref[T:R, :] = iq_ref[0].astype(F32)
    ikw8_ref[0:T, :] = ikw_ref[0]
    ikw8_ref[T:R, :] = ikw_ref[0]
    lane = lax.broadcasted_iota(jnp.int32, (R, LANES), 1)
    ikw8 = ikw8_ref[...]
    xh = []
    for h in range(IDX_HEADS):
        tile = iq8_ref[:, (h // 2) * LANES:(h // 2 + 1) * LANES]
        if h % 2:
            tile = pltpu.roll(tile, IDX_HEAD_DIM, axis=1)
        xh.append(jnp.where(lane < IDX_HEAD_DIM, tile, 0.0))
    x_all = jnp.concatenate(xh, axis=0).astype(BF16)

    def index_scores(d):
        acc = jnp.zeros((R, LANES), F32)
        for h in range(IDX_HEADS):
            wcol = ikw8[:, IDX_HEAD_DIM + h:IDX_HEAD_DIM + h + 1]
            acc = acc + wcol * jnp.maximum(d[h * R:(h + 1) * R], 0.0)
        return acc * (IDX_HEAD_DIM ** -0.5)

    rows = pl.ds(pl.multiple_of(j * R, R), R)
    zeros64 = jnp.zeros((LANES - IDX_HEAD_DIM, PAGE), BF16)
    for p in range(n_pages):
        keys_t = jnp.concatenate([page_refs[p][0].astype(BF16), zeros64], axis=0)
        sc = index_scores(_dot(x_all, keys_t))
        if p == 0:
            sc = jnp.where(lane < N_META, jnp.inf, sc)
        stack_ref[rows, p * PAGE:(p + 1) * PAGE] = sc
    tmod = lax.broadcasted_iota(jnp.int32, (R, 1), 0) % T
    newk_ref[...] = jnp.zeros_like(newk_ref)
    newk_ref[0:T, :] = ikw_ref[0]
    nk = newk_ref[...]
    lane_k = lax.broadcasted_iota(jnp.int32, nk.shape, 1)
    sc_new = index_scores(_dot_nt(x_all, jnp.where(lane_k < IDX_HEAD_DIM, nk, 0.0).astype(BF16)))
    stack_ref[rows, past:n_keys] = jnp.where(lane <= tmod, sc_new, NEG_INF)

    @pl.when(j == SAMPLE_GROUP - 1)
    def _():
        st_ref[...] = stack_ref[...].T
        qlane = lax.broadcasted_iota(jnp.int32, (1, GR), 1)
        qpos = past + (qlane % R) % T
        target = jnp.minimum(qpos + 1, topk).astype(F32)
        NACC = 8

        def count_ge(thr):
            hit = jnp.where(st_ref[...] >= thr, 1.0, 0.0)
            part = jnp.sum(hit.reshape(n_keys // (NACC * SUBLANES), NACC, SUBLANES, GR), axis=0)
            return jnp.sum(jnp.sum(part, axis=0), axis=0, keepdims=True)

        mag = jnp.abs(st_ref[...])
        mag = jnp.where(mag < jnp.inf, mag, 0.0)
        amax = jnp.max(jnp.max(mag.reshape(n_keys // SUBLANES, SUBLANES, GR), axis=0), axis=0, keepdims=True)
        lo, hi = _kth_largest_bounds(count_ge, target, amax)
        surplus = count_ge(lo) - target

        @pl.when(jnp.max(surplus) > 0.0)
        def _():
            need = target - count_ge(hi)
            rr = lax.broadcasted_iota(jnp.int32, (LANES, LANES), 0)
            cc = lax.broadcasted_iota(jnp.int32, (LANES, LANES), 1)
            tri = (cc <= rr).astype(BF16)
            carry = jnp.zeros((1, GR), F32)
            for kt in range(n_keys // LANES):
                t = st_ref[kt * LANES:(kt + 1) * LANES, :]
                tie = (t >= lo) & jnp.logical_not(t >= hi)
                tie_f = jnp.where(tie, 1.0, 0.0)
                rank = carry + _dot(tri, tie_f.astype(BF16))
                st_ref[kt * LANES:(kt + 1) * LANES, :] = jnp.where(tie & (rank > need), NEG_INF, t)
                carry = carry + jnp.sum(tie_f, axis=0, keepdims=True)

        sel_ref[...] = st_ref[...].T
        lo_ref[...] = jnp.broadcast_to(lo, (GR, GR)).T


def _dsa_sample_select(page_table, iq, ikw, cache_ik, topk):
    nb, t, _ = iq.shape
    n_pages = page_table.shape[1]
    n_keys = n_pages * PAGE + LANES
    gr = SAMPLE_GROUP * 2 * t
    assert nb % SAMPLE_GROUP == 0 and gr == LANES
    tok = lambda width: pl.BlockSpec((1, t, width), lambda b, pt: (b, 0, 0))
    page = lambda p: pl.BlockSpec((1, IDX_HEAD_DIM, PAGE), lambda b, pt: (pt[b, p], 0, 0))
    grid_spec = pltpu.PrefetchScalarGridSpec(
        num_scalar_prefetch=1,
        grid=(nb,),
        in_specs=[tok(IDX_HEADS * IDX_HEAD_DIM), tok(LANES)] + [page(p) for p in range(n_pages)],
        out_specs=[pl.BlockSpec((gr, n_keys), lambda b, pt: (b // SAMPLE_GROUP, 0)),
                   pl.BlockSpec((gr, LANES), lambda b, pt: (b // SAMPLE_GROUP, 0))],
        scratch_shapes=[pltpu.VMEM((gr, n_keys), F32),
                        pltpu.VMEM((n_keys, gr), F32),
                        pltpu.VMEM((2 * t, IDX_HEADS * IDX_HEAD_DIM), F32),
                        pltpu.VMEM((2 * t, LANES), F32),
                        pltpu.VMEM((LANES, LANES), F32)])
    return pl.pallas_call(
        functools.partial(_dsa_sample_select_kernel, topk, n_pages),
        grid_spec=grid_spec,
        out_shape=[jax.ShapeDtypeStruct((nb * 2 * t, n_keys), F32),
                   jax.ShapeDtypeStruct((nb * 2 * t, LANES), F32)],
        compiler_params=_cparams(("arbitrary",)),
        name="dsa_sample_select",
    )(page_table, iq, ikw, *([cache_ik] * n_pages))


def _dsa_sample_attend_kernel(n_pages, pt_ref, q_ref, kn_ref, vn_ref, sel_ref, lo_ref, *rest):
    del pt_ref
    n_seq = q_ref.shape[0]
    pages = rest[:2 * n_seq * n_pages]
    o_ref, q8_ref, newk_ref, newv_ref = rest[2 * n_seq * n_pages:]
    R = 2 * q_ref.shape[1]
    for i in range(n_seq):
        one = pl.ds(i, 1)
        _dsa_sample_attend_seq(q_ref.at[one], kn_ref.at[one], vn_ref.at[one],
                               sel_ref.at[pl.ds(i * R, R)], lo_ref.at[pl.ds(i * R, R)],
                               pages[i * n_pages:(i + 1) * n_pages],
                               pages[(n_seq + i) * n_pages:(n_seq + i + 1) * n_pages],
                               o_ref.at[one], q8_ref.at[i], newk_ref.at[i], newv_ref.at[i])


def _dsa_sample_attend_seq(q_ref, kn_ref, vn_ref, sel_ref, lo_ref, k_refs, v_refs, o_ref,
                           q8_ref, newk_ref, newv_ref):
    T = q_ref.shape[1]
    R = 2 * T
    q8_ref[0:T, :] = q_ref[0].astype(F32)
    q8_ref[T:R, :] = q_ref[0].astype(F32)
    newk_ref[...] = jnp.zeros_like(newk_ref)
    newv_ref[...] = jnp.zeros_like(newv_ref)
    newk_ref[0:T, :] = kn_ref[0]
    newv_ref[0:T, :] = vn_ref[0]
    bias8 = jnp.where(sel_ref[...] >= lo_ref[:, 0:1], 0.0, NEG_INF)
    bias = jnp.concatenate([bias8] * ATT_GROUP, axis=0)
    for g in range(ATT_KV_HEADS):
        qg = jnp.concatenate([q8_ref[:, (g * ATT_GROUP + hq) * LANES:(g * ATT_GROUP + hq + 1) * LANES]
                              for hq in range(ATT_GROUP)], axis=0).astype(BF16)
        head_rows = pl.ds(g, PAGE, stride=ATT_KV_HEADS)
        k_all = jnp.concatenate([r[0, head_rows, :].astype(BF16) for r in k_refs]
                                + [newk_ref[:, g * LANES:(g + 1) * LANES].astype(BF16)], axis=0)
        v_all = jnp.concatenate([r[0, head_rows, :].astype(BF16) for r in v_refs]
                                + [newv_ref[:, g * LANES:(g + 1) * LANES].astype(BF16)], axis=0)
        sc = _dot_nt(qg, k_all) + bias
        m = jnp.max(sc, axis=1, keepdims=True)
        pr = jnp.exp(sc - m)
        out = _dot(pr.astype(BF16), v_all) / jnp.sum(pr, axis=1, keepdims=True)
        for hq in range(ATT_GROUP):
            h = g * ATT_GROUP + hq
            o_ref[0, :, h * LANES:(h + 1) * LANES] = out[hq * R:hq * R + T].astype(o_ref.dtype)


def _dsa_sample_attend(page_table, q, k_new, v_new, sel, lo, cache_k, cache_v):
    nb, t, _ = q.shape
    n_pages = page_table.shape[1]
    kv = ATT_KV_HEADS * ATT_HEAD_DIM
    n_keys = sel.shape[1]
    sq = SAMPLE_SEQS_PER_STEP
    assert nb % sq == 0
    tok = lambda width: pl.BlockSpec((sq, t, width), lambda b, pt: (b, 0, 0))
    page = lambda i, p: pl.BlockSpec((1, ATT_KV_HEADS * PAGE, ATT_HEAD_DIM),
                                     lambda b, pt: (pt[b * sq + i, p], 0, 0))
    pages = [page(i, p) for i in range(sq) for p in range(n_pages)]
    grid_spec = pltpu.PrefetchScalarGridSpec(
        num_scalar_prefetch=1,
        grid=(nb // sq,),
        in_specs=[tok(ATT_HEADS * ATT_HEAD_DIM), tok(kv), tok(kv),
                  pl.BlockSpec((sq * 2 * t, n_keys), lambda b, pt: (b, 0)),
                  pl.BlockSpec((sq * 2 * t, LANES), lambda b, pt: (b, 0))] + pages * 2,
        out_specs=tok(ATT_HEADS * ATT_HEAD_DIM),
        scratch_shapes=[pltpu.VMEM((sq, 2 * t, ATT_HEADS * ATT_HEAD_DIM), F32),
                        pltpu.VMEM((sq, LANES, kv), F32),
                        pltpu.VMEM((sq, LANES, kv), F32)])
    return pl.pallas_call(
        functools.partial(_dsa_sample_attend_kernel, n_pages),
        grid_spec=grid_spec,
        out_shape=jax.ShapeDtypeStruct((nb, t, ATT_HEADS * ATT_HEAD_DIM), BF16),
        compiler_params=_cparams(("parallel",)),
        name="dsa_sample_attend",
    )(page_table, q, k_new, v_new, sel, lo, *([cache_k] * (sq * n_pages)), *([cache_v] * (sq * n_pages)))


def kernel(x_prompt, x_sample, state_gdn, state_gdn_conv, cache_k, cache_v, cache_idx_k, page_table,
           meta_tokens, ln1_g, ln1_b, ln2_g, ln2_b, mlp_w1, mlp_w2,
           gdn_w_in, gdn_conv_w, gdn_a_log, gdn_dt_bias, gdn_norm_w, gdn_w_out,
           dsa_w_in, dsa_ik_norm_g, dsa_ik_norm_b, dsa_w_o):
    nb, seq, d = x_prompt.shape
    ns, ts, _ = x_sample.shape
    n_tok = N_META + seq
    lp = -(-n_tok // KEY_TILE) * KEY_TILE
    n_prompt_rows = nb * lp
    rows = n_prompt_rows + ns * ts
    assert rows % ROW_TILE == 0 and lp % GDN_CHUNK == 0 and lp % KEY_CHUNK == 0 and d == D_MODEL
    kvd = ATT_KV_HEADS * ATT_HEAD_DIM

    meta = meta_tokens.astype(x_prompt.dtype)
    pad = jnp.zeros((lp - n_tok, d), x_prompt.dtype)
    h = jnp.concatenate([piece for b in range(nb) for piece in (meta, x_prompt[b], pad)]
                        + [x_sample.reshape(ns * ts, d)], 0)

    def row2(x):
        return x.reshape(1, -1)

    def prompt_rows(x, width, first=0):
        rows_b = [x[b * lp + first:b * lp + n_tok, :width] for b in range(nb)]
        return jnp.concatenate(rows_b, 0).reshape(nb, n_tok - first, width)

    def with_sample_rows(a_prompt, a_sample):
        return lax.dynamic_update_slice(a_prompt, a_sample.reshape(ns * ts, -1), (n_prompt_rows, 0))

    w_in = gdn_w_in[0]
    split = GDN_CONV_DIM + GDN_VALUE_DIM
    n_gate = 2 * GDN_V_HEADS
    w_cat = jnp.concatenate([w_in, jnp.zeros((d, LANES - n_gate), w_in.dtype)], 1).astype(BF16)
    proj, ba = _gdn_inproj(h, w_cat, split)
    decay_lanes = lambda v: jnp.pad(row2(v), ((0, 0), (GDN_V_HEADS, LANES - n_gate)))
    a_p, gdn_state_prompt = _gdn_prompt(proj, ba, gdn_conv_w[0], decay_lanes(gdn_a_log[0]),
                                        decay_lanes(gdn_dt_bias[0]), row2(gdn_norm_w[0]), nb, lp, n_tok)
    proj_s = proj[n_prompt_rows:].reshape(ns, ts, split)
    mixed_s = proj_s[:, :, :GDN_CONV_DIM]
    a_s, gdn_state_sample = _gdn_sample(state_gdn_conv[0], mixed_s, proj_s[:, :, GDN_CONV_DIM:],
                                        ba[n_prompt_rows:, :n_gate].reshape(ns, ts, n_gate), gdn_conv_w[0],
                                        row2(gdn_a_log[0]), row2(gdn_dt_bias[0]), row2(gdn_norm_w[0]),
                                        state_gdn[0])
    h = _outproj_ln(with_sample_rows(a_p, a_s), h, gdn_w_out[0].astype(BF16), row2(ln1_g[0]), row2(ln1_b[0]))
    h = _mlp_ln(h, mlp_w1[0].astype(BF16), mlp_w2[0].astype(BF16), row2(ln2_g[0]), row2(ln2_b[0]))
    keep = GDN_CONV_WIDTH - 1
    gdn_conv_prompt = prompt_rows(proj, GDN_CONV_DIM, first=n_tok - keep)
    gdn_conv_sample = jnp.concatenate([state_gdn_conv[0], mixed_s], 1)[:, -keep:]

    past = page_table.shape[1] * PAGE
    pos = jnp.concatenate([jnp.tile(jnp.arange(lp, dtype=jnp.int32), nb),
                           jnp.tile(past + jnp.arange(ts, dtype=jnp.int32), ns)])
    tables = _rope_tables(pos, ATT_ROT_HALF, LANES) + _rope_tables(pos, IDX_ROT_HALF, IDX_HEAD_DIM)
    n_in = dsa_w_in.shape[2]
    w_dsa = jnp.pad(dsa_w_in[0], ((0, 0), (0, -n_in % LANES))).astype(BF16)
    pad_lanes = lambda v: row2(jnp.pad(v, (0, LANES - v.shape[0])))
    q, k, v, vt, iq, ikw, ik2 = _dsa_proj(h, w_dsa, tables, pad_lanes(dsa_ik_norm_g[0]),
                                          pad_lanes(dsa_ik_norm_b[0]))
    o_p = _dsa_prompt(q, iq, ikw, k.astype(BF16), vt, ik2, nb, lp, min(TOPK_MAX, (n_tok - N_META) // 4))
    smp = lambda x: x[n_prompt_rows:].reshape(ns, ts, -1)
    n_pool = cache_k.shape[1]
    sel, lo = _dsa_sample_select(page_table, smp(iq), smp(ikw), jnp.swapaxes(cache_idx_k[0], 1, 2),
                                 min(TOPK_MAX, (past + ts) // 4))
    rows_kh = lambda c: c[0].reshape(n_pool, PAGE * ATT_KV_HEADS, ATT_HEAD_DIM)
    o_s = _dsa_sample_attend(page_table, smp(q), smp(k), smp(v), sel, lo, rows_kh(cache_k), rows_kh(cache_v))
    h = _outproj_ln(with_sample_rows(o_p, o_s), h, dsa_w_o[0].astype(BF16), row2(ln1_g[1]), row2(ln1_b[1]))
    h = _mlp_ln(h, mlp_w1[1].astype(BF16), mlp_w2[1].astype(BF16), row2(ln2_g[1]), row2(ln2_b[1]))

    heads = lambda x: x.reshape(x.shape[:-1] + (ATT_KV_HEADS, ATT_HEAD_DIM))
    y_prompt = prompt_rows(h, d, first=N_META)
    y_sample = h[n_prompt_rows:].reshape(ns, ts, d)
    return (y_prompt, y_sample,
            gdn_state_prompt[None], gdn_conv_prompt[None], gdn_state_sample[None], gdn_conv_sample[None],
            heads(prompt_rows(k, kvd))[None], heads(prompt_rows(v, kvd))[None],
            prompt_rows(ikw, IDX_HEAD_DIM)[None],
            heads(smp(k))[None], heads(smp(v))[None], smp(ikw)[:, :, :IDX_HEAD_DIM][None])
```

```python
import functools

import jax
import jax.numpy as jnp
from jax import lax
from jax.experimental import pallas as pl
from jax.experimental.pallas import tpu as pltpu

F32 = jnp.float32
BF16 = jnp.bfloat16
HIGHEST = lax.Precision.HIGHEST

D_MODEL = 1024
N_META = 16
DEPTH = 2
LN_EPS = 1e-5
DEEPNORM_ALPHA = (2 * DEPTH) ** 0.25
GDN_K_HEADS = 8
GDN_V_HEADS = 16
GDN_HEAD = 128
GDN_KEY_DIM = GDN_K_HEADS * GDN_HEAD
GDN_VALUE_DIM = GDN_V_HEADS * GDN_HEAD
GDN_CONV_DIM = 2 * GDN_KEY_DIM + GDN_VALUE_DIM
GDN_CONV_WIDTH = 4
GDN_CHUNK = 128
L2_EPS = 1e-6
RMS_EPS = 1e-6
ATT_HEADS = 8
ATT_KV_HEADS = 2
ATT_HEAD_DIM = 128
ATT_GROUP = ATT_HEADS // ATT_KV_HEADS
IDX_HEADS = 8
IDX_HEAD_DIM = 64
TOPK_MAX = 256
ROPE_THETA = 500000.0
ATT_ROT_HALF = ATT_HEAD_DIM // 8
IDX_ROT_HALF = IDX_HEAD_DIM // 8
PAGE = 128

LANES = 128
SUBLANES = 8
ROW_TILE = 512
KEY_TILE = 128
KEY_CHUNK = 3 * KEY_TILE
VMEM_LIMIT = 56 * 1024 * 1024

NEG_INF = float("-inf")


def _cparams(sem):
    return pltpu.CompilerParams(dimension_semantics=sem, vmem_limit_bytes=VMEM_LIMIT)


def _dot(a, b):
    return jnp.dot(a, b, preferred_element_type=F32)


def _dot_nt(a, b, precision=None):
    return lax.dot_general(a, b, (((1,), (1,)), ((), ())), preferred_element_type=F32,
                           precision=precision)


def _dot_tn(a, b):
    return lax.dot_general(a, b, (((0,), (0,)), ((), ())), preferred_element_type=F32)


def _layernorm_rows(x, g, b):
    mu = jnp.mean(x, axis=-1, keepdims=True)
    xc = x - mu
    var = jnp.mean(xc * xc, axis=-1, keepdims=True)
    return xc * lax.rsqrt(var + LN_EPS) * g + b


def _sigmoid(x):
    return 0.5 * jnp.tanh(0.5 * x) + 0.5


def _silu(x):
    return x * _sigmoid(x)


def _softplus(x):
    return jnp.maximum(x, 0.0) + jnp.log(1.0 + jnp.exp(-jnp.abs(x)))


def _resident(shape):
    return pl.BlockSpec(shape, lambda *_: (0,) * len(shape), pipeline_mode=pl.Buffered(1))


def _gdn_inproj_kernel(x_ref, w_ref, proj_ref, ba_ref):
    x = x_ref[...].astype(BF16)
    n_main = proj_ref.shape[1]
    slab = 1024
    for j in range(n_main // slab):
        proj_ref[:, j * slab:(j + 1) * slab] = _dot(x, w_ref[:, j * slab:(j + 1) * slab])
    ba_ref[...] = _dot(x, w_ref[:, n_main:])


def _gdn_inproj(x, w, n_main):
    rows, k = x.shape
    n = w.shape[1]
    return pl.pallas_call(
        _gdn_inproj_kernel,
        grid=(rows // ROW_TILE,),
        in_specs=[pl.BlockSpec((ROW_TILE, k), lambda i: (i, 0)), _resident((k, n))],
        out_specs=[pl.BlockSpec((ROW_TILE, n_main), lambda i: (i, 0)),
                   pl.BlockSpec((ROW_TILE, n - n_main), lambda i: (i, 0))],
        out_shape=[jax.ShapeDtypeStruct((rows, n_main), F32),
                   jax.ShapeDtypeStruct((rows, n - n_main), F32)],
        compiler_params=_cparams(("parallel",)),
        name="gdn_inproj",
    )(x, w)


def _outproj_ln_kernel(a_ref, x_ref, w_ref, g_ref, b_ref, o_ref):
    y = DEEPNORM_ALPHA * x_ref[...] + _dot(a_ref[...], w_ref[...])
    o_ref[...] = _layernorm_rows(y, g_ref[...], b_ref[...])


def _outproj_ln(a, x, w, g, b):
    rows, k = a.shape
    d = x.shape[1]
    return pl.pallas_call(
        _outproj_ln_kernel,
        grid=(rows // ROW_TILE,),
        in_specs=[pl.BlockSpec((ROW_TILE, k), lambda i: (i, 0)),
                  pl.BlockSpec((ROW_TILE, d), lambda i: (i, 0)),
                  _resident((k, d)), _resident((1, d)), _resident((1, d))],
        out_specs=pl.BlockSpec((ROW_TILE, d), lambda i: (i, 0)),
        out_shape=jax.ShapeDtypeStruct((rows, d), F32),
        compiler_params=_cparams(("parallel",)),
        name="outproj_ln",
    )(a, x, w, g, b)


def _mlp_ln_kernel(x_ref, w1_ref, w2_ref, g_ref, b_ref, o_ref):
    x = x_ref[...]
    h = jnp.maximum(_dot(x.astype(BF16), w1_ref[...]), 0.0)
    y = DEEPNORM_ALPHA * x + _dot((h * h).astype(BF16), w2_ref[...])
    o_ref[...] = _layernorm_rows(y, g_ref[...], b_ref[...])


def _mlp_ln(x, w1, w2, g, b):
    rows, d = x.shape
    f = w1.shape[1]
    return pl.pallas_call(
        _mlp_ln_kernel,
        grid=(rows // ROW_TILE,),
        in_specs=[pl.BlockSpec((ROW_TILE, d), lambda i: (i, 0)),
                  _resident((d, f)), _resident((f, d)), _resident((1, d)), _resident((1, d))],
        out_specs=pl.BlockSpec((ROW_TILE, d), lambda i: (i, 0)),
        out_shape=jax.ShapeDtypeStruct((rows, d), F32),
        compiler_params=_cparams(("parallel",)),
        name="mlp_ln",
    )(x, w1, w2, g, b)


def _mlp_ln_sequences(x, w1, w2, g, b, n_seq, stride, first, n_out):
    d = x.shape[1]
    f = w1.shape[1]
    assert n_out % ROW_TILE == 0 and (stride % SUBLANES, first % SUBLANES) == (0, 0)

    def kern(x_ref, w1_ref, w2_ref, g_ref, b_ref, o_ref):
        _mlp_ln_kernel(x_ref, w1_ref, w2_ref, g_ref, b_ref, o_ref.at[0])

    return pl.pallas_call(
        kern,
        grid=(n_seq, n_out // ROW_TILE),
        in_specs=[pl.BlockSpec((pl.Element(ROW_TILE), pl.Element(d)),
                               lambda s, t: (pl.multiple_of(s * stride + first + t * ROW_TILE, SUBLANES), 0)),
                  _resident((d, f)), _resident((f, d)), _resident((1, d)), _resident((1, d))],
        out_specs=pl.BlockSpec((1, ROW_TILE, d), lambda s, t: (s, t, 0)),
        out_shape=jax.ShapeDtypeStruct((n_seq, n_out, d), F32),
        compiler_params=_cparams(("parallel", "parallel")),
        name="mlp_ln_sequences",
    )(x, w1, w2, g, b)


def _unit_lower_inverse(a, n_factors):
    c = a.shape[0]
    row = lax.broadcasted_iota(jnp.int32, (c, c), 0)
    col = lax.broadcasted_iota(jnp.int32, (c, c), 1)
    eye = (row == col).astype(F32)
    p = -a
    t = eye + p
    for _ in range(n_factors - 1):
        pb = p.astype(BF16)
        p = _dot(pb, pb)
        t = t + _dot(t.astype(BF16), p.astype(BF16))
    return t


def _bdot(a, b):
    return lax.dot_general(a, b, (((2,), (1,)), ((0,), (0,))), preferred_element_type=F32)


def _bdot_nt(a, b):
    return lax.dot_general(a, b, (((2,), (2,)), ((0,), (0,))), preferred_element_type=F32)


def _bdot_tn(a, b):
    return lax.dot_general(a, b, (((1,), (1,)), ((0,), (0,))), preferred_element_type=F32)


def _unit_lower_inverse_batched(a):
    c = a.shape[-1]
    row = lax.broadcasted_iota(jnp.int32, (c, c), 0)
    col = lax.broadcasted_iota(jnp.int32, (c, c), 1)
    def coupling(s):
        couples = ((row // (2 * s)) == (col // (2 * s))) & ((row // s) != (col // s))
        return jnp.where(couples, a, 0.0)

    t = (row == col).astype(F32) - coupling(1)
    s = 2
    while s < c:
        tb = t.astype(BF16)
        t = t - _bdot(_bdot(tb, coupling(s).astype(BF16)).astype(BF16), tb)
        s *= 2
    return t


def _pairwise_diff(col):
    c = col.shape[0]
    lane = lax.broadcasted_iota(jnp.int32, (c, LANES), 1)
    left = jnp.where(lane == 0, col, jnp.where(lane == 1, 1.0, 0.0))
    right = jnp.where(lane == 0, 1.0, jnp.where(lane == 1, -col, 0.0))
    return _dot_nt(left, right, precision=HIGHEST)


def _l2norm_rows(x):
    return x * lax.rsqrt(jnp.sum(x * x, axis=-1, keepdims=True) + L2_EPS)


def _gated_rmsnorm(o, z, norm_w):
    on = o * lax.rsqrt(jnp.mean(o * o, axis=-1, keepdims=True) + RMS_EPS) * norm_w
    return on * _silu(z)


def _gdn_gates(ba, a_log, dt_bias):
    beta = _sigmoid(ba[:, :GDN_V_HEADS])
    g = -jnp.exp(a_log) * _softplus(ba[:, GDN_V_HEADS:] + dt_bias)
    return beta, g


def _gdn_prompt_kernel(n_tokens, mixed_ref, z_ref, ba_ref, convw_ref, alog_ref, dtb_ref, normw_ref,
                       o_ref, sfin_ref, xc_ref, s_ref):
    c = pl.program_id(1)
    C = GDN_CHUNK
    W = GDN_CONV_WIDTH

    @pl.when(c == 0)
    def _():
        xc_ref[...] = jnp.zeros_like(xc_ref)
        s_ref[...] = jnp.zeros_like(s_ref)

    x = mixed_ref[...]
    tail = xc_ref[...]
    w = convw_ref[...]
    row8 = lax.broadcasted_iota(jnp.int32, (SUBLANES, 1), 0)
    acc = x * w[W - 1:W, :]
    for s in range(1, W):
        shifted = pltpu.roll(x, s, axis=0)
        head = jnp.where(row8 < s, pltpu.roll(tail, s, axis=0), shifted[:SUBLANES])
        acc = acc + jnp.concatenate([head, shifted[SUBLANES:]], axis=0) * w[W - 1 - s:W - s, :]
    qkv = _silu(acc)
    xc_ref[...] = x[C - SUBLANES:]

    row = lax.broadcasted_iota(jnp.int32, (C, 1), 0)
    valid = (c * C + row) < n_tokens
    ba = ba_ref[...]
    beta = jnp.where(valid, _sigmoid(ba), 0.0)
    g = jnp.where(valid, -jnp.exp(alog_ref[...]) * _softplus(ba + dtb_ref[...]), 0.0)

    ri = lax.broadcasted_iota(jnp.int32, (C, C), 0)
    ci = lax.broadcasted_iota(jnp.int32, (C, C), 1)
    tril = ci <= ri
    strict = ci < ri
    gc = jnp.dot(tril.astype(F32), g, preferred_element_type=F32, precision=HIGHEST)
    gc_rows = gc.T
    egc = jnp.exp(gc)
    glast = gc[C - 1:C, :]
    ekd = jnp.exp(glast - gc)
    eglast = jnp.exp(glast)
    normw = normw_ref[...]
    NH = GDN_V_HEADS

    def gcol(x, h):
        return x[:, NH + h:NH + h + 1]

    qk_raw = [qkv[:, i * GDN_HEAD:(i + 1) * GDN_HEAD] for i in range(2 * GDN_K_HEADS)]
    sq = jnp.concatenate([t * t for t in qk_raw], axis=0)
    sq_hi = sq.astype(BF16)
    sq_lo = (sq - sq_hi.astype(F32)).astype(BF16)
    ones = jnp.ones((GDN_HEAD, GDN_HEAD), BF16)
    inv_norm = lax.rsqrt(_dot(sq_hi, ones) + _dot(sq_lo, ones) + L2_EPS)
    qk_n = [t * inv_norm[i * C:(i + 1) * C] for i, t in enumerate(qk_raw)]
    q_l = [t * (GDN_HEAD ** -0.5) for t in qk_n[:GDN_K_HEADS]]
    k_l = qk_n[GDN_K_HEADS:]
    k8 = jnp.stack(k_l).astype(BF16)
    kk8 = _bdot_nt(k8, k8)
    qk8 = _bdot_nt(jnp.stack(q_l).astype(BF16), k8)
    a_l, qkd_l, rhs_l, qg_l, kd_l = [], [], [], [], []
    for h in range(NH):
        kh = h // 2
        v = qkv[:, 2 * GDN_KEY_DIM + h * GDN_HEAD:2 * GDN_KEY_DIM + (h + 1) * GDN_HEAD]
        bcol = beta[:, h:h + 1]
        diff = gcol(gc, h) - gc_rows[NH + h:NH + h + 1, :]
        decay = jnp.where(tril, jnp.exp(jnp.where(tril, diff, 0.0)), 0.0)
        a_l.append(jnp.where(strict, kk8[kh] * decay * bcol, 0.0))
        qkd_l.append((qk8[kh] * decay).astype(BF16))
        rhs_l.append(jnp.concatenate([v * bcol, k_l[kh] * (bcol * gcol(egc, h))], axis=1).astype(BF16))
        qg_l.append(q_l[kh] * gcol(egc, h))
        kd_l.append((k_l[kh] * gcol(ekd, h)).astype(BF16))
    t_inv = _unit_lower_inverse_batched(jnp.stack(a_l)).astype(BF16)
    sol = _bdot(t_inv, jnp.stack(rhs_l))
    u, wm = sol[:, :, :GDN_HEAD], sol[:, :, GDN_HEAD:]
    s_old = s_ref[...]
    lhs = jnp.concatenate([wm, jnp.stack(qg_l)], axis=1).astype(BF16)
    ws_qs = _bdot(lhs, s_old.astype(BF16))
    v_new = u - ws_qs[:, :C]
    vnb = v_new.astype(BF16)
    o = ws_qs[:, C:] + _bdot(jnp.stack(qkd_l), vnb)
    upd = _bdot_tn(jnp.stack(kd_l), vnb)
    for h in range(NH):
        s_ref[h] = s_old[h] * gcol(eglast, h) + upd[h]
        zh = z_ref[:, h * GDN_HEAD:(h + 1) * GDN_HEAD]
        o_ref[:, h * GDN_HEAD:(h + 1) * GDN_HEAD] = _gated_rmsnorm(o[h], zh, normw).astype(o_ref.dtype)

    @pl.when(c == pl.num_programs(1) - 1)
    def _():
        sfin_ref[0] = s_ref[...]


def _gdn_prompt(proj, ba, conv_w, a_log, dt_bias, norm_w, n_batch, lp, n_tokens):
    C = GDN_CHUNK
    per_b = lp // C
    n_chunks = per_b
    z_off = GDN_CONV_DIM // GDN_VALUE_DIM
    kern = functools.partial(_gdn_prompt_kernel, n_tokens)
    return pl.pallas_call(
        kern,
        grid=(n_batch, n_chunks),
        in_specs=[pl.BlockSpec((C, GDN_CONV_DIM), lambda b, c: (b * per_b + c, 0)),
                  pl.BlockSpec((C, GDN_VALUE_DIM), lambda b, c: (b * per_b + c, z_off)),
                  pl.BlockSpec((C, LANES), lambda b, c: (b * per_b + c, 0)),
                  pl.BlockSpec((GDN_CONV_WIDTH, GDN_CONV_DIM), lambda b, c: (0, 0)),
                  pl.BlockSpec((1, LANES), lambda b, c: (0, 0)),
                  pl.BlockSpec((1, LANES), lambda b, c: (0, 0)),
                  pl.BlockSpec((1, GDN_HEAD), lambda b, c: (0, 0))],
        out_specs=[pl.BlockSpec((C, GDN_VALUE_DIM), lambda b, c: (b * per_b + c, 0)),
                   pl.BlockSpec((1, GDN_V_HEADS, GDN_HEAD, GDN_HEAD), lambda b, c: (b, 0, 0, 0))],
        out_shape=[jax.ShapeDtypeStruct((proj.shape[0], GDN_VALUE_DIM), BF16),
                   jax.ShapeDtypeStruct((n_batch, GDN_V_HEADS, GDN_HEAD, GDN_HEAD), F32)],
        scratch_shapes=[pltpu.VMEM((SUBLANES, GDN_CONV_DIM), F32),
                        pltpu.VMEM((GDN_V_HEADS, GDN_HEAD, GDN_HEAD), F32)],
        compiler_params=_cparams(("parallel", "arbitrary")),
        name="gdn_prompt",
    )(proj, proj, ba, conv_w, a_log, dt_bias, norm_w)


def _twice(dst_ref, x):
    t = x.shape[0]
    dst_ref[0:t, :] = x
    dst_ref[t:2 * t, :] = x
    return dst_ref[...]


SAMPLE_SEQS_PER_STEP = 2


def _gdn_sample_kernel(cs_ref, mixed_ref, z_ref, ba_ref, convw_ref, alog_ref, dtb_ref, normw_ref, s_ref,
                       o_ref, snew_ref, xc_ref, dq_ref, dz_ref, dba_ref):
    for i in range(mixed_ref.shape[0]):
        one = pl.ds(i, 1)
        _gdn_sample_seq(cs_ref.at[one], mixed_ref.at[one], z_ref.at[one], ba_ref.at[one], convw_ref, alog_ref,
                        dtb_ref, normw_ref, s_ref.at[one], o_ref.at[one], snew_ref.at[one],
                        xc_ref.at[i], dq_ref.at[i], dz_ref.at[i], dba_ref.at[i])


def _gdn_sample_seq(cs_ref, mixed_ref, z_ref, ba_ref, convw_ref, alog_ref, dtb_ref, normw_ref, s_ref,
                    o_ref, snew_ref, xc_ref, dq_ref, dz_ref, dba_ref):
    T = mixed_ref.shape[1]
    R = 2 * T
    W = GDN_CONV_WIDTH
    xc_ref[0:W - 1, :] = cs_ref[0]
    xc_ref[W - 1:W - 1 + T, :] = mixed_ref[0]
    w = convw_ref[...]
    acc = xc_ref[0:T, :] * w[0:1, :]
    for j in range(1, W):
        acc = acc + xc_ref[j:j + T, :] * w[j:j + 1, :]
    qkv = _twice(dq_ref, _silu(acc))
    z2 = _twice(dz_ref, z_ref[0])
    ba2 = _twice(dba_ref, ba_ref[0])
    beta, g = _gdn_gates(ba2, alog_ref[...], dtb_ref[...])

    rr = lax.broadcasted_iota(jnp.int32, (R, 1), 0)
    tmod = rr % T
    first = rr < T
    gc = jnp.zeros_like(g)
    for s in range(T):
        gc = gc + jnp.where(tmod >= s, g[s:s + 1, :], 0.0)
    glast = gc[T - 1:T, :]
    normw = normw_ref[...]

    NP = GDN_K_HEADS
    M = NP * R

    def pair_col(x, kh):
        return jnp.where(first, x[:, 2 * kh:2 * kh + 1], x[:, 2 * kh + 1:2 * kh + 2])

    def pair_tile(x, base, kh):
        a = x[:, base + (2 * kh) * GDN_HEAD:base + (2 * kh + 1) * GDN_HEAD]
        b = x[:, base + (2 * kh + 1) * GDN_HEAD:base + (2 * kh + 2) * GDN_HEAD]
        return jnp.where(first, a, b)

    q_t, k_t, v_t, z_t, b_c, gc_c, gl_c = [], [], [], [], [], [], []
    for kh in range(NP):
        q_t.append(_l2norm_rows(qkv[:, kh * GDN_HEAD:(kh + 1) * GDN_HEAD]) * (GDN_HEAD ** -0.5))
        k_t.append(_l2norm_rows(qkv[:, GDN_KEY_DIM + kh * GDN_HEAD:GDN_KEY_DIM + (kh + 1) * GDN_HEAD]))
        v_t.append(pair_tile(qkv, 2 * GDN_KEY_DIM, kh))
        z_t.append(pair_tile(z2, 0, kh))
        b_c.append(pair_col(beta, kh))
        gc_c.append(pair_col(gc, kh))
        gl_c.append(pair_col(jnp.broadcast_to(glast, (R, GDN_V_HEADS)), kh))
    q64 = jnp.concatenate(q_t, axis=0)
    k64 = jnp.concatenate(k_t, axis=0)
    v64 = jnp.concatenate(v_t, axis=0)
    bcol = jnp.concatenate(b_c, axis=0)
    gcol = jnp.concatenate(gc_c, axis=0)
    glcol = jnp.concatenate(gl_c, axis=0)

    ri = lax.broadcasted_iota(jnp.int32, (M, M), 0)
    ci = lax.broadcasted_iota(jnp.int32, (M, M), 1)
    same = (ri // T) == (ci // T)
    tril = same & ((ci % T) <= (ri % T))
    strict = same & ((ci % T) < (ri % T))
    kb = k64.astype(BF16)
    kk = _dot_nt(kb, kb)
    qk0 = _dot_nt(q64.astype(BF16), kb)
    diff = _pairwise_diff(gcol)
    decay = jnp.where(tril, jnp.exp(jnp.where(tril, diff, 0.0)), 0.0)
    a_mat = jnp.where(strict, kk * decay * bcol, 0.0)
    t_inv = _unit_lower_inverse(a_mat, max(1, (T - 1).bit_length())).astype(BF16)
    egc = jnp.exp(gcol)
    u = _dot(t_inv, (v64 * bcol).astype(BF16))
    wm = _dot(t_inv, (k64 * (bcol * egc)).astype(BF16))
    qg = q64 * egc
    kd = k64 * jnp.exp(glcol - gcol)
    qkd = (qk0 * decay).astype(BF16)

    v_new_t, qs_t = [], []
    for kh in range(NP):
        lhs = jnp.concatenate([wm[kh * R:(kh + 1) * R], qg[kh * R:(kh + 1) * R]], axis=0).astype(BF16)
        r0 = _dot(lhs, s_ref[0, 2 * kh].astype(BF16))
        r1 = _dot(lhs, s_ref[0, 2 * kh + 1].astype(BF16))
        ws = jnp.where(first, r0[:R], r1[:R])
        qs_t.append(jnp.where(first, r0[R:], r1[R:]))
        v_new_t.append(u[kh * R:(kh + 1) * R] - ws)
    v_new = jnp.concatenate(v_new_t, axis=0)
    o64 = jnp.concatenate(qs_t, axis=0) + _dot(qkd, v_new.astype(BF16))
    on = _gated_rmsnorm(o64, jnp.concatenate(z_t, axis=0), normw)

    for kh in range(NP):
        vn = v_new_t[kh].astype(BF16)
        kdp = kd[kh * R:(kh + 1) * R]
        for j in range(2):
            h = 2 * kh + j
            keep = first if j == 0 else jnp.logical_not(first)
            kdm = jnp.where(keep, kdp, 0.0).astype(BF16)
            eg = jnp.exp(glast[:, h:h + 1])
            snew_ref[0, h] = s_ref[0, h] * eg + _dot_tn(kdm, vn)
            tile = on[kh * R:(kh + 1) * R]
            if j == 1:
                tile = pltpu.roll(tile, T, axis=0)
            o_ref[0, :, h * GDN_HEAD:(h + 1) * GDN_HEAD] = tile[:T].astype(o_ref.dtype)


def _gdn_sample(conv_state, mixed, z, ba, conv_w, a_log, dt_bias, norm_w, state):
    nb, t, _ = mixed.shape
    sq = SAMPLE_SEQS_PER_STEP
    assert nb % sq == 0
    seq = lambda *tail: pl.BlockSpec((sq,) + tail, lambda b: (b,) + (0,) * len(tail))
    return pl.pallas_call(
        _gdn_sample_kernel,
        grid=(nb // sq,),
        in_specs=[seq(GDN_CONV_WIDTH - 1, GDN_CONV_DIM), seq(t, GDN_CONV_DIM), seq(t, GDN_VALUE_DIM),
                  seq(t, 2 * GDN_V_HEADS),
                  _resident((GDN_CONV_WIDTH, GDN_CONV_DIM)), _resident((1, GDN_V_HEADS)),
                  _resident((1, GDN_V_HEADS)), _resident((1, GDN_HEAD)),
                  seq(GDN_V_HEADS, GDN_HEAD, GDN_HEAD)],
        out_specs=[seq(t, GDN_VALUE_DIM), seq(GDN_V_HEADS, GDN_HEAD, GDN_HEAD)],
        out_shape=[jax.ShapeDtypeStruct((nb, t, GDN_VALUE_DIM), BF16),
                   jax.ShapeDtypeStruct(state.shape, F32)],
        scratch_shapes=[pltpu.VMEM((sq, GDN_CONV_WIDTH - 1 + t, GDN_CONV_DIM), F32),
                        pltpu.VMEM((sq, 2 * t, GDN_CONV_DIM), F32),
                        pltpu.VMEM((sq, 2 * t, GDN_VALUE_DIM), F32),
                        pltpu.VMEM((sq, 2 * t, 2 * GDN_V_HEADS), F32)],
        compiler_params=_cparams(("parallel",)),
        name="gdn_sample",
    )(conv_state, mixed, z, ba, conv_w, a_log, dt_bias, norm_w, state)


def _rope_tables(pos, half, period):
    inv_freq = ROPE_THETA ** (-jnp.arange(half, dtype=F32) * 2.0 / (2 * half))
    ang = pos.astype(F32)[:, None] * inv_freq[None, :]
    cos, sin = jnp.cos(ang), jnp.sin(ang)
    ones = jnp.ones((pos.shape[0], period - 2 * half), F32)
    cos_p = jnp.concatenate([cos, cos, ones], axis=1)
    sin_p = jnp.concatenate([-sin, sin, 0.0 * ones], axis=1)
    reps = LANES // period
    return jnp.tile(cos_p, (1, reps)), jnp.tile(sin_p, (1, reps))


def _rope_tile(x, cos, sin, half, period):
    lane = lax.broadcasted_iota(jnp.int32, x.shape, 1) % period
    partner = jnp.where(lane < half, pltpu.roll(x, LANES - half, axis=1), pltpu.roll(x, half, axis=1))
    return x * cos + partner * sin


def _dsa_proj_kernel(x_ref, w_ref, ca_ref, sa_ref, ci_ref, si_ref, g_ref, b_ref,
                     q_ref, k_ref, v_ref, vt_ref, iq_ref, ikw_ref, ik2_ref):
    x = x_ref[...].astype(BF16)
    ca, sa, ci, si = ca_ref[...], sa_ref[...], ci_ref[...], si_ref[...]
    q_off, k_off = 0, ATT_HEADS * ATT_HEAD_DIM
    v_off = k_off + ATT_KV_HEADS * ATT_HEAD_DIM
    iq_off = v_off + ATT_KV_HEADS * ATT_HEAD_DIM
    ik_off = iq_off + IDX_HEADS * IDX_HEAD_DIM

    def proj(off):
        return _dot(x, w_ref[:, off:off + LANES])

    for h in range(ATT_HEADS):
        t = _rope_tile(proj(q_off + h * LANES), ca, sa, ATT_ROT_HALF, LANES)
        q_ref[:, h * LANES:(h + 1) * LANES] = (t * (ATT_HEAD_DIM ** -0.5)).astype(q_ref.dtype)
    for h in range(ATT_KV_HEADS):
        k_ref[:, h * LANES:(h + 1) * LANES] = _rope_tile(proj(k_off + h * LANES), ca, sa, ATT_ROT_HALF, LANES)
        v = proj(v_off + h * LANES)
        v_ref[:, h * LANES:(h + 1) * LANES] = v
        vt_ref[h * LANES:(h + 1) * LANES, :] = v.T.astype(vt_ref.dtype)
    for h in range(IDX_HEADS * IDX_HEAD_DIM // LANES):
        t = _rope_tile(proj(iq_off + h * LANES), ci, si, IDX_ROT_HALF, IDX_HEAD_DIM)
        iq_ref[:, h * LANES:(h + 1) * LANES] = t.astype(iq_ref.dtype)
    t = proj(ik_off)
    lane = lax.broadcasted_iota(jnp.int32, t.shape, 1)
    is_ik = lane < IDX_HEAD_DIM
    mu = jnp.sum(jnp.where(is_ik, t, 0.0), axis=-1, keepdims=True) / IDX_HEAD_DIM
    tc = jnp.where(is_ik, t - mu, 0.0)
    var = jnp.sum(tc * tc, axis=-1, keepdims=True) / IDX_HEAD_DIM
    ik = _rope_tile(tc * lax.rsqrt(var + LN_EPS) * g_ref[...] + b_ref[...], ci, si, IDX_ROT_HALF, IDX_HEAD_DIM)
    ikw = jnp.where(is_ik, ik, t * (IDX_HEADS ** -0.5))
    ikw_ref[...] = ikw
    ik_only = jnp.where(is_ik, ik, 0.0)
    ik2_ref[...] = (ik_only + pltpu.roll(ik_only, IDX_HEAD_DIM, axis=1)).astype(ik2_ref.dtype)


def _dsa_proj(x, w, tables, ik_g, ik_b):
    rows, d = x.shape
    n = w.shape[1]
    kv = ATT_KV_HEADS * ATT_HEAD_DIM
    row_spec = lambda width: pl.BlockSpec((ROW_TILE, width), lambda i: (i, 0))
    const_spec = _resident
    return pl.pallas_call(
        _dsa_proj_kernel,
        grid=(rows // ROW_TILE,),
        in_specs=[row_spec(d), const_spec((d, n))] + [row_spec(LANES)] * 4 + [const_spec((1, LANES))] * 2,
        out_specs=[row_spec(ATT_HEADS * ATT_HEAD_DIM), row_spec(kv), row_spec(kv),
                   pl.BlockSpec((kv, ROW_TILE), lambda i: (0, i)),
                   row_spec(IDX_HEADS * IDX_HEAD_DIM), row_spec(LANES), row_spec(LANES)],
        out_shape=[jax.ShapeDtypeStruct((rows, ATT_HEADS * ATT_HEAD_DIM), BF16),
                   jax.ShapeDtypeStruct((rows, kv), F32),
                   jax.ShapeDtypeStruct((rows, kv), F32),
                   jax.ShapeDtypeStruct((kv, rows), BF16),
                   jax.ShapeDtypeStruct((rows, IDX_HEADS * IDX_HEAD_DIM), BF16),
                   jax.ShapeDtypeStruct((rows, LANES), F32),
                   jax.ShapeDtypeStruct((rows, LANES), BF16)],
        compiler_params=_cparams(("parallel",)),
        name="dsa_proj",
    )(x, w, *tables, ik_g, ik_b)


def _float_order_key(x):
    b = pltpu.bitcast(x, jnp.int32)
    return b ^ ((b >> 31) & jnp.int32(0x7FFFFFFF))


def _float_from_key(k):
    return pltpu.bitcast(k ^ ((k >> 31) & jnp.int32(0x7FFFFFFF)), F32)


BISECT_UNROLL = 8
VALUE_ROUNDS = 2
KEY_ROUNDS = 32 // BISECT_UNROLL
F32_MAX = 3.4028234663852886e38


def _kth_largest_bounds(count_ge, target, amax):
    bound = jnp.minimum(2.0 * amax, F32_MAX)
    lo = _float_order_key(-bound)
    hi = _float_order_key(bound) + 1
    c_lo = jnp.full(target.shape, -1.0, F32)

    def step(carry, split_values):
        lo, hi, c_lo = carry
        mid = (lo >> 1) + (hi >> 1) + (lo & hi & 1)
        if split_values:
            vmid = _float_order_key(_float_from_key(lo) * 0.5 + _float_from_key(hi) * 0.5)
            mid = jnp.where((vmid > lo) & (vmid < hi), vmid, mid)
        cnt = count_ge(_float_from_key(mid))
        ok = cnt >= target
        return jnp.where(ok, mid, lo), jnp.where(ok, hi, mid), jnp.where(ok, cnt, c_lo)

    def rounds(carry, n_rounds, split_values):
        def cond(state):
            it, (lo, hi, c_lo) = state
            pending = (c_lo != target) & (hi != lo + 1)
            return (it < n_rounds) & (jnp.max(jnp.where(pending, 1.0, 0.0)) > 0.0)

        def body(state):
            it, carry = state
            for _ in range(BISECT_UNROLL):
                carry = step(carry, split_values)
            return it + 1, carry

        return lax.while_loop(cond, body, (jnp.int32(0), carry))[1]

    carry = rounds((lo, hi, c_lo), VALUE_ROUNDS, True)
    lo, hi, _ = rounds(carry, KEY_ROUNDS, False)
    return _float_from_key(lo), _float_from_key(hi)


def _dsa_prompt_kernel(topk, q_ref, iq_ref, ikw_ref, k_ref, vt_ref, ik2_ref, o_ref,
                       s_ref, xh_ref, qg_ref, sc_ref, acc_ref):
    i = pl.program_id(1)
    KT = KEY_TILE
    CH = KEY_CHUNK
    nc = i // (CH // KT) + 1
    row = lax.broadcasted_iota(jnp.int32, (CH, KT), 0)
    lane = lax.broadcasted_iota(jnp.int32, (KT, KT), 1)
    qpos = i * KT + lax.broadcasted_iota(jnp.int32, (1, KT), 1)
    GQ = ATT_GROUP * KT

    def chunk(c):
        return pl.ds(pl.multiple_of(c * CH, CH), CH)

    NACC = 8

    def fold_rows(x):
        return x.reshape(CH // (NACC * SUBLANES), NACC, SUBLANES, x.shape[-1])

    def unfold(x, op):
        return op(op(x, axis=0), axis=0, keepdims=True)

    for h in range(IDX_HEADS):
        tile = iq_ref[:, (h // 2) * LANES:(h // 2 + 1) * LANES]
        mine = (lane // IDX_HEAD_DIM) == (h % 2)
        xh_ref[h * KT:(h + 1) * KT, :] = jnp.where(mine, tile, jnp.zeros_like(tile))
    for g in range(ATT_KV_HEADS):
        for hq in range(ATT_GROUP):
            h = g * ATT_GROUP + hq
            qg_ref[g, hq * KT:(hq + 1) * KT, :] = q_ref[:, h * LANES:(h + 1) * LANES]
    w_rows = ikw_ref[...].T[IDX_HEAD_DIM:IDX_HEAD_DIM + IDX_HEADS, :]

    n_pairs = (nc + 1) // 2

    def pair(j):
        return 2 * j, jnp.minimum(2 * j + 1, nc - 1)

    def score_chunk(c, amax):
        d = _dot_nt(ik2_ref[chunk(c), :], xh_ref[...])
        acc = w_rows[0:1, :] * jnp.maximum(d[:, 0:KT], 0.0)
        for h in range(1, IDX_HEADS):
            acc = acc + w_rows[h:h + 1, :] * jnp.maximum(d[:, h * KT:(h + 1) * KT], 0.0)
        acc = acc * (IDX_HEAD_DIM ** -0.5)
        kpos = c * CH + row
        causal = kpos <= qpos
        meta = kpos < N_META
        s_ref[chunk(c), :] = jnp.where(causal, jnp.where(meta, jnp.inf, acc), NEG_INF)
        finite = jnp.where(causal & jnp.logical_not(meta), jnp.abs(acc), 0.0)
        return jnp.maximum(amax, jnp.max(fold_rows(finite), axis=0))

    def score_pair(j, amax):
        c0, c1 = pair(j)
        return score_chunk(c1, score_chunk(c0, amax))

    amax = unfold(lax.fori_loop(0, n_pairs, score_pair, jnp.zeros((NACC, SUBLANES, KT), F32)), jnp.max)

    target = jnp.minimum(qpos + 1, topk).astype(F32)

    def count_ge(thr):
        def body(c, acc):
            hit = jnp.where(s_ref[chunk(c), :] >= thr, 1.0, 0.0)
            return acc + jnp.sum(fold_rows(hit), axis=0)
        return unfold(lax.fori_loop(0, nc, body, jnp.zeros((NACC, SUBLANES, KT), F32)), jnp.sum)

    lo, hi = _kth_largest_bounds(count_ge, target, amax)
    surplus = count_ge(lo) - target

    @pl.when(jnp.max(surplus) > 0.0)
    def _():
        need = target - count_ge(hi)
        rr = lax.broadcasted_iota(jnp.int32, (CH, CH), 0)
        cc = lax.broadcasted_iota(jnp.int32, (CH, CH), 1)
        tri = (cc <= rr).astype(BF16)

        def drop(c, carry):
            t = s_ref[chunk(c), :]
            tie = (t >= lo) & jnp.logical_not(t >= hi)
            tie_f = jnp.where(tie, 1.0, 0.0)
            rank = carry + _dot(tri, tie_f.astype(BF16))
            s_ref[chunk(c), :] = jnp.where(tie & (rank > need), NEG_INF, t)
            return carry + jnp.sum(tie_f, axis=0, keepdims=True)

        lax.fori_loop(0, nc, drop, jnp.zeros((1, KT), F32))

    def bias_cols(c):
        b = jnp.where(s_ref[chunk(c), :] >= lo, 0.0, NEG_INF)
        return jnp.concatenate([b] * ATT_GROUP, axis=1)

    G = ATT_KV_HEADS

    def max_chunk(c, macc):
        bias = bias_cols(c)
        tops = []
        for g in range(G):
            sc = _dot_nt(k_ref[chunk(c), g * LANES:(g + 1) * LANES], qg_ref[g]) + bias
            sc_ref[chunk(c), g * GQ:(g + 1) * GQ] = sc
            tops.append(jnp.max(fold_rows(sc), axis=0))
        return jnp.maximum(macc, jnp.concatenate(tops, axis=-1))

    def max_pair(j, macc):
        c0, c1 = pair(j)
        return max_chunk(c1, max_chunk(c0, macc))

    macc = lax.fori_loop(0, n_pairs, max_pair, jnp.full((NACC, SUBLANES, G * GQ), NEG_INF, F32))
    m = unfold(macc, jnp.max)

    ones_rows = jnp.ones((2 * SUBLANES, CH), BF16)
    acc_ref[...] = jnp.zeros_like(acc_ref)

    def sum_chunk(c, shift):
        for g in range(G):
            p = jnp.exp(sc_ref[chunk(c), g * GQ:(g + 1) * GQ] - shift[:, g * GQ:(g + 1) * GQ]).astype(BF16)
            vt = jnp.concatenate([vt_ref[g * LANES:(g + 1) * LANES, chunk(c)], ones_rows], axis=0)
            acc_ref[g] += _dot(vt, p)

    def sum_pair(j, _):
        c0, c1 = pair(j)
        sum_chunk(c0, m)
        sum_chunk(c1, jnp.where(2 * j + 1 < nc, m, jnp.inf))
        return 0

    lax.fori_loop(0, n_pairs, sum_pair, 0)
    for g in range(G):
        acc = acc_ref[g]
        out_t = acc[:ATT_HEAD_DIM] / acc[ATT_HEAD_DIM:ATT_HEAD_DIM + 1]
        for hq in range(ATT_GROUP):
            h = g * ATT_GROUP + hq
            o_ref[:, h * LANES:(h + 1) * LANES] = out_t[:, hq * KT:(hq + 1) * KT].T.astype(o_ref.dtype)


def _dsa_prompt(q, iq, ikw, k, vt, ik2, n_batch, lp, topk):
    KT = KEY_TILE
    nq = lp // KT
    kv = ATT_KV_HEADS * ATT_HEAD_DIM
    qspec = lambda width: pl.BlockSpec((KT, width), lambda b, i: (b * nq + i, 0))
    return pl.pallas_call(
        functools.partial(_dsa_prompt_kernel, topk),
        grid=(n_batch, nq),
        in_specs=[qspec(ATT_HEADS * ATT_HEAD_DIM), qspec(IDX_HEADS * IDX_HEAD_DIM), qspec(LANES),
                  pl.BlockSpec((lp, kv), lambda b, i: (b, 0)),
                  pl.BlockSpec((kv, lp), lambda b, i: (0, b)),
                  pl.BlockSpec((lp, LANES), lambda b, i: (b, 0))],
        out_specs=qspec(ATT_HEADS * ATT_HEAD_DIM),
        out_shape=jax.ShapeDtypeStruct(q.shape, BF16),
        scratch_shapes=[pltpu.VMEM((lp, KT), F32),
                        pltpu.VMEM((IDX_HEADS * KT, LANES), BF16),
                        pltpu.VMEM((ATT_KV_HEADS, ATT_GROUP * KT, ATT_HEAD_DIM), BF16),
                        pltpu.VMEM((lp, ATT_HEADS * KT), F32),
                        pltpu.VMEM((ATT_KV_HEADS, ATT_HEAD_DIM + 2 * SUBLANES, ATT_GROUP * KT), F32)],
        compiler_params=_cparams(("parallel", "arbitrary")),
        name="dsa_prompt",
    )(q, iq, ikw, k, vt, ik2)


SAMPLE_GROUP = 16


def _dsa_sample_select_kernel(topk, n_pages, pt_ref, iq_ref, ikw_ref, *rest):
    del pt_ref
    page_refs = rest[:n_pages]
    sel_ref, lo_ref, stack_ref, st_ref, iq8_ref, ikw8_ref, newk_ref = rest[n_pages:]
    b = pl.program_id(0)
    j = b % SAMPLE_GROUP
    T = iq_ref.shape[1]
    R = 2 * T
    past = n_pages * PAGE
    n_keys = stack_ref.shape[1]
    GR = SAMPLE_GROUP * R

    iq8_ref[0:T, :] = iq_ref[0].astype(F32)
    iq8_ref[T:R, :] = iq_ref[0].astype(F32)
    ikw8_ref[0:T, :] = ikw_ref[0]
    ikw8_ref[T:R, :] = ikw_ref[0]
    lane = lax.broadcasted_iota(jnp.int32, (R, LANES), 1)
    ikw8 = ikw8_ref[...]
    xh = []
    for h in range(IDX_HEADS):
        tile = iq8_ref[:, (h // 2) * LANES:(h // 2 + 1) * LANES]
        if h % 2:
            tile = pltpu.roll(tile, IDX_HEAD_DIM, axis=1)
        xh.append(jnp.where(lane < IDX_HEAD_DIM, tile, 0.0))
    x_all = jnp.concatenate(xh, axis=0).astype(BF16)

    def index_scores(d):
        acc = jnp.zeros((R, LANES), F32)
        for h in range(IDX_HEADS):
            wcol = ikw8[:, IDX_HEAD_DIM + h:IDX_HEAD_DIM + h + 1]
            acc = acc + wcol * jnp.maximum(d[h * R:(h + 1) * R], 0.0)
        return acc * (IDX_HEAD_DIM ** -0.5)

    rows = pl.ds(pl.multiple_of(j * R, R), R)
    zeros64 = jnp.zeros((LANES - IDX_HEAD_DIM, PAGE), BF16)
    for p in range(n_pages):
        keys_t = jnp.concatenate([page_refs[p][0].astype(BF16), zeros64], axis=0)
        sc = index_scores(_dot(x_all, keys_t))
        if p == 0:
            sc = jnp.where(lane < N_META, jnp.inf, sc)
        stack_ref[rows, p * PAGE:(p + 1) * PAGE] = sc
    tmod = lax.broadcasted_iota(jnp.int32, (R, 1), 0) % T
    newk_ref[...] = jnp.zeros_like(newk_ref)
    newk_ref[0:T, :] = ikw_ref[0]
    nk = newk_ref[...]
    lane_k = lax.broadcasted_iota(jnp.int32, nk.shape, 1)
    sc_new = index_scores(_dot_nt(x_all, jnp.where(lane_k < IDX_HEAD_DIM, nk, 0.0).astype(BF16)))
    stack_ref[rows, past:n_keys] = jnp.where(lane <= tmod, sc_new, NEG_INF)

    @pl.when(j == SAMPLE_GROUP - 1)
    def _():
        st_ref[...] = stack_ref[...].T
        qlane = lax.broadcasted_iota(jnp.int32, (1, GR), 1)
        qpos = past + (qlane % R) % T
        target = jnp.minimum(qpos + 1, topk).astype(F32)
        NACC = 8

        def count_ge(thr):
            hit = jnp.where(st_ref[...] >= thr, 1.0, 0.0)
            part = jnp.sum(hit.reshape(n_keys // (NACC * SUBLANES), NACC, SUBLANES, GR), axis=0)
            return jnp.sum(jnp.sum(part, axis=0), axis=0, keepdims=True)

        mag = jnp.abs(st_ref[...])
        mag = jnp.where(mag < jnp.inf, mag, 0.0)
        amax = jnp.max(jnp.max(mag.reshape(n_keys // SUBLANES, SUBLANES, GR), axis=0), axis=0, keepdims=True)
        lo, hi = _kth_largest_bounds(count_ge, target, amax)
        surplus = count_ge(lo) - target

        @pl.when(jnp.max(surplus) > 0.0)
        def _():
            need = target - count_ge(hi)
            rr = lax.broadcasted_iota(jnp.int32, (LANES, LANES), 0)
            cc = lax.broadcasted_iota(jnp.int32, (LANES, LANES), 1)
            tri = (cc <= rr).astype(BF16)
            carry = jnp.zeros((1, GR), F32)
            for kt in range(n_keys // LANES):
                t = st_ref[kt * LANES:(kt + 1) * LANES, :]
                tie = (t >= lo) & jnp.logical_not(t >= hi)
                tie_f = jnp.where(tie, 1.0, 0.0)
                rank = carry + _dot(tri, tie_f.astype(BF16))
                st_ref[kt * LANES:(kt + 1) * LANES, :] = jnp.where(tie & (rank > need), NEG_INF, t)
                carry = carry + jnp.sum(tie_f, axis=0, keepdims=True)

        sel_ref[...] = st_ref[...].T
        lo_ref[...] = jnp.broadcast_to(lo, (GR, GR)).T


def _dsa_sample_select(page_table, iq, ikw, cache_ik, topk):
    nb, t, _ = iq.shape
    n_pages = page_table.shape[1]
    n_keys = n_pages * PAGE + LANES
    gr = SAMPLE_GROUP * 2 * t
    assert nb % SAMPLE_GROUP == 0 and gr == LANES
    tok = lambda width: pl.BlockSpec((1, t, width), lambda b, pt: (b, 0, 0))
    page = lambda p: pl.BlockSpec((1, IDX_HEAD_DIM, PAGE), lambda b, pt: (pt[b, p], 0, 0))
    grid_spec = pltpu.PrefetchScalarGridSpec(
        num_scalar_prefetch=1,
        grid=(nb,),
        in_specs=[tok(IDX_HEADS * IDX_HEAD_DIM), tok(LANES)] + [page(p) for p in range(n_pages)],
        out_specs=[pl.BlockSpec((gr, n_keys), lambda b, pt: (b // SAMPLE_GROUP, 0)),
                   pl.BlockSpec((gr, LANES), lambda b, pt: (b // SAMPLE_GROUP, 0))],
        scratch_shapes=[pltpu.VMEM((gr, n_keys), F32),
                        pltpu.VMEM((n_keys, gr), F32),
                        pltpu.VMEM((2 * t, IDX_HEADS * IDX_HEAD_DIM), F32),
                        pltpu.VMEM((2 * t, LANES), F32),
                        pltpu.VMEM((LANES, LANES), F32)])
    return pl.pallas_call(
        functools.partial(_dsa_sample_select_kernel, topk, n_pages),
        grid_spec=grid_spec,
        out_shape=[jax.ShapeDtypeStruct((nb * 2 * t, n_keys), F32),
                   jax.ShapeDtypeStruct((nb * 2 * t, LANES), F32)],
        compiler_params=_cparams(("arbitrary",)),
        name="dsa_sample_select",
    )(page_table, iq, ikw, *([cache_ik] * n_pages))


def _dsa_sample_attend_kernel(n_pages, pt_ref, q_ref, kn_ref, vn_ref, sel_ref, lo_ref, *rest):
    del pt_ref
    n_seq = q_ref.shape[0]
    pages = rest[:2 * n_seq * n_pages]
    o_ref, q8_ref, newk_ref, newv_ref = rest[2 * n_seq * n_pages:]
    R = 2 * q_ref.shape[1]
    for i in range(n_seq):
        one = pl.ds(i, 1)
        _dsa_sample_attend_seq(q_ref.at[one], kn_ref.at[one], vn_ref.at[one],
                               sel_ref.at[pl.ds(i * R, R)], lo_ref.at[pl.ds(i * R, R)],
                               pages[i * n_pages:(i + 1) * n_pages],
                               pages[(n_seq + i) * n_pages:(n_seq + i + 1) * n_pages],
                               o_ref.at[one], q8_ref.at[i], newk_ref.at[i], newv_ref.at[i])


def _dsa_sample_attend_seq(q_ref, kn_ref, vn_ref, sel_ref, lo_ref, k_refs, v_refs, o_ref,
                           q8_ref, newk_ref, newv_ref):
    T = q_ref.shape[1]
    R = 2 * T
    q8_ref[0:T, :] = q_ref[0].astype(F32)
    q8_ref[T:R, :] = q_ref[0].astype(F32)
    newk_ref[...] = jnp.zeros_like(newk_ref)
    newv_ref[...] = jnp.zeros_like(newv_ref)
    newk_ref[0:T, :] = kn_ref[0]
    newv_ref[0:T, :] = vn_ref[0]
    bias8 = jnp.where(sel_ref[...] >= lo_ref[:, 0:1], 0.0, NEG_INF)
    bias = jnp.concatenate([bias8] * ATT_GROUP, axis=0)
    for g in range(ATT_KV_HEADS):
        qg = jnp.concatenate([q8_ref[:, (g * ATT_GROUP + hq) * LANES:(g * ATT_GROUP + hq + 1) * LANES]
                              for hq in range(ATT_GROUP)], axis=0).astype(BF16)
        head_rows = pl.ds(g, PAGE, stride=ATT_KV_HEADS)
        k_all = jnp.concatenate([r[0, head_rows, :].astype(BF16) for r in k_refs]
                                + [newk_ref[:, g * LANES:(g + 1) * LANES].astype(BF16)], axis=0)
        v_all = jnp.concatenate([r[0, head_rows, :].astype(BF16) for r in v_refs]
                                + [newv_ref[:, g * LANES:(g + 1) * LANES].astype(BF16)], axis=0)
        sc = _dot_nt(qg, k_all) + bias
        m = jnp.max(sc, axis=1, keepdims=True)
        pr = jnp.exp(sc - m)
        out = _dot(pr.astype(BF16), v_all) / jnp.sum(pr, axis=1, keepdims=True)
        for hq in range(ATT_GROUP):
            h = g * ATT_GROUP + hq
            o_ref[0, :, h * LANES:(h + 1) * LANES] = out[hq * R:hq * R + T].astype(o_ref.dtype)


def _dsa_sample_attend(page_table, q, k_new, v_new, sel, lo, cache_k, cache_v):
    nb, t, _ = q.shape
    n_pages = page_table.shape[1]
    kv = ATT_KV_HEADS * ATT_HEAD_DIM
    n_keys = sel.shape[1]
    sq = SAMPLE_SEQS_PER_STEP
    assert nb % sq == 0
    tok = lambda width: pl.BlockSpec((sq, t, width), lambda b, pt: (b, 0, 0))
    page = lambda i, p: pl.BlockSpec((1, ATT_KV_HEADS * PAGE, ATT_HEAD_DIM),
                                     lambda b, pt: (pt[b * sq + i, p], 0, 0))
    pages = [page(i, p) for i in range(sq) for p in range(n_pages)]
    grid_spec = pltpu.PrefetchScalarGridSpec(
        num_scalar_prefetch=1,
        grid=(nb // sq,),
        in_specs=[tok(ATT_HEADS * ATT_HEAD_DIM), tok(kv), tok(kv),
                  pl.BlockSpec((sq * 2 * t, n_keys), lambda b, pt: (b, 0)),
                  pl.BlockSpec((sq * 2 * t, LANES), lambda b, pt: (b, 0))] + pages * 2,
        out_specs=tok(ATT_HEADS * ATT_HEAD_DIM),
        scratch_shapes=[pltpu.VMEM((sq, 2 * t, ATT_HEADS * ATT_HEAD_DIM), F32),
                        pltpu.VMEM((sq, LANES, kv), F32),
                        pltpu.VMEM((sq, LANES, kv), F32)])
    return pl.pallas_call(
        functools.partial(_dsa_sample_attend_kernel, n_pages),
        grid_spec=grid_spec,
        out_shape=jax.ShapeDtypeStruct((nb, t, ATT_HEADS * ATT_HEAD_DIM), BF16),
        compiler_params=_cparams(("parallel",)),
        name="dsa_sample_attend",
    )(page_table, q, k_new, v_new, sel, lo, *([cache_k] * (sq * n_pages)), *([cache_v] * (sq * n_pages)))


def kernel(x_prompt, x_sample, state_gdn, state_gdn_conv, cache_k, cache_v, cache_idx_k, page_table,
           meta_tokens, ln1_g, ln1_b, ln2_g, ln2_b, mlp_w1, mlp_w2,
           gdn_w_in, gdn_conv_w, gdn_a_log, gdn_dt_bias, gdn_norm_w, gdn_w_out,
           dsa_w_in, dsa_ik_norm_g, dsa_ik_norm_b, dsa_w_o):
    nb, seq, d = x_prompt.shape
    ns, ts, _ = x_sample.shape
    n_tok = N_META + seq
    lp = -(-n_tok // KEY_TILE) * KEY_TILE
    n_prompt_rows = nb * lp
    rows = n_prompt_rows + ns * ts
    assert rows % ROW_TILE == 0 and lp % GDN_CHUNK == 0 and lp % KEY_CHUNK == 0 and d == D_MODEL
    kvd = ATT_KV_HEADS * ATT_HEAD_DIM

    meta = meta_tokens.astype(x_prompt.dtype)
    pad = jnp.zeros((lp - n_tok, d), x_prompt.dtype)
    h = jnp.concatenate([piece for b in range(nb) for piece in (meta, x_prompt[b], pad)]
                        + [x_sample.reshape(ns * ts, d)], 0)

    def row2(x):
        return x.reshape(1, -1)

    def prompt_rows(x, width, first=0):
        rows_b = [x[b * lp + first:b * lp + n_tok, :width] for b in range(nb)]
        return jnp.concatenate(rows_b, 0).reshape(nb, n_tok - first, width)

    def with_sample_rows(a_prompt, a_sample):
        return lax.dynamic_update_slice(a_prompt, a_sample.reshape(ns * ts, -1), (n_prompt_rows, 0))

    w_in = gdn_w_in[0]
    split = GDN_CONV_DIM + GDN_VALUE_DIM
    n_gate = 2 * GDN_V_HEADS
    w_cat = jnp.concatenate([w_in, jnp.zeros((d, LANES - n_gate), w_in.dtype)], 1).astype(BF16)
    proj, ba = _gdn_inproj(h, w_cat, split)
    decay_lanes = lambda v: jnp.pad(row2(v), ((0, 0), (GDN_V_HEADS, LANES - n_gate)))
    a_p, gdn_state_prompt = _gdn_prompt(proj, ba, gdn_conv_w[0], decay_lanes(gdn_a_log[0]),
                                        decay_lanes(gdn_dt_bias[0]), row2(gdn_norm_w[0]), nb, lp, n_tok)
    proj_s = proj[n_prompt_rows:].reshape(ns, ts, split)
    mixed_s = proj_s[:, :, :GDN_CONV_DIM]
    a_s, gdn_state_sample = _gdn_sample(state_gdn_conv[0], mixed_s, proj_s[:, :, GDN_CONV_DIM:],
                                        ba[n_prompt_rows:, :n_gate].reshape(ns, ts, n_gate), gdn_conv_w[0],
                                        row2(gdn_a_log[0]), row2(gdn_dt_bias[0]), row2(gdn_norm_w[0]),
                                        state_gdn[0])
    h = _outproj_ln(with_sample_rows(a_p, a_s), h, gdn_w_out[0].astype(BF16), row2(ln1_g[0]), row2(ln1_b[0]))
    h = _mlp_ln(h, mlp_w1[0].astype(BF16), mlp_w2[0].astype(BF16), row2(ln2_g[0]), row2(ln2_b[0]))
    keep = GDN_CONV_WIDTH - 1
    gdn_conv_prompt = prompt_rows(proj, GDN_CONV_DIM, first=n_tok - keep)
    gdn_conv_sample = jnp.concatenate([state_gdn_conv[0], mixed_s], 1)[:, -keep:]

    past = page_table.shape[1] * PAGE
    pos = jnp.concatenate([jnp.tile(jnp.arange(lp, dtype=jnp.int32), nb),
                           jnp.tile(past + jnp.arange(ts, dtype=jnp.int32), ns)])
    tables = _rope_tables(pos, ATT_ROT_HALF, LANES) + _rope_tables(pos, IDX_ROT_HALF, IDX_HEAD_DIM)
    n_in = dsa_w_in.shape[2]
    w_dsa = jnp.pad(dsa_w_in[0], ((0, 0), (0, -n_in % LANES))).astype(BF16)
    pad_lanes = lambda v: row2(jnp.pad(v, (0, LANES - v.shape[0])))
    q, k, v, vt, iq, ikw, ik2 = _dsa_proj(h, w_dsa, tables, pad_lanes(dsa_ik_norm_g[0]),
                                          pad_lanes(dsa_ik_norm_b[0]))
    o_p = _dsa_prompt(q, iq, ikw, k.astype(BF16), vt, ik2, nb, lp, min(TOPK_MAX, (n_tok - N_META) // 4))
    smp = lambda x: x[n_prompt_rows:].reshape(ns, ts, -1)
    n_pool = cache_k.shape[1]
    sel, lo = _dsa_sample_select(page_table, smp(iq), smp(ikw), jnp.swapaxes(cache_idx_k[0], 1, 2),
                                 min(TOPK_MAX, (past + ts) // 4))
    rows_kh = lambda c: c[0].reshape(n_pool, PAGE * ATT_KV_HEADS, ATT_HEAD_DIM)
    o_s = _dsa_sample_attend(page_table, smp(q), smp(k), smp(v), sel, lo, rows_kh(cache_k), rows_kh(cache_v))
    h = _outproj_ln(with_sample_rows(o_p, o_s), h, dsa_w_o[0].astype(BF16), row2(ln1_g[1]), row2(ln1_b[1]))
    mlp_last = (mlp_w1[1].astype(BF16), mlp_w2[1].astype(BF16), row2(ln2_g[1]), row2(ln2_b[1]))
    y_prompt = _mlp_ln_sequences(h, *mlp_last, nb, lp, N_META, seq)
    y_sample = _mlp_ln(h[n_prompt_rows:], *mlp_last).reshape(ns, ts, d)

    heads = lambda x: x.reshape(x.shape[:-1] + (ATT_KV_HEADS, ATT_HEAD_DIM))
    return (y_prompt, y_sample,
            gdn_state_prompt[None], gdn_conv_prompt[None], gdn_state_sample[None], gdn_conv_sample[None],
            heads(prompt_rows(k, kvd))[None], heads(prompt_rows(v, kvd))[None],
            prompt_rows(ikw, IDX_HEAD_DIM)[None],
            heads(smp(k))[None], heads(smp(v))[None], smp(ikw)[:, :, :IDX_HEAD_DIM][None])
```

```python
import functools

import jax
import jax.numpy as jnp
from jax import lax
from jax.experimental import pallas as pl
from jax.experimental.pallas import tpu as pltpu

F32 = jnp.float32
BF16 = jnp.bfloat16
HIGHEST = lax.Precision.HIGHEST

D_MODEL = 1024
N_META = 16
DEPTH = 2
LN_EPS = 1e-5
DEEPNORM_ALPHA = (2 * DEPTH) ** 0.25
GDN_K_HEADS = 8
GDN_V_HEADS = 16
GDN_HEAD = 128
GDN_KEY_DIM = GDN_K_HEADS * GDN_HEAD
GDN_VALUE_DIM = GDN_V_HEADS * GDN_HEAD
GDN_CONV_DIM = 2 * GDN_KEY_DIM + GDN_VALUE_DIM
GDN_CONV_WIDTH = 4
GDN_CHUNK = 128
L2_EPS = 1e-6
RMS_EPS = 1e-6
ATT_HEADS = 8
ATT_KV_HEADS = 2
ATT_HEAD_DIM = 128
ATT_GROUP = ATT_HEADS // ATT_KV_HEADS
IDX_HEADS = 8
IDX_HEAD_DIM = 64
TOPK_MAX = 256
ROPE_THETA = 500000.0
ATT_ROT_HALF = ATT_HEAD_DIM // 8
IDX_ROT_HALF = IDX_HEAD_DIM // 8
PAGE = 128

LANES = 128
SUBLANES = 8
ROW_TILE = 512
KEY_TILE = 128
KEY_CHUNK = 3 * KEY_TILE
VMEM_LIMIT = 56 * 1024 * 1024

NEG_INF = float("-inf")


def _cparams(sem):
    return pltpu.CompilerParams(dimension_semantics=sem, vmem_limit_bytes=VMEM_LIMIT)


def _dot(a, b):
    return jnp.dot(a, b, preferred_element_type=F32)


def _dot_nt(a, b, precision=None):
    return lax.dot_general(a, b, (((1,), (1,)), ((), ())), preferred_element_type=F32,
                           precision=precision)


def _dot_tn(a, b):
    return lax.dot_general(a, b, (((0,), (0,)), ((), ())), preferred_element_type=F32)


def _layernorm_rows(x, g, b):
    mu = jnp.mean(x, axis=-1, keepdims=True)
    xc = x - mu
    var = jnp.mean(xc * xc, axis=-1, keepdims=True)
    return xc * lax.rsqrt(var + LN_EPS) * g + b


def _sigmoid(x):
    return 0.5 * jnp.tanh(0.5 * x) + 0.5


def _silu(x):
    return x * _sigmoid(x)


def _softplus(x):
    return jnp.maximum(x, 0.0) + jnp.log(1.0 + jnp.exp(-jnp.abs(x)))


def _resident(shape):
    return pl.BlockSpec(shape, lambda *_: (0,) * len(shape), pipeline_mode=pl.Buffered(1))


def _gdn_inproj_kernel(x_ref, w_ref, proj_ref, ba_ref):
    x = x_ref[...].astype(BF16)
    n_main = proj_ref.shape[1]
    slab = 1024
    for j in range(n_main // slab):
        proj_ref[:, j * slab:(j + 1) * slab] = _dot(x, w_ref[:, j * slab:(j + 1) * slab])
    ba_ref[...] = _dot(x, w_ref[:, n_main:])


def _gdn_inproj(x, w, n_main):
    rows, k = x.shape
    n = w.shape[1]
    return pl.pallas_call(
        _gdn_inproj_kernel,
        grid=(rows // ROW_TILE,),
        in_specs=[pl.BlockSpec((ROW_TILE, k), lambda i: (i, 0)), _resident((k, n))],
        out_specs=[pl.BlockSpec((ROW_TILE, n_main), lambda i: (i, 0)),
                   pl.BlockSpec((ROW_TILE, n - n_main), lambda i: (i, 0))],
        out_shape=[jax.ShapeDtypeStruct((rows, n_main), F32),
                   jax.ShapeDtypeStruct((rows, n - n_main), F32)],
        compiler_params=_cparams(("parallel",)),
        name="gdn_inproj",
    )(x, w)


def _block_tail_kernel(a_ref, x_ref, wo_ref, g1_ref, b1_ref, w1_ref, w2_ref, g2_ref, b2_ref, o_ref):
    x = _layernorm_rows(DEEPNORM_ALPHA * x_ref[...] + _dot(a_ref[...], wo_ref[...]), g1_ref[...], b1_ref[...])
    t = jnp.maximum(_dot(x.astype(BF16), w1_ref[...]), 0.0)
    y = DEEPNORM_ALPHA * x + _dot((t * t).astype(BF16), w2_ref[...])
    o_ref[...] = _layernorm_rows(y, g2_ref[...], b2_ref[...])


def _tail_weight_specs(k, d, f):
    return [_resident((k, d)), _resident((1, d)), _resident((1, d)),
            _resident((d, f)), _resident((f, d)), _resident((1, d)), _resident((1, d))]


def _block_tail(a, x, weights):
    rows, k = a.shape
    d = x.shape[1]
    f = weights[3].shape[1]
    return pl.pallas_call(
        _block_tail_kernel,
        grid=(rows // ROW_TILE,),
        in_specs=[pl.BlockSpec((ROW_TILE, k), lambda i: (i, 0)),
                  pl.BlockSpec((ROW_TILE, d), lambda i: (i, 0))] + _tail_weight_specs(k, d, f),
        out_specs=pl.BlockSpec((ROW_TILE, d), lambda i: (i, 0)),
        out_shape=jax.ShapeDtypeStruct((rows, d), F32),
        compiler_params=_cparams(("parallel",)),
        name="block_tail",
    )(a, x, *weights)


def _block_tail_sequences(a, x, weights, n_seq, stride, first, n_out):
    k = a.shape[1]
    d = x.shape[1]
    f = weights[3].shape[1]
    align = 2 * SUBLANES
    assert n_out % ROW_TILE == 0 and (stride % align, first % align) == (0, 0)

    def kern(*refs):
        _block_tail_kernel(*refs[:-1], refs[-1].at[0])

    def window(width):
        return pl.BlockSpec((pl.Element(ROW_TILE), pl.Element(width)),
                            lambda s, t: (pl.multiple_of(s * stride + first + t * ROW_TILE, align), 0))

    return pl.pallas_call(
        kern,
        grid=(n_seq, n_out // ROW_TILE),
        in_specs=[window(k), window(d)] + _tail_weight_specs(k, d, f),
        out_specs=pl.BlockSpec((1, ROW_TILE, d), lambda s, t: (s, t, 0)),
        out_shape=jax.ShapeDtypeStruct((n_seq, n_out, d), F32),
        compiler_params=_cparams(("parallel", "parallel")),
        name="block_tail_sequences",
    )(a, x, *weights)


def _unit_lower_inverse(a, n_factors):
    c = a.shape[0]
    row = lax.broadcasted_iota(jnp.int32, (c, c), 0)
    col = lax.broadcasted_iota(jnp.int32, (c, c), 1)
    eye = (row == col).astype(F32)
    p = -a
    t = eye + p
    for _ in range(n_factors - 1):
        pb = p.astype(BF16)
        p = _dot(pb, pb)
        t = t + _dot(t.astype(BF16), p.astype(BF16))
    return t


def _bdot(a, b):
    return lax.dot_general(a, b, (((2,), (1,)), ((0,), (0,))), preferred_element_type=F32)


def _bdot_nt(a, b):
    return lax.dot_general(a, b, (((2,), (2,)), ((0,), (0,))), preferred_element_type=F32)


def _bdot_tn(a, b):
    return lax.dot_general(a, b, (((1,), (1,)), ((0,), (0,))), preferred_element_type=F32)


def _unit_lower_inverse_batched(a):
    c = a.shape[-1]
    row = lax.broadcasted_iota(jnp.int32, (c, c), 0)
    col = lax.broadcasted_iota(jnp.int32, (c, c), 1)
    def coupling(s):
        couples = ((row // (2 * s)) == (col // (2 * s))) & ((row // s) != (col // s))
        return jnp.where(couples, a, 0.0)

    t = (row == col).astype(F32) - coupling(1)
    s = 2
    while s < c:
        tb = t.astype(BF16)
        t = t - _bdot(_bdot(tb, coupling(s).astype(BF16)).astype(BF16), tb)
        s *= 2
    return t


def _pairwise_diff(col):
    c = col.shape[0]
    lane = lax.broadcasted_iota(jnp.int32, (c, LANES), 1)
    left = jnp.where(lane == 0, col, jnp.where(lane == 1, 1.0, 0.0))
    right = jnp.where(lane == 0, 1.0, jnp.where(lane == 1, -col, 0.0))
    return _dot_nt(left, right, precision=HIGHEST)


def _l2norm_rows(x):
    return x * lax.rsqrt(jnp.sum(x * x, axis=-1, keepdims=True) + L2_EPS)


def _gated_rmsnorm(o, z, norm_w):
    on = o * lax.rsqrt(jnp.mean(o * o, axis=-1, keepdims=True) + RMS_EPS) * norm_w
    return on * _silu(z)


def _gdn_gates(ba, a_log, dt_bias):
    beta = _sigmoid(ba[:, :GDN_V_HEADS])
    g = -jnp.exp(a_log) * _softplus(ba[:, GDN_V_HEADS:] + dt_bias)
    return beta, g


def _gdn_prompt_kernel(n_tokens, mixed_ref, z_ref, ba_ref, convw_ref, alog_ref, dtb_ref, normw_ref,
                       o_ref, sfin_ref, xc_ref, s_ref):
    c = pl.program_id(1)
    C = GDN_CHUNK
    W = GDN_CONV_WIDTH

    @pl.when(c == 0)
    def _():
        xc_ref[...] = jnp.zeros_like(xc_ref)
        s_ref[...] = jnp.zeros_like(s_ref)

    x = mixed_ref[...]
    tail = xc_ref[...]
    w = convw_ref[...]
    row8 = lax.broadcasted_iota(jnp.int32, (SUBLANES, 1), 0)
    acc = x * w[W - 1:W, :]
    for s in range(1, W):
        shifted = pltpu.roll(x, s, axis=0)
        head = jnp.where(row8 < s, pltpu.roll(tail, s, axis=0), shifted[:SUBLANES])
        acc = acc + jnp.concatenate([head, shifted[SUBLANES:]], axis=0) * w[W - 1 - s:W - s, :]
    qkv = _silu(acc)
    xc_ref[...] = x[C - SUBLANES:]

    row = lax.broadcasted_iota(jnp.int32, (C, 1), 0)
    valid = (c * C + row) < n_tokens
    ba = ba_ref[...]
    beta = jnp.where(valid, _sigmoid(ba), 0.0)
    g = jnp.where(valid, -jnp.exp(alog_ref[...]) * _softplus(ba + dtb_ref[...]), 0.0)

    ri = lax.broadcasted_iota(jnp.int32, (C, C), 0)
    ci = lax.broadcasted_iota(jnp.int32, (C, C), 1)
    tril = ci <= ri
    strict = ci < ri
    gc = jnp.dot(tril.astype(F32), g, preferred_element_type=F32, precision=HIGHEST)
    gc_rows = gc.T
    egc = jnp.exp(gc)
    glast = gc[C - 1:C, :]
    ekd = jnp.exp(glast - gc)
    eglast = jnp.exp(glast)
    normw = normw_ref[...]
    NH = GDN_V_HEADS

    def gcol(x, h):
        return x[:, NH + h:NH + h + 1]

    qk_raw = [qkv[:, i * GDN_HEAD:(i + 1) * GDN_HEAD] for i in range(2 * GDN_K_HEADS)]
    sq = jnp.concatenate([t * t for t in qk_raw], axis=0)
    sq_hi = sq.astype(BF16)
    sq_lo = (sq - sq_hi.astype(F32)).astype(BF16)
    ones = jnp.ones((GDN_HEAD, GDN_HEAD), BF16)
    inv_norm = lax.rsqrt(_dot(sq_hi, ones) + _dot(sq_lo, ones) + L2_EPS)
    qk_n = [t * inv_norm[i * C:(i + 1) * C] for i, t in enumerate(qk_raw)]
    q_l = [t * (GDN_HEAD ** -0.5) for t in qk_n[:GDN_K_HEADS]]
    k_l = qk_n[GDN_K_HEADS:]
    k8 = jnp.stack(k_l).astype(BF16)
    kk8 = _bdot_nt(k8, k8)
    qk8 = _bdot_nt(jnp.stack(q_l).astype(BF16), k8)
    a_l, qkd_l, rhs_l, qg_l, kd_l = [], [], [], [], []
    for h in range(NH):
        kh = h // 2
        v = qkv[:, 2 * GDN_KEY_DIM + h * GDN_HEAD:2 * GDN_KEY_DIM + (h + 1) * GDN_HEAD]
        bcol = beta[:, h:h + 1]
        diff = gcol(gc, h) - gc_rows[NH + h:NH + h + 1, :]
        decay = jnp.where(tril, jnp.exp(jnp.where(tril, diff, 0.0)), 0.0)
        a_l.append(jnp.where(strict, kk8[kh] * decay * bcol, 0.0))
        qkd_l.append((qk8[kh] * decay).astype(BF16))
        rhs_l.append(jnp.concatenate([v * bcol, k_l[kh] * (bcol * gcol(egc, h))], axis=1).astype(BF16))
        qg_l.append(q_l[kh] * gcol(egc, h))
        kd_l.append((k_l[kh] * gcol(ekd, h)).astype(BF16))
    t_inv = _unit_lower_inverse_batched(jnp.stack(a_l)).astype(BF16)
    sol = _bdot(t_inv, jnp.stack(rhs_l))
    u, wm = sol[:, :, :GDN_HEAD], sol[:, :, GDN_HEAD:]
    s_old = s_ref[...]
    lhs = jnp.concatenate([wm, jnp.stack(qg_l)], axis=1).astype(BF16)
    ws_qs = _bdot(lhs, s_old.astype(BF16))
    v_new = u - ws_qs[:, :C]
    vnb = v_new.astype(BF16)
    o = ws_qs[:, C:] + _bdot(jnp.stack(qkd_l), vnb)
    upd = _bdot_tn(jnp.stack(kd_l), vnb)
    for h in range(NH):
        s_ref[h] = s_old[h] * gcol(eglast, h) + upd[h]
        zh = z_ref[:, h * GDN_HEAD:(h + 1) * GDN_HEAD]
        o_ref[:, h * GDN_HEAD:(h + 1) * GDN_HEAD] = _gated_rmsnorm(o[h], zh, normw).astype(o_ref.dtype)

    @pl.when(c == pl.num_programs(1) - 1)
    def _():
        sfin_ref[0] = s_ref[...]


def _gdn_prompt(proj, ba, conv_w, a_log, dt_bias, norm_w, n_batch, lp, n_tokens):
    C = GDN_CHUNK
    per_b = lp // C
    n_chunks = per_b
    z_off = GDN_CONV_DIM // GDN_VALUE_DIM
    kern = functools.partial(_gdn_prompt_kernel, n_tokens)
    return pl.pallas_call(
        kern,
        grid=(n_batch, n_chunks),
        in_specs=[pl.BlockSpec((C, GDN_CONV_DIM), lambda b, c: (b * per_b + c, 0)),
                  pl.BlockSpec((C, GDN_VALUE_DIM), lambda b, c: (b * per_b + c, z_off)),
                  pl.BlockSpec((C, LANES), lambda b, c: (b * per_b + c, 0)),
                  pl.BlockSpec((GDN_CONV_WIDTH, GDN_CONV_DIM), lambda b, c: (0, 0)),
                  pl.BlockSpec((1, LANES), lambda b, c: (0, 0)),
                  pl.BlockSpec((1, LANES), lambda b, c: (0, 0)),
                  pl.BlockSpec((1, GDN_HEAD), lambda b, c: (0, 0))],
        out_specs=[pl.BlockSpec((C, GDN_VALUE_DIM), lambda b, c: (b * per_b + c, 0)),
                   pl.BlockSpec((1, GDN_V_HEADS, GDN_HEAD, GDN_HEAD), lambda b, c: (b, 0, 0, 0))],
        out_shape=[jax.ShapeDtypeStruct((proj.shape[0], GDN_VALUE_DIM), BF16),
                   jax.ShapeDtypeStruct((n_batch, GDN_V_HEADS, GDN_HEAD, GDN_HEAD), F32)],
        scratch_shapes=[pltpu.VMEM((SUBLANES, GDN_CONV_DIM), F32),
                        pltpu.VMEM((GDN_V_HEADS, GDN_HEAD, GDN_HEAD), F32)],
        compiler_params=_cparams(("parallel", "arbitrary")),
        name="gdn_prompt",
    )(proj, proj, ba, conv_w, a_log, dt_bias, norm_w)


def _twice(dst_ref, x):
    t = x.shape[0]
    dst_ref[0:t, :] = x
    dst_ref[t:2 * t, :] = x
    return dst_ref[...]


SAMPLE_SEQS_PER_STEP = 2


def _gdn_sample_kernel(cs_ref, mixed_ref, z_ref, ba_ref, convw_ref, alog_ref, dtb_ref, normw_ref, s_ref,
                       o_ref, snew_ref, xc_ref, dq_ref, dz_ref, dba_ref):
    for i in range(mixed_ref.shape[0]):
        one = pl.ds(i, 1)
        _gdn_sample_seq(cs_ref.at[one], mixed_ref.at[one], z_ref.at[one], ba_ref.at[one], convw_ref, alog_ref,
                        dtb_ref, normw_ref, s_ref.at[one], o_ref.at[one], snew_ref.at[one],
                        xc_ref.at[i], dq_ref.at[i], dz_ref.at[i], dba_ref.at[i])


def _gdn_sample_seq(cs_ref, mixed_ref, z_ref, ba_ref, convw_ref, alog_ref, dtb_ref, normw_ref, s_ref,
                    o_ref, snew_ref, xc_ref, dq_ref, dz_ref, dba_ref):
    T = mixed_ref.shape[1]
    R = 2 * T
    W = GDN_CONV_WIDTH
    xc_ref[0:W - 1, :] = cs_ref[0]
    xc_ref[W - 1:W - 1 + T, :] = mixed_ref[0]
    w = convw_ref[...]
    acc = xc_ref[0:T, :] * w[0:1, :]
    for j in range(1, W):
        acc = acc + xc_ref[j:j + T, :] * w[j:j + 1, :]
    qkv = _twice(dq_ref, _silu(acc))
    z2 = _twice(dz_ref, z_ref[0])
    ba2 = _twice(dba_ref, ba_ref[0])
    beta, g = _gdn_gates(ba2, alog_ref[...], dtb_ref[...])

    rr = lax.broadcasted_iota(jnp.int32, (R, 1), 0)
    tmod = rr % T
    first = rr < T
    gc = jnp.zeros_like(g)
    for s in range(T):
        gc = gc + jnp.where(tmod >= s, g[s:s + 1, :], 0.0)
    glast = gc[T - 1:T, :]
    normw = normw_ref[...]

    NP = GDN_K_HEADS
    M = NP * R

    def pair_col(x, kh):
        return jnp.where(first, x[:, 2 * kh:2 * kh + 1], x[:, 2 * kh + 1:2 * kh + 2])

    def pair_tile(x, base, kh):
        a = x[:, base + (2 * kh) * GDN_HEAD:base + (2 * kh + 1) * GDN_HEAD]
        b = x[:, base + (2 * kh + 1) * GDN_HEAD:base + (2 * kh + 2) * GDN_HEAD]
        return jnp.where(first, a, b)

    q_t, k_t, v_t, z_t, b_c, gc_c, gl_c = [], [], [], [], [], [], []
    for kh in range(NP):
        q_t.append(_l2norm_rows(qkv[:, kh * GDN_HEAD:(kh + 1) * GDN_HEAD]) * (GDN_HEAD ** -0.5))
        k_t.append(_l2norm_rows(qkv[:, GDN_KEY_DIM + kh * GDN_HEAD:GDN_KEY_DIM + (kh + 1) * GDN_HEAD]))
        v_t.append(pair_tile(qkv, 2 * GDN_KEY_DIM, kh))
        z_t.append(pair_tile(z2, 0, kh))
        b_c.append(pair_col(beta, kh))
        gc_c.append(pair_col(gc, kh))
        gl_c.append(pair_col(jnp.broadcast_to(glast, (R, GDN_V_HEADS)), kh))
    q64 = jnp.concatenate(q_t, axis=0)
    k64 = jnp.concatenate(k_t, axis=0)
    v64 = jnp.concatenate(v_t, axis=0)
    bcol = jnp.concatenate(b_c, axis=0)
    gcol = jnp.concatenate(gc_c, axis=0)
    glcol = jnp.concatenate(gl_c, axis=0)

    ri = lax.broadcasted_iota(jnp.int32, (M, M), 0)
    ci = lax.broadcasted_iota(jnp.int32, (M, M), 1)
    same = (ri // T) == (ci // T)
    tril = same & ((ci % T) <= (ri % T))
    strict = same & ((ci % T) < (ri % T))
    kb = k64.astype(BF16)
    kk = _dot_nt(kb, kb)
    qk0 = _dot_nt(q64.astype(BF16), kb)
    diff = _pairwise_diff(gcol)
    decay = jnp.where(tril, jnp.exp(jnp.where(tril, diff, 0.0)), 0.0)
    a_mat = jnp.where(strict, kk * decay * bcol, 0.0)
    t_inv = _unit_lower_inverse(a_mat, max(1, (T - 1).bit_length())).astype(BF16)
    egc = jnp.exp(gcol)
    u = _dot(t_inv, (v64 * bcol).astype(BF16))
    wm = _dot(t_inv, (k64 * (bcol * egc)).astype(BF16))
    qg = q64 * egc
    kd = k64 * jnp.exp(glcol - gcol)
    qkd = (qk0 * decay).astype(BF16)

    v_new_t, qs_t = [], []
    for kh in range(NP):
        lhs = jnp.concatenate([wm[kh * R:(kh + 1) * R], qg[kh * R:(kh + 1) * R]], axis=0).astype(BF16)
        r0 = _dot(lhs, s_ref[0, 2 * kh].astype(BF16))
        r1 = _dot(lhs, s_ref[0, 2 * kh + 1].astype(BF16))
        ws = jnp.where(first, r0[:R], r1[:R])
        qs_t.append(jnp.where(first, r0[R:], r1[R:]))
        v_new_t.append(u[kh * R:(kh + 1) * R] - ws)
    v_new = jnp.concatenate(v_new_t, axis=0)
    o64 = jnp.concatenate(qs_t, axis=0) + _dot(qkd, v_new.astype(BF16))
    on = _gated_rmsnorm(o64, jnp.concatenate(z_t, axis=0), normw)

    for kh in range(NP):
        vn = v_new_t[kh].astype(BF16)
        kdp = kd[kh * R:(kh + 1) * R]
        for j in range(2):
            h = 2 * kh + j
            keep = first if j == 0 else jnp.logical_not(first)
            kdm = jnp.where(keep, kdp, 0.0).astype(BF16)
            eg = jnp.exp(glast[:, h:h + 1])
            snew_ref[0, h] = s_ref[0, h] * eg + _dot_tn(kdm, vn)
            tile = on[kh * R:(kh + 1) * R]
            if j == 1:
                tile = pltpu.roll(tile, T, axis=0)
            o_ref[0, :, h * GDN_HEAD:(h + 1) * GDN_HEAD] = tile[:T].astype(o_ref.dtype)


def _gdn_sample(conv_state, mixed, z, ba, conv_w, a_log, dt_bias, norm_w, state):
    nb, t, _ = mixed.shape
    sq = SAMPLE_SEQS_PER_STEP
    assert nb % sq == 0
    seq = lambda *tail: pl.BlockSpec((sq,) + tail, lambda b: (b,) + (0,) * len(tail))
    return pl.pallas_call(
        _gdn_sample_kernel,
        grid=(nb // sq,),
        in_specs=[seq(GDN_CONV_WIDTH - 1, GDN_CONV_DIM), seq(t, GDN_CONV_DIM), seq(t, GDN_VALUE_DIM),
                  seq(t, 2 * GDN_V_HEADS),
                  _resident((GDN_CONV_WIDTH, GDN_CONV_DIM)), _resident((1, GDN_V_HEADS)),
                  _resident((1, GDN_V_HEADS)), _resident((1, GDN_HEAD)),
                  seq(GDN_V_HEADS, GDN_HEAD, GDN_HEAD)],
        out_specs=[seq(t, GDN_VALUE_DIM), seq(GDN_V_HEADS, GDN_HEAD, GDN_HEAD)],
        out_shape=[jax.ShapeDtypeStruct((nb, t, GDN_VALUE_DIM), BF16),
                   jax.ShapeDtypeStruct(state.shape, F32)],
        scratch_shapes=[pltpu.VMEM((sq, GDN_CONV_WIDTH - 1 + t, GDN_CONV_DIM), F32),
                        pltpu.VMEM((sq, 2 * t, GDN_CONV_DIM), F32),
                        pltpu.VMEM((sq, 2 * t, GDN_VALUE_DIM), F32),
                        pltpu.VMEM((sq, 2 * t, 2 * GDN_V_HEADS), F32)],
        compiler_params=_cparams(("parallel",)),
        name="gdn_sample",
    )(conv_state, mixed, z, ba, conv_w, a_log, dt_bias, norm_w, state)


def _rope_tables(pos, half, period):
    inv_freq = ROPE_THETA ** (-jnp.arange(half, dtype=F32) * 2.0 / (2 * half))
    ang = pos.astype(F32)[:, None] * inv_freq[None, :]
    cos, sin = jnp.cos(ang), jnp.sin(ang)
    ones = jnp.ones((pos.shape[0], period - 2 * half), F32)
    cos_p = jnp.concatenate([cos, cos, ones], axis=1)
    sin_p = jnp.concatenate([-sin, sin, 0.0 * ones], axis=1)
    reps = LANES // period
    return jnp.tile(cos_p, (1, reps)), jnp.tile(sin_p, (1, reps))


def _rope_tile(x, cos, sin, half, period):
    lane = lax.broadcasted_iota(jnp.int32, x.shape, 1) % period
    partner = jnp.where(lane < half, pltpu.roll(x, LANES - half, axis=1), pltpu.roll(x, half, axis=1))
    return x * cos + partner * sin


def _dsa_proj_kernel(x_ref, w_ref, ca_ref, sa_ref, ci_ref, si_ref, g_ref, b_ref,
                     q_ref, k_ref, v_ref, vt_ref, iq_ref, ikw_ref, ik2_ref):
    x = x_ref[...].astype(BF16)
    ca, sa, ci, si = ca_ref[...], sa_ref[...], ci_ref[...], si_ref[...]
    q_off, k_off = 0, ATT_HEADS * ATT_HEAD_DIM
    v_off = k_off + ATT_KV_HEADS * ATT_HEAD_DIM
    iq_off = v_off + ATT_KV_HEADS * ATT_HEAD_DIM
    ik_off = iq_off + IDX_HEADS * IDX_HEAD_DIM

    def proj(off):
        return _dot(x, w_ref[:, off:off + LANES])

    for h in range(ATT_HEADS):
        t = _rope_tile(proj(q_off + h * LANES), ca, sa, ATT_ROT_HALF, LANES)
        q_ref[:, h * LANES:(h + 1) * LANES] = (t * (ATT_HEAD_DIM ** -0.5)).astype(q_ref.dtype)
    for h in range(ATT_KV_HEADS):
        k_ref[:, h * LANES:(h + 1) * LANES] = _rope_tile(proj(k_off + h * LANES), ca, sa, ATT_ROT_HALF, LANES)
        v = proj(v_off + h * LANES)
        v_ref[:, h * LANES:(h + 1) * LANES] = v
        vt_ref[h * LANES:(h + 1) * LANES, :] = v.T.astype(vt_ref.dtype)
    for h in range(IDX_HEADS * IDX_HEAD_DIM // LANES):
        t = _rope_tile(proj(iq_off + h * LANES), ci, si, IDX_ROT_HALF, IDX_HEAD_DIM)
        iq_ref[:, h * LANES:(h + 1) * LANES] = t.astype(iq_ref.dtype)
    t = proj(ik_off)
    lane = lax.broadcasted_iota(jnp.int32, t.shape, 1)
    is_ik = lane < IDX_HEAD_DIM
    mu = jnp.sum(jnp.where(is_ik, t, 0.0), axis=-1, keepdims=True) / IDX_HEAD_DIM
    tc = jnp.where(is_ik, t - mu, 0.0)
    var = jnp.sum(tc * tc, axis=-1, keepdims=True) / IDX_HEAD_DIM
    ik = _rope_tile(tc * lax.rsqrt(var + LN_EPS) * g_ref[...] + b_ref[...], ci, si, IDX_ROT_HALF, IDX_HEAD_DIM)
    ikw = jnp.where(is_ik, ik, t * (IDX_HEADS ** -0.5))
    ikw_ref[...] = ikw
    ik_only = jnp.where(is_ik, ik, 0.0)
    ik2_ref[...] = (ik_only + pltpu.roll(ik_only, IDX_HEAD_DIM, axis=1)).astype(ik2_ref.dtype)


def _dsa_proj(x, w, tables, ik_g, ik_b):
    rows, d = x.shape
    n = w.shape[1]
    kv = ATT_KV_HEADS * ATT_HEAD_DIM
    row_spec = lambda width: pl.BlockSpec((ROW_TILE, width), lambda i: (i, 0))
    const_spec = _resident
    return pl.pallas_call(
        _dsa_proj_kernel,
        grid=(rows // ROW_TILE,),
        in_specs=[row_spec(d), const_spec((d, n))] + [row_spec(LANES)] * 4 + [const_spec((1, LANES))] * 2,
        out_specs=[row_spec(ATT_HEADS * ATT_HEAD_DIM), row_spec(kv), row_spec(kv),
                   pl.BlockSpec((kv, ROW_TILE), lambda i: (0, i)),
                   row_spec(IDX_HEADS * IDX_HEAD_DIM), row_spec(LANES), row_spec(LANES)],
        out_shape=[jax.ShapeDtypeStruct((rows, ATT_HEADS * ATT_HEAD_DIM), BF16),
                   jax.ShapeDtypeStruct((rows, kv), F32),
                   jax.ShapeDtypeStruct((rows, kv), F32),
                   jax.ShapeDtypeStruct((kv, rows), BF16),
                   jax.ShapeDtypeStruct((rows, IDX_HEADS * IDX_HEAD_DIM), BF16),
                   jax.ShapeDtypeStruct((rows, LANES), F32),
                   jax.ShapeDtypeStruct((rows, LANES), BF16)],
        compiler_params=_cparams(("parallel",)),
        name="dsa_proj",
    )(x, w, *tables, ik_g, ik_b)


def _float_order_key(x):
    b = pltpu.bitcast(x, jnp.int32)
    return b ^ ((b >> 31) & jnp.int32(0x7FFFFFFF))


def _float_from_key(k):
    return pltpu.bitcast(k ^ ((k >> 31) & jnp.int32(0x7FFFFFFF)), F32)


VALUE_STEPS = 8
VALUE_ROUNDS = 2
KEY_STEPS = 4
KEY_ROUNDS = 32 // KEY_STEPS
F32_MAX = 3.4028234663852886e38


def _kth_largest_bounds(count_ge, max_below, target, amax):
    bound = jnp.minimum(2.0 * amax, F32_MAX)
    lo = _float_order_key(-bound)
    hi = _float_order_key(bound) + 1
    c_lo = jnp.full(target.shape, -1.0, F32)

    def step(carry, split_values):
        lo, hi, c_lo, lo_val = carry
        mid = (lo >> 1) + (hi >> 1) + (lo & hi & 1)
        if split_values:
            vmid = _float_order_key(_float_from_key(lo) * 0.5 + _float_from_key(hi) * 0.5)
            mid = jnp.where((vmid > lo) & (vmid < hi), vmid, mid)
        mid_val = _float_from_key(mid)
        cnt = count_ge(mid_val)
        ok = cnt >= target
        return (jnp.where(ok, mid, lo), jnp.where(ok, hi, mid), jnp.where(ok, cnt, c_lo),
                jnp.where(ok & (mid != lo), mid_val, lo_val))

    def pending(carry):
        lo, hi, c_lo, _ = carry
        return (c_lo != target) & (hi != lo + 1)

    def any_pending(carry):
        return jnp.max(jnp.where(pending(carry), 1.0, 0.0)) > 0.0

    def rounds(carry, n_rounds, n_steps, split_values):
        def body(state):
            it, carry = state
            for _ in range(n_steps):
                carry = step(carry, split_values)
            return it + 1, carry

        return lax.while_loop(lambda st: (st[0] < n_rounds) & any_pending(st[1]), body, (jnp.int32(0), carry))[1]

    def snap(carry):
        lo, hi, c_lo, lo_val = carry
        v = max_below(_float_from_key(hi))
        cnt = count_ge(v)
        hit = pending(carry) & (cnt >= target)
        vk = _float_order_key(jnp.where(v == 0.0, 0.0, v))
        return (jnp.where(hit, vk, lo), jnp.where(hit, vk + 1, hi), jnp.where(hit, cnt, c_lo),
                jnp.where(hit, v, lo_val))

    def snap_round(state):
        it, carry = state
        return it + 1, rounds(snap(carry), 1, KEY_STEPS, False)

    carry = rounds((lo, hi, c_lo, -bound), VALUE_ROUNDS, VALUE_STEPS, True)
    carry = lax.while_loop(lambda st: (st[0] < KEY_ROUNDS) & any_pending(st[1]), snap_round,
                           (jnp.int32(0), carry))[1]
    return carry[3]


def _dsa_prompt_kernel(topk, q_ref, iq_ref, ikw_ref, k_ref, vt_ref, ik2_ref, o_ref,
                       s_ref, xh_ref, qg_ref, sc_ref, acc_ref):
    i = pl.program_id(1)
    KT = KEY_TILE
    CH = KEY_CHUNK
    nc = i // (CH // KT) + 1
    row = lax.broadcasted_iota(jnp.int32, (CH, KT), 0)
    lane = lax.broadcasted_iota(jnp.int32, (KT, KT), 1)
    qpos = i * KT + lax.broadcasted_iota(jnp.int32, (1, KT), 1)
    GQ = ATT_GROUP * KT

    def chunk(c):
        return pl.ds(pl.multiple_of(c * CH, CH), CH)

    NACC = 8

    def fold_rows(x):
        return x.reshape(CH // (NACC * SUBLANES), NACC, SUBLANES, x.shape[-1])

    def unfold(x, op):
        return op(op(x, axis=0), axis=0, keepdims=True)

    for h in range(IDX_HEADS):
        tile = iq_ref[:, (h // 2) * LANES:(h // 2 + 1) * LANES]
        mine = (lane // IDX_HEAD_DIM) == (h % 2)
        xh_ref[h * KT:(h + 1) * KT, :] = jnp.where(mine, tile, jnp.zeros_like(tile))
    for g in range(ATT_KV_HEADS):
        for hq in range(ATT_GROUP):
            h = g * ATT_GROUP + hq
            qg_ref[g, hq * KT:(hq + 1) * KT, :] = q_ref[:, h * LANES:(h + 1) * LANES]
    w_rows = ikw_ref[...].T[IDX_HEAD_DIM:IDX_HEAD_DIM + IDX_HEADS, :]

    n_pairs = (nc + 1) // 2

    def pair(j):
        return 2 * j, jnp.minimum(2 * j + 1, nc - 1)

    def score_chunk(c, amax):
        d = _dot_nt(ik2_ref[chunk(c), :], xh_ref[...])
        acc = w_rows[0:1, :] * jnp.maximum(d[:, 0:KT], 0.0)
        for h in range(1, IDX_HEADS):
            acc = acc + w_rows[h:h + 1, :] * jnp.maximum(d[:, h * KT:(h + 1) * KT], 0.0)
        acc = acc * (IDX_HEAD_DIM ** -0.5)
        kpos = c * CH + row
        causal = kpos <= qpos
        meta = kpos < N_META
        s_ref[chunk(c), :] = jnp.where(causal, jnp.where(meta, jnp.inf, acc), NEG_INF)
        finite = jnp.where(causal & jnp.logical_not(meta), jnp.abs(acc), 0.0)
        return jnp.maximum(amax, jnp.max(fold_rows(finite), axis=0))

    def score_pair(j, amax):
        c0, c1 = pair(j)
        return score_chunk(c1, score_chunk(c0, amax))

    amax = unfold(lax.fori_loop(0, n_pairs, score_pair, jnp.zeros((NACC, SUBLANES, KT), F32)), jnp.max)

    target = jnp.minimum(qpos + 1, topk).astype(F32)

    def count_where(test):
        def body(c, acc):
            hit = jnp.where(test(s_ref[chunk(c), :]), 1.0, 0.0)
            return acc + jnp.sum(fold_rows(hit), axis=0)
        return unfold(lax.fori_loop(0, nc, body, jnp.zeros((NACC, SUBLANES, KT), F32)), jnp.sum)

    def count_ge(thr):
        return count_where(lambda t: t >= thr)

    def max_below(thr):
        def body(c, acc):
            t = s_ref[chunk(c), :]
            return jnp.maximum(acc, jnp.max(fold_rows(jnp.where(t < thr, t, NEG_INF)), axis=0))
        return unfold(lax.fori_loop(0, nc, body, jnp.full((NACC, SUBLANES, KT), NEG_INF, F32)), jnp.max)

    lo = _kth_largest_bounds(count_ge, max_below, target, amax)
    surplus = count_ge(lo) - target

    @pl.when(jnp.max(surplus) > 0.0)
    def _():
        need = target - count_where(lambda t: t > lo)
        rr = lax.broadcasted_iota(jnp.int32, (CH, CH), 0)
        cc = lax.broadcasted_iota(jnp.int32, (CH, CH), 1)
        tri = (cc <= rr).astype(BF16)

        def drop(c, carry):
            t = s_ref[chunk(c), :]
            tie = t == lo
            tie_f = jnp.where(tie, 1.0, 0.0)
            rank = carry + _dot(tri, tie_f.astype(BF16))
            s_ref[chunk(c), :] = jnp.where(tie & (rank > need), NEG_INF, t)
            return carry + jnp.sum(tie_f, axis=0, keepdims=True)

        lax.fori_loop(0, nc, drop, jnp.zeros((1, KT), F32))

    def bias_cols(c):
        b = jnp.where(s_ref[chunk(c), :] >= lo, 0.0, NEG_INF)
        return jnp.concatenate([b] * ATT_GROUP, axis=1)

    G = ATT_KV_HEADS

    def max_chunk(c, macc):
        bias = bias_cols(c)
        tops = []
        for g in range(G):
            sc = _dot_nt(k_ref[chunk(c), g * LANES:(g + 1) * LANES], qg_ref[g]) + bias
            sc_ref[chunk(c), g * GQ:(g + 1) * GQ] = sc
            tops.append(jnp.max(fold_rows(sc), axis=0))
        return jnp.maximum(macc, jnp.concatenate(tops, axis=-1))

    def max_pair(j, macc):
        c0, c1 = pair(j)
        return max_chunk(c1, max_chunk(c0, macc))

    macc = lax.fori_loop(0, n_pairs, max_pair, jnp.full((NACC, SUBLANES, G * GQ), NEG_INF, F32))
    m = unfold(macc, jnp.max)

    ones_rows = jnp.ones((2 * SUBLANES, CH), BF16)
    acc_ref[...] = jnp.zeros_like(acc_ref)

    def sum_chunk(c, shift):
        for g in range(G):
            p = jnp.exp(sc_ref[chunk(c), g * GQ:(g + 1) * GQ] - shift[:, g * GQ:(g + 1) * GQ]).astype(BF16)
            vt = jnp.concatenate([vt_ref[g * LANES:(g + 1) * LANES, chunk(c)], ones_rows], axis=0)
            acc_ref[g] += _dot(vt, p)

    def sum_pair(j, _):
        c0, c1 = pair(j)
        sum_chunk(c0, m)
        sum_chunk(c1, jnp.where(2 * j + 1 < nc, m, jnp.inf))
        return 0

    lax.fori_loop(0, n_pairs, sum_pair, 0)
    for g in range(G):
        acc = acc_ref[g]
        out_t = acc[:ATT_HEAD_DIM] / acc[ATT_HEAD_DIM:ATT_HEAD_DIM + 1]
        for hq in range(ATT_GROUP):
            h = g * ATT_GROUP + hq
            o_ref[:, h * LANES:(h + 1) * LANES] = out_t[:, hq * KT:(hq + 1) * KT].T.astype(o_ref.dtype)


def _dsa_prompt(q, iq, ikw, k, vt, ik2, n_batch, lp, topk):
    KT = KEY_TILE
    nq = lp // KT
    kv = ATT_KV_HEADS * ATT_HEAD_DIM
    qspec = lambda width: pl.BlockSpec((KT, width), lambda b, i: (b * nq + i, 0))
    return pl.pallas_call(
        functools.partial(_dsa_prompt_kernel, topk),
        grid=(n_batch, nq),
        in_specs=[qspec(ATT_HEADS * ATT_HEAD_DIM), qspec(IDX_HEADS * IDX_HEAD_DIM), qspec(LANES),
                  pl.BlockSpec((lp, kv), lambda b, i: (b, 0)),
                  pl.BlockSpec((kv, lp), lambda b, i: (0, b)),
                  pl.BlockSpec((lp, LANES), lambda b, i: (b, 0))],
        out_specs=qspec(ATT_HEADS * ATT_HEAD_DIM),
        out_shape=jax.ShapeDtypeStruct(q.shape, BF16),
        scratch_shapes=[pltpu.VMEM((lp, KT), F32),
                        pltpu.VMEM((IDX_HEADS * KT, LANES), BF16),
                        pltpu.VMEM((ATT_KV_HEADS, ATT_GROUP * KT, ATT_HEAD_DIM), BF16),
                        pltpu.VMEM((lp, ATT_HEADS * KT), F32),
                        pltpu.VMEM((ATT_KV_HEADS, ATT_HEAD_DIM + 2 * SUBLANES, ATT_GROUP * KT), F32)],
        compiler_params=_cparams(("parallel", "arbitrary")),
        name="dsa_prompt",
    )(q, iq, ikw, k, vt, ik2)


SAMPLE_GROUP = 16


def _dsa_sample_select_kernel(topk, n_pages, pt_ref, iq_ref, ikw_ref, *rest):
    del pt_ref
    page_refs = rest[:n_pages]
    sel_ref, lo_ref, stack_ref, st_ref, iq8_ref, ikw8_ref, newk_ref = rest[n_pages:]
    b = pl.program_id(0)
    j = b % SAMPLE_GROUP
    T = iq_ref.shape[1]
    R = 2 * T
    past = n_pages * PAGE
    n_keys = stack_ref.shape[1]
    GR = SAMPLE_GROUP * R

    iq8_ref[0:T, :] = iq_ref[0].astype(F32)
    iq8_ref[T:R, :] = iq_ref[0].astype(F32)
    ikw8_ref[0:T, :] = ikw_ref[0]
    ikw8_ref[T:R, :] = ikw_ref[0]
    lane = lax.broadcasted_iota(jnp.int32, (R, LANES), 1)
    ikw8 = ikw8_ref[...]
    xh = []
    for h in range(IDX_HEADS):
        tile = iq8_ref[:, (h // 2) * LANES:(h // 2 + 1) * LANES]
        if h % 2:
            tile = pltpu.roll(tile, IDX_HEAD_DIM, axis=1)
        xh.append(jnp.where(lane < IDX_HEAD_DIM, tile, 0.0))
    x_all = jnp.concatenate(xh, axis=0).astype(BF16)

    def index_scores(d):
        acc = jnp.zeros((R, LANES), F32)
        for h in range(IDX_HEADS):
            wcol = ikw8[:, IDX_HEAD_DIM + h:IDX_HEAD_DIM + h + 1]
            acc = acc + wcol * jnp.maximum(d[h * R:(h + 1) * R], 0.0)
        return acc * (IDX_HEAD_DIM ** -0.5)

    rows = pl.ds(pl.multiple_of(j * R, R), R)
    zeros64 = jnp.zeros((LANES - IDX_HEAD_DIM, PAGE), BF16)
    for p in range(n_pages):
        keys_t = jnp.concatenate([page_refs[p][0].astype(BF16), zeros64], axis=0)
        sc = index_scores(_dot(x_all, keys_t))
        if p == 0:
            sc = jnp.where(lane < N_META, jnp.inf, sc)
        stack_ref[rows, p * PAGE:(p + 1) * PAGE] = sc
    tmod = lax.broadcasted_iota(jnp.int32, (R, 1), 0) % T
    newk_ref[...] = jnp.zeros_like(newk_ref)
    newk_ref[0:T, :] = ikw_ref[0]
    nk = newk_ref[...]
    lane_k = lax.broadcasted_iota(jnp.int32, nk.shape, 1)
    sc_new = index_scores(_dot_nt(x_all, jnp.where(lane_k < IDX_HEAD_DIM, nk, 0.0).astype(BF16)))
    stack_ref[rows, past:n_keys] = jnp.where(lane <= tmod, sc_new, NEG_INF)

    @pl.when(j == SAMPLE_GROUP - 1)
    def _():
        st_ref[...] = stack_ref[...].T
        qlane = lax.broadcasted_iota(jnp.int32, (1, GR), 1)
        qpos = past + (qlane % R) % T
        target = jnp.minimum(qpos + 1, topk).astype(F32)
        NACC = 8

        def count_where(test):
            hit = jnp.where(test(st_ref[...]), 1.0, 0.0)
            part = jnp.sum(hit.reshape(n_keys // (NACC * SUBLANES), NACC, SUBLANES, GR), axis=0)
            return jnp.sum(jnp.sum(part, axis=0), axis=0, keepdims=True)

        def count_ge(thr):
            return count_where(lambda t: t >= thr)

        mag = jnp.abs(st_ref[...])
        mag = jnp.where(mag < jnp.inf, mag, 0.0)
        amax = jnp.max(jnp.max(mag.reshape(n_keys // SUBLANES, SUBLANES, GR), axis=0), axis=0, keepdims=True)
        def max_below(thr):
            t = st_ref[...]
            below = jnp.where(t < thr, t, NEG_INF).reshape(n_keys // (NACC * SUBLANES), NACC, SUBLANES, GR)
            return jnp.max(jnp.max(jnp.max(below, axis=0), axis=0), axis=0, keepdims=True)

        lo = _kth_largest_bounds(count_ge, max_below, target, amax)
        surplus = count_ge(lo) - target

        @pl.when(jnp.max(surplus) > 0.0)
        def _():
            need = target - count_where(lambda t: t > lo)
            rr = lax.broadcasted_iota(jnp.int32, (LANES, LANES), 0)
            cc = lax.broadcasted_iota(jnp.int32, (LANES, LANES), 1)
            tri = (cc <= rr).astype(BF16)
            carry = jnp.zeros((1, GR), F32)
            for kt in range(n_keys // LANES):
                t = st_ref[kt * LANES:(kt + 1) * LANES, :]
                tie = t == lo
                tie_f = jnp.where(tie, 1.0, 0.0)
                rank = carry + _dot(tri, tie_f.astype(BF16))
                st_ref[kt * LANES:(kt + 1) * LANES, :] = jnp.where(tie & (rank > need), NEG_INF, t)
                carry = carry + jnp.sum(tie_f, axis=0, keepdims=True)

        sel_ref[...] = st_ref[...].T
        lo_ref[...] = jnp.broadcast_to(lo, (GR, GR)).T


def _dsa_sample_select(page_table, iq, ikw, cache_ik, topk):
    nb, t, _ = iq.shape
    n_pages = page_table.shape[1]
    n_keys = n_pages * PAGE + LANES
    gr = SAMPLE_GROUP * 2 * t
    assert nb % SAMPLE_GROUP == 0 and gr == LANES
    tok = lambda width: pl.BlockSpec((1, t, width), lambda b, pt: (b, 0, 0))
    page = lambda p: pl.BlockSpec((1, IDX_HEAD_DIM, PAGE), lambda b, pt: (pt[b, p], 0, 0))
    grid_spec = pltpu.PrefetchScalarGridSpec(
        num_scalar_prefetch=1,
        grid=(nb,),
        in_specs=[tok(IDX_HEADS * IDX_HEAD_DIM), tok(LANES)] + [page(p) for p in range(n_pages)],
        out_specs=[pl.BlockSpec((gr, n_keys), lambda b, pt: (b // SAMPLE_GROUP, 0)),
                   pl.BlockSpec((gr, LANES), lambda b, pt: (b // SAMPLE_GROUP, 0))],
        scratch_shapes=[pltpu.VMEM((gr, n_keys), F32),
                        pltpu.VMEM((n_keys, gr), F32),
                        pltpu.VMEM((2 * t, IDX_HEADS * IDX_HEAD_DIM), F32),
                        pltpu.VMEM((2 * t, LANES), F32),
                        pltpu.VMEM((LANES, LANES), F32)])
    return pl.pallas_call(
        functools.partial(_dsa_sample_select_kernel, topk, n_pages),
        grid_spec=grid_spec,
        out_shape=[jax.ShapeDtypeStruct((nb * 2 * t, n_keys), F32),
                   jax.ShapeDtypeStruct((nb * 2 * t, LANES), F32)],
        compiler_params=_cparams(("arbitrary",)),
        name="dsa_sample_select",
    )(page_table, iq, ikw, *([cache_ik] * n_pages))


def _dsa_sample_attend_kernel(n_pages, pt_ref, q_ref, kn_ref, vn_ref, sel_ref, lo_ref, *rest):
    del pt_ref
    n_seq = q_ref.shape[0]
    pages = rest[:2 * n_seq * n_pages]
    o_ref, q8_ref, newk_ref, newv_ref = rest[2 * n_seq * n_pages:]
    R = 2 * q_ref.shape[1]
    for i in range(n_seq):
        one = pl.ds(i, 1)
        _dsa_sample_attend_seq(q_ref.at[one], kn_ref.at[one], vn_ref.at[one],
                               sel_ref.at[pl.ds(i * R, R)], lo_ref.at[pl.ds(i * R, R)],
                               pages[i * n_pages:(i + 1) * n_pages],
                               pages[(n_seq + i) * n_pages:(n_seq + i + 1) * n_pages],
                               o_ref.at[one], q8_ref.at[i], newk_ref.at[i], newv_ref.at[i])


def _dsa_sample_attend_seq(q_ref, kn_ref, vn_ref, sel_ref, lo_ref, k_refs, v_refs, o_ref,
                           q8_ref, newk_ref, newv_ref):
    T = q_ref.shape[1]
    R = 2 * T
    q8_ref[0:T, :] = q_ref[0].astype(F32)
    q8_ref[T:R, :] = q_ref[0].astype(F32)
    newk_ref[...] = jnp.zeros_like(newk_ref)
    newv_ref[...] = jnp.zeros_like(newv_ref)
    newk_ref[0:T, :] = kn_ref[0]
    newv_ref[0:T, :] = vn_ref[0]
    bias8 = jnp.where(sel_ref[...] >= lo_ref[:, 0:1], 0.0, NEG_INF)
    bias = jnp.concatenate([bias8] * ATT_GROUP, axis=0)
    for g in range(ATT_KV_HEADS):
        qg = jnp.concatenate([q8_ref[:, (g * ATT_GROUP + hq) * LANES:(g * ATT_GROUP + hq + 1) * LANES]
                              for hq in range(ATT_GROUP)], axis=0).astype(BF16)
        head_rows = pl.ds(g, PAGE, stride=ATT_KV_HEADS)
        k_all = jnp.concatenate([r[0, head_rows, :].astype(BF16) for r in k_refs]
                                + [newk_ref[:, g * LANES:(g + 1) * LANES].astype(BF16)], axis=0)
        v_all = jnp.concatenate([r[0, head_rows, :].astype(BF16) for r in v_refs]
                                + [newv_ref[:, g * LANES:(g + 1) * LANES].astype(BF16)], axis=0)
        sc = _dot_nt(qg, k_all) + bias
        m = jnp.max(sc, axis=1, keepdims=True)
        pr = jnp.exp(sc - m)
        out = _dot(pr.astype(BF16), v_all) / jnp.sum(pr, axis=1, keepdims=True)
        for hq in range(ATT_GROUP):
            h = g * ATT_GROUP + hq
            o_ref[0, :, h * LANES:(h + 1) * LANES] = out[hq * R:hq * R + T].astype(o_ref.dtype)


def _dsa_sample_attend(page_table, q, k_new, v_new, sel, lo, cache_k, cache_v):
    nb, t, _ = q.shape
    n_pages = page_table.shape[1]
    kv = ATT_KV_HEADS * ATT_HEAD_DIM
    n_keys = sel.shape[1]
    sq = SAMPLE_SEQS_PER_STEP
    assert nb % sq == 0
    tok = lambda width: pl.BlockSpec((sq, t, width), lambda b, pt: (b, 0, 0))
    page = lambda i, p: pl.BlockSpec((1, ATT_KV_HEADS * PAGE, ATT_HEAD_DIM),
                                     lambda b, pt: (pt[b * sq + i, p], 0, 0))
    pages = [page(i, p) for i in range(sq) for p in range(n_pages)]
    grid_spec = pltpu.PrefetchScalarGridSpec(
        num_scalar_prefetch=1,
        grid=(nb // sq,),
        in_specs=[tok(ATT_HEADS * ATT_HEAD_DIM), tok(kv), tok(kv),
                  pl.BlockSpec((sq * 2 * t, n_keys), lambda b, pt: (b, 0)),
                  pl.BlockSpec((sq * 2 * t, LANES), lambda b, pt: (b, 0))] + pages * 2,
        out_specs=tok(ATT_HEADS * ATT_HEAD_DIM),
        scratch_shapes=[pltpu.VMEM((sq, 2 * t, ATT_HEADS * ATT_HEAD_DIM), F32),
                        pltpu.VMEM((sq, LANES, kv), F32),
                        pltpu.VMEM((sq, LANES, kv), F32)])
    return pl.pallas_call(
        functools.partial(_dsa_sample_attend_kernel, n_pages),
        grid_spec=grid_spec,
        out_shape=jax.ShapeDtypeStruct((nb, t, ATT_HEADS * ATT_HEAD_DIM), BF16),
        compiler_params=_cparams(("parallel",)),
        name="dsa_sample_attend",
    )(page_table, q, k_new, v_new, sel, lo, *([cache_k] * (sq * n_pages)), *([cache_v] * (sq * n_pages)))


def kernel(x_prompt, x_sample, state_gdn, state_gdn_conv, cache_k, cache_v, cache_idx_k, page_table,
           meta_tokens, ln1_g, ln1_b, ln2_g, ln2_b, mlp_w1, mlp_w2,
           gdn_w_in, gdn_conv_w, gdn_a_log, gdn_dt_bias, gdn_norm_w, gdn_w_out,
           dsa_w_in, dsa_ik_norm_g, dsa_ik_norm_b, dsa_w_o):
    nb, seq, d = x_prompt.shape
    ns, ts, _ = x_sample.shape
    n_tok = N_META + seq
    lp = -(-n_tok // KEY_TILE) * KEY_TILE
    n_prompt_rows = nb * lp
    rows = n_prompt_rows + ns * ts
    assert rows % ROW_TILE == 0 and lp % GDN_CHUNK == 0 and lp % KEY_CHUNK == 0 and d == D_MODEL
    kvd = ATT_KV_HEADS * ATT_HEAD_DIM

    meta = meta_tokens.astype(x_prompt.dtype)
    pad = jnp.zeros((lp - n_tok, d), x_prompt.dtype)
    h = jnp.concatenate([piece for b in range(nb) for piece in (meta, x_prompt[b], pad)]
                        + [x_sample.reshape(ns * ts, d)], 0)

    def row2(x):
        return x.reshape(1, -1)

    def prompt_rows(x, width, first=0):
        rows_b = [x[b * lp + first:b * lp + n_tok, :width] for b in range(nb)]
        return jnp.concatenate(rows_b, 0).reshape(nb, n_tok - first, width)

    def with_sample_rows(a_prompt, a_sample):
        return lax.dynamic_update_slice(a_prompt, a_sample.reshape(ns * ts, -1), (n_prompt_rows, 0))

    w_in = gdn_w_in[0]
    split = GDN_CONV_DIM + GDN_VALUE_DIM
    n_gate = 2 * GDN_V_HEADS
    w_cat = jnp.concatenate([w_in, jnp.zeros((d, LANES - n_gate), w_in.dtype)], 1).astype(BF16)
    proj, ba = _gdn_inproj(h, w_cat, split)
    decay_lanes = lambda v: jnp.pad(row2(v), ((0, 0), (GDN_V_HEADS, LANES - n_gate)))
    a_p, gdn_state_prompt = _gdn_prompt(proj, ba, gdn_conv_w[0], decay_lanes(gdn_a_log[0]),
                                        decay_lanes(gdn_dt_bias[0]), row2(gdn_norm_w[0]), nb, lp, n_tok)
    proj_s = proj[n_prompt_rows:].reshape(ns, ts, split)
    mixed_s = proj_s[:, :, :GDN_CONV_DIM]
    a_s, gdn_state_sample = _gdn_sample(state_gdn_conv[0], mixed_s, proj_s[:, :, GDN_CONV_DIM:],
                                        ba[n_prompt_rows:, :n_gate].reshape(ns, ts, n_gate), gdn_conv_w[0],
                                        row2(gdn_a_log[0]), row2(gdn_dt_bias[0]), row2(gdn_norm_w[0]),
                                        state_gdn[0])
    def tail_weights(i, w_out):
        return (w_out.astype(BF16), row2(ln1_g[i]), row2(ln1_b[i]),
                mlp_w1[i].astype(BF16), mlp_w2[i].astype(BF16), row2(ln2_g[i]), row2(ln2_b[i]))

    h = _block_tail(with_sample_rows(a_p, a_s), h, tail_weights(0, gdn_w_out[0]))
    keep = GDN_CONV_WIDTH - 1
    gdn_conv_prompt = prompt_rows(proj, GDN_CONV_DIM, first=n_tok - keep)
    gdn_conv_sample = jnp.concatenate([state_gdn_conv[0], mixed_s], 1)[:, -keep:]

    past = page_table.shape[1] * PAGE
    pos = jnp.concatenate([jnp.tile(jnp.arange(lp, dtype=jnp.int32), nb),
                           jnp.tile(past + jnp.arange(ts, dtype=jnp.int32), ns)])
    tables = _rope_tables(pos, ATT_ROT_HALF, LANES) + _rope_tables(pos, IDX_ROT_HALF, IDX_HEAD_DIM)
    n_in = dsa_w_in.shape[2]
    w_dsa = jnp.pad(dsa_w_in[0], ((0, 0), (0, -n_in % LANES))).astype(BF16)
    pad_lanes = lambda v: row2(jnp.pad(v, (0, LANES - v.shape[0])))
    q, k, v, vt, iq, ikw, ik2 = _dsa_proj(h, w_dsa, tables, pad_lanes(dsa_ik_norm_g[0]),
                                          pad_lanes(dsa_ik_norm_b[0]))
    o_p = _dsa_prompt(q, iq, ikw, k.astype(BF16), vt, ik2, nb, lp, min(TOPK_MAX, (n_tok - N_META) // 4))
    smp = lambda x: x[n_prompt_rows:].reshape(ns, ts, -1)
    n_pool = cache_k.shape[1]
    sel, lo = _dsa_sample_select(page_table, smp(iq), smp(ikw), jnp.swapaxes(cache_idx_k[0], 1, 2),
                                 min(TOPK_MAX, (past + ts) // 4))
    rows_kh = lambda c: c[0].reshape(n_pool, PAGE * ATT_KV_HEADS, ATT_HEAD_DIM)
    o_s = _dsa_sample_attend(page_table, smp(q), smp(k), smp(v), sel, lo, rows_kh(cache_k), rows_kh(cache_v))
    last = tail_weights(1, dsa_w_o[0])
    y_prompt = _block_tail_sequences(o_p, h, last, nb, lp, N_META, seq)
    y_sample = _block_tail(o_s.reshape(ns * ts, -1), h[n_prompt_rows:], last).reshape(ns, ts, d)

    heads = lambda x: x.reshape(x.shape[:-1] + (ATT_KV_HEADS, ATT_HEAD_DIM))
    return (y_prompt, y_sample,
            gdn_state_prompt[None], gdn_conv_prompt[None], gdn_state_sample[None], gdn_conv_sample[None],
            heads(prompt_rows(k, kvd))[None], heads(prompt_rows(v, kvd))[None],
            prompt_rows(ikw, IDX_HEAD_DIM)[None],
            heads(smp(k))[None], heads(smp(v))[None], smp(ikw)[:, :, :IDX_HEAD_DIM][None])
```

```python
import functools

import jax
import jax.numpy as jnp
from jax import lax
from jax.experimental import pallas as pl
from jax.experimental.pallas import tpu as pltpu

F32 = jnp.float32
BF16 = jnp.bfloat16
HIGHEST = lax.Precision.HIGHEST

D_MODEL = 1024
N_META = 16
DEPTH = 2
LN_EPS = 1e-5
DEEPNORM_ALPHA = (2 * DEPTH) ** 0.25
GDN_K_HEADS = 8
GDN_V_HEADS = 16
GDN_HEAD = 128
GDN_KEY_DIM = GDN_K_HEADS * GDN_HEAD
GDN_VALUE_DIM = GDN_V_HEADS * GDN_HEAD
GDN_CONV_DIM = 2 * GDN_KEY_DIM + GDN_VALUE_DIM
GDN_CONV_WIDTH = 4
GDN_CHUNK = 128
L2_EPS = 1e-6
RMS_EPS = 1e-6
ATT_HEADS = 8
ATT_KV_HEADS = 2
ATT_HEAD_DIM = 128
ATT_GROUP = ATT_HEADS // ATT_KV_HEADS
IDX_HEADS = 8
IDX_HEAD_DIM = 64
TOPK_MAX = 256
ROPE_THETA = 500000.0
ATT_ROT_HALF = ATT_HEAD_DIM // 8
IDX_ROT_HALF = IDX_HEAD_DIM // 8
PAGE = 128

LANES = 128
SUBLANES = 8
ROW_TILE = 512
KEY_TILE = 128
KEY_CHUNK = 3 * KEY_TILE
VMEM_LIMIT = 56 * 1024 * 1024

NEG_INF = float("-inf")


def _cparams(sem):
    return pltpu.CompilerParams(dimension_semantics=sem, vmem_limit_bytes=VMEM_LIMIT)


def _dot(a, b):
    return jnp.dot(a, b, preferred_element_type=F32)


def _dot_nt(a, b, precision=None):
    return lax.dot_general(a, b, (((1,), (1,)), ((), ())), preferred_element_type=F32,
                           precision=precision)


def _dot_tn(a, b):
    return lax.dot_general(a, b, (((0,), (0,)), ((), ())), preferred_element_type=F32)


def _layernorm_rows(x, g, b):
    mu = jnp.mean(x, axis=-1, keepdims=True)
    xc = x - mu
    var = jnp.mean(xc * xc, axis=-1, keepdims=True)
    return xc * lax.rsqrt(var + LN_EPS) * g + b


def _sigmoid(x):
    return 0.5 * jnp.tanh(0.5 * x) + 0.5


def _silu(x):
    return x * _sigmoid(x)


def _softplus(x):
    return jnp.maximum(x, 0.0) + jnp.log(1.0 + jnp.exp(-jnp.abs(x)))


def _resident(shape):
    return pl.BlockSpec(shape, lambda *_: (0,) * len(shape), pipeline_mode=pl.Buffered(1))


def _gdn_inproj_kernel(x_ref, w_ref, proj_ref, ba_ref):
    x = x_ref[...].astype(BF16)
    n_main = proj_ref.shape[1]
    slab = 1024
    for j in range(n_main // slab):
        proj_ref[:, j * slab:(j + 1) * slab] = _dot(x, w_ref[:, j * slab:(j + 1) * slab])
    ba_ref[...] = _dot(x, w_ref[:, n_main:])


def _gdn_inproj(x, w, n_main):
    rows, k = x.shape
    n = w.shape[1]
    return pl.pallas_call(
        _gdn_inproj_kernel,
        grid=(rows // ROW_TILE,),
        in_specs=[pl.BlockSpec((ROW_TILE, k), lambda i: (i, 0)), _resident((k, n))],
        out_specs=[pl.BlockSpec((ROW_TILE, n_main), lambda i: (i, 0)),
                   pl.BlockSpec((ROW_TILE, n - n_main), lambda i: (i, 0))],
        out_shape=[jax.ShapeDtypeStruct((rows, n_main), F32),
                   jax.ShapeDtypeStruct((rows, n - n_main), F32)],
        compiler_params=_cparams(("parallel",)),
        name="gdn_inproj",
    )(x, w)


def _block_tail_kernel(a_ref, x_ref, wo_ref, g1_ref, b1_ref, w1_ref, w2_ref, g2_ref, b2_ref, o_ref):
    x = _layernorm_rows(DEEPNORM_ALPHA * x_ref[...] + _dot(a_ref[...], wo_ref[...]), g1_ref[...], b1_ref[...])
    t = jnp.maximum(_dot(x.astype(BF16), w1_ref[...]), 0.0)
    y = DEEPNORM_ALPHA * x + _dot((t * t).astype(BF16), w2_ref[...])
    o_ref[...] = _layernorm_rows(y, g2_ref[...], b2_ref[...])


def _tail_weight_specs(k, d, f):
    return [_resident((k, d)), _resident((1, d)), _resident((1, d)),
            _resident((d, f)), _resident((f, d)), _resident((1, d)), _resident((1, d))]


def _block_tail(a, x, weights):
    rows, k = a.shape
    d = x.shape[1]
    f = weights[3].shape[1]
    return pl.pallas_call(
        _block_tail_kernel,
        grid=(rows // ROW_TILE,),
        in_specs=[pl.BlockSpec((ROW_TILE, k), lambda i: (i, 0)),
                  pl.BlockSpec((ROW_TILE, d), lambda i: (i, 0))] + _tail_weight_specs(k, d, f),
        out_specs=pl.BlockSpec((ROW_TILE, d), lambda i: (i, 0)),
        out_shape=jax.ShapeDtypeStruct((rows, d), F32),
        compiler_params=_cparams(("parallel",)),
        name="block_tail",
    )(a, x, *weights)


def _block_tail_sequences(a, x, weights, n_seq, stride, first, n_out):
    k = a.shape[1]
    d = x.shape[1]
    f = weights[3].shape[1]
    align = 2 * SUBLANES
    assert n_out % ROW_TILE == 0 and (stride % align, first % align) == (0, 0)

    def kern(*refs):
        _block_tail_kernel(*refs[:-1], refs[-1].at[0])

    def window(width):
        return pl.BlockSpec((pl.Element(ROW_TILE), pl.Element(width)),
                            lambda s, t: (pl.multiple_of(s * stride + first + t * ROW_TILE, align), 0))

    return pl.pallas_call(
        kern,
        grid=(n_seq, n_out // ROW_TILE),
        in_specs=[window(k), window(d)] + _tail_weight_specs(k, d, f),
        out_specs=pl.BlockSpec((1, ROW_TILE, d), lambda s, t: (s, t, 0)),
        out_shape=jax.ShapeDtypeStruct((n_seq, n_out, d), F32),
        compiler_params=_cparams(("parallel", "parallel")),
        name="block_tail_sequences",
    )(a, x, *weights)


def _unit_lower_inverse(a, n_factors):
    c = a.shape[0]
    row = lax.broadcasted_iota(jnp.int32, (c, c), 0)
    col = lax.broadcasted_iota(jnp.int32, (c, c), 1)
    eye = (row == col).astype(F32)
    p = -a
    t = eye + p
    for _ in range(n_factors - 1):
        pb = p.astype(BF16)
        p = _dot(pb, pb)
        t = t + _dot(t.astype(BF16), p.astype(BF16))
    return t


def _bdot(a, b):
    return lax.dot_general(a, b, (((2,), (1,)), ((0,), (0,))), preferred_element_type=F32)


def _bdot_nt(a, b):
    return lax.dot_general(a, b, (((2,), (2,)), ((0,), (0,))), preferred_element_type=F32)


def _bdot_tn(a, b):
    return lax.dot_general(a, b, (((1,), (1,)), ((0,), (0,))), preferred_element_type=F32)


def _unit_lower_inverse_batched(a):
    c = a.shape[-1]
    row = lax.broadcasted_iota(jnp.int32, (c, c), 0)
    col = lax.broadcasted_iota(jnp.int32, (c, c), 1)
    def coupling(s):
        couples = ((row // (2 * s)) == (col // (2 * s))) & ((row // s) != (col // s))
        return jnp.where(couples, a, 0.0)

    t = (row == col).astype(F32) - coupling(1)
    s = 2
    while s < c:
        tb = t.astype(BF16)
        t = t - _bdot(_bdot(tb, coupling(s).astype(BF16)).astype(BF16), tb)
        s *= 2
    return t


def _pairwise_diff(col):
    c = col.shape[0]
    lane = lax.broadcasted_iota(jnp.int32, (c, LANES), 1)
    left = jnp.where(lane == 0, col, jnp.where(lane == 1, 1.0, 0.0))
    right = jnp.where(lane == 0, 1.0, jnp.where(lane == 1, -col, 0.0))
    return _dot_nt(left, right, precision=HIGHEST)


def _l2norm_rows(x):
    return x * lax.rsqrt(jnp.sum(x * x, axis=-1, keepdims=True) + L2_EPS)


def _gated_rmsnorm(o, z, norm_w):
    on = o * lax.rsqrt(jnp.mean(o * o, axis=-1, keepdims=True) + RMS_EPS) * norm_w
    return on * _silu(z)


def _gdn_gates(ba, a_log, dt_bias):
    beta = _sigmoid(ba[:, :GDN_V_HEADS])
    g = -jnp.exp(a_log) * _softplus(ba[:, GDN_V_HEADS:] + dt_bias)
    return beta, g


def _gdn_prompt_kernel(n_tokens, mixed_ref, z_ref, ba_ref, convw_ref, alog_ref, dtb_ref, normw_ref,
                       o_ref, sfin_ref, xc_ref, s_ref):
    c = pl.program_id(1)
    C = GDN_CHUNK
    W = GDN_CONV_WIDTH

    @pl.when(c == 0)
    def _():
        xc_ref[...] = jnp.zeros_like(xc_ref)
        s_ref[...] = jnp.zeros_like(s_ref)

    x = mixed_ref[...]
    tail = xc_ref[...]
    w = convw_ref[...]
    row8 = lax.broadcasted_iota(jnp.int32, (SUBLANES, 1), 0)
    acc = x * w[W - 1:W, :]
    for s in range(1, W):
        shifted = pltpu.roll(x, s, axis=0)
        head = jnp.where(row8 < s, pltpu.roll(tail, s, axis=0), shifted[:SUBLANES])
        acc = acc + jnp.concatenate([head, shifted[SUBLANES:]], axis=0) * w[W - 1 - s:W - s, :]
    qkv = _silu(acc)
    xc_ref[...] = x[C - SUBLANES:]

    row = lax.broadcasted_iota(jnp.int32, (C, 1), 0)
    valid = (c * C + row) < n_tokens
    ba = ba_ref[...]
    beta = jnp.where(valid, _sigmoid(ba), 0.0)
    g = jnp.where(valid, -jnp.exp(alog_ref[...]) * _softplus(ba + dtb_ref[...]), 0.0)

    ri = lax.broadcasted_iota(jnp.int32, (C, C), 0)
    ci = lax.broadcasted_iota(jnp.int32, (C, C), 1)
    tril = ci <= ri
    strict = ci < ri
    gc = jnp.dot(tril.astype(F32), g, preferred_element_type=F32, precision=HIGHEST)
    gc_rows = gc.T
    egc = jnp.exp(gc)
    glast = gc[C - 1:C, :]
    ekd = jnp.exp(glast - gc)
    eglast = jnp.exp(glast)
    normw = normw_ref[...]
    NH = GDN_V_HEADS

    def gcol(x, h):
        return x[:, NH + h:NH + h + 1]

    qk_raw = [qkv[:, i * GDN_HEAD:(i + 1) * GDN_HEAD] for i in range(2 * GDN_K_HEADS)]
    sq = jnp.concatenate([t * t for t in qk_raw], axis=0)
    sq_hi = sq.astype(BF16)
    sq_lo = (sq - sq_hi.astype(F32)).astype(BF16)
    ones = jnp.ones((GDN_HEAD, GDN_HEAD), BF16)
    inv_norm = lax.rsqrt(_dot(sq_hi, ones) + _dot(sq_lo, ones) + L2_EPS)
    qk_n = [t * inv_norm[i * C:(i + 1) * C] for i, t in enumerate(qk_raw)]
    q_l = [t * (GDN_HEAD ** -0.5) for t in qk_n[:GDN_K_HEADS]]
    k_l = qk_n[GDN_K_HEADS:]
    k8 = jnp.stack(k_l).astype(BF16)
    kk8 = _bdot_nt(k8, k8)
    qk8 = _bdot_nt(jnp.stack(q_l).astype(BF16), k8)
    a_l, qkd_l, rhs_l, qg_l, kd_l = [], [], [], [], []
    for h in range(NH):
        kh = h // 2
        v = qkv[:, 2 * GDN_KEY_DIM + h * GDN_HEAD:2 * GDN_KEY_DIM + (h + 1) * GDN_HEAD]
        bcol = beta[:, h:h + 1]
        diff = gcol(gc, h) - gc_rows[NH + h:NH + h + 1, :]
        decay = jnp.where(tril, jnp.exp(jnp.where(tril, diff, 0.0)), 0.0)
        a_l.append(jnp.where(strict, kk8[kh] * decay * bcol, 0.0))
        qkd_l.append((qk8[kh] * decay).astype(BF16))
        rhs_l.append(jnp.concatenate([v * bcol, k_l[kh] * (bcol * gcol(egc, h))], axis=1).astype(BF16))
        qg_l.append(q_l[kh] * gcol(egc, h))
        kd_l.append((k_l[kh] * gcol(ekd, h)).astype(BF16))
    t_inv = _unit_lower_inverse_batched(jnp.stack(a_l)).astype(BF16)
    sol = _bdot(t_inv, jnp.stack(rhs_l))
    u, wm = sol[:, :, :GDN_HEAD], sol[:, :, GDN_HEAD:]
    s_old = s_ref[...]
    lhs = jnp.concatenate([wm, jnp.stack(qg_l)], axis=1).astype(BF16)
    ws_qs = _bdot(lhs, s_old.astype(BF16))
    v_new = u - ws_qs[:, :C]
    vnb = v_new.astype(BF16)
    o = ws_qs[:, C:] + _bdot(jnp.stack(qkd_l), vnb)
    upd = _bdot_tn(jnp.stack(kd_l), vnb)
    for h in range(NH):
        s_ref[h] = s_old[h] * gcol(eglast, h) + upd[h]
        zh = z_ref[:, h * GDN_HEAD:(h + 1) * GDN_HEAD]
        o_ref[:, h * GDN_HEAD:(h + 1) * GDN_HEAD] = _gated_rmsnorm(o[h], zh, normw).astype(o_ref.dtype)

    @pl.when(c == pl.num_programs(1) - 1)
    def _():
        sfin_ref[0] = s_ref[...]


def _gdn_prompt(proj, ba, conv_w, a_log, dt_bias, norm_w, n_batch, lp, n_tokens):
    C = GDN_CHUNK
    per_b = lp // C
    n_chunks = per_b
    z_off = GDN_CONV_DIM // GDN_VALUE_DIM
    kern = functools.partial(_gdn_prompt_kernel, n_tokens)
    return pl.pallas_call(
        kern,
        grid=(n_batch, n_chunks),
        in_specs=[pl.BlockSpec((C, GDN_CONV_DIM), lambda b, c: (b * per_b + c, 0)),
                  pl.BlockSpec((C, GDN_VALUE_DIM), lambda b, c: (b * per_b + c, z_off)),
                  pl.BlockSpec((C, LANES), lambda b, c: (b * per_b + c, 0)),
                  pl.BlockSpec((GDN_CONV_WIDTH, GDN_CONV_DIM), lambda b, c: (0, 0)),
                  pl.BlockSpec((1, LANES), lambda b, c: (0, 0)),
                  pl.BlockSpec((1, LANES), lambda b, c: (0, 0)),
                  pl.BlockSpec((1, GDN_HEAD), lambda b, c: (0, 0))],
        out_specs=[pl.BlockSpec((C, GDN_VALUE_DIM), lambda b, c: (b * per_b + c, 0)),
                   pl.BlockSpec((1, GDN_V_HEADS, GDN_HEAD, GDN_HEAD), lambda b, c: (b, 0, 0, 0))],
        out_shape=[jax.ShapeDtypeStruct((proj.shape[0], GDN_VALUE_DIM), BF16),
                   jax.ShapeDtypeStruct((n_batch, GDN_V_HEADS, GDN_HEAD, GDN_HEAD), F32)],
        scratch_shapes=[pltpu.VMEM((SUBLANES, GDN_CONV_DIM), F32),
                        pltpu.VMEM((GDN_V_HEADS, GDN_HEAD, GDN_HEAD), F32)],
        compiler_params=_cparams(("parallel", "arbitrary")),
        name="gdn_prompt",
    )(proj, proj, ba, conv_w, a_log, dt_bias, norm_w)


def _twice(dst_ref, x):
    t = x.shape[0]
    dst_ref[0:t, :] = x
    dst_ref[t:2 * t, :] = x
    return dst_ref[...]


SAMPLE_SEQS_PER_STEP = 4


def _gdn_sample_kernel(cs_ref, mixed_ref, z_ref, ba_ref, convw_ref, alog_ref, dtb_ref, normw_ref, s_ref,
                       o_ref, snew_ref, xc_ref, dq_ref, dz_ref, dba_ref):
    for i in range(mixed_ref.shape[0]):
        one = pl.ds(i, 1)
        _gdn_sample_seq(cs_ref.at[one], mixed_ref.at[one], z_ref.at[one], ba_ref.at[one], convw_ref, alog_ref,
                        dtb_ref, normw_ref, s_ref.at[one], o_ref.at[one], snew_ref.at[one],
                        xc_ref.at[i], dq_ref.at[i], dz_ref.at[i], dba_ref.at[i])


def _gdn_sample_seq(cs_ref, mixed_ref, z_ref, ba_ref, convw_ref, alog_ref, dtb_ref, normw_ref, s_ref,
                    o_ref, snew_ref, xc_ref, dq_ref, dz_ref, dba_ref):
    T = mixed_ref.shape[1]
    R = 2 * T
    W = GDN_CONV_WIDTH
    xc_ref[0:W - 1, :] = cs_ref[0]
    xc_ref[W - 1:W - 1 + T, :] = mixed_ref[0]
    w = convw_ref[...]
    acc = xc_ref[0:T, :] * w[0:1, :]
    for j in range(1, W):
        acc = acc + xc_ref[j:j + T, :] * w[j:j + 1, :]
    qkv = _twice(dq_ref, _silu(acc))
    z2 = _twice(dz_ref, z_ref[0])
    ba2 = _twice(dba_ref, ba_ref[0])
    beta, g = _gdn_gates(ba2, alog_ref[...], dtb_ref[...])

    rr = lax.broadcasted_iota(jnp.int32, (R, 1), 0)
    tmod = rr % T
    first = rr < T
    gc = jnp.zeros_like(g)
    for s in range(T):
        gc = gc + jnp.where(tmod >= s, g[s:s + 1, :], 0.0)
    glast = gc[T - 1:T, :]
    normw = normw_ref[...]

    NP = GDN_K_HEADS
    M = NP * R

    def pair_col(x, kh):
        return jnp.where(first, x[:, 2 * kh:2 * kh + 1], x[:, 2 * kh + 1:2 * kh + 2])

    def pair_tile(x, base, kh):
        a = x[:, base + (2 * kh) * GDN_HEAD:base + (2 * kh + 1) * GDN_HEAD]
        b = x[:, base + (2 * kh + 1) * GDN_HEAD:base + (2 * kh + 2) * GDN_HEAD]
        return jnp.where(first, a, b)

    q_t, k_t, v_t, z_t, b_c, gc_c, gl_c = [], [], [], [], [], [], []
    for kh in range(NP):
        q_t.append(_l2norm_rows(qkv[:, kh * GDN_HEAD:(kh + 1) * GDN_HEAD]) * (GDN_HEAD ** -0.5))
        k_t.append(_l2norm_rows(qkv[:, GDN_KEY_DIM + kh * GDN_HEAD:GDN_KEY_DIM + (kh + 1) * GDN_HEAD]))
        v_t.append(pair_tile(qkv, 2 * GDN_KEY_DIM, kh))
        z_t.append(pair_tile(z2, 0, kh))
        b_c.append(pair_col(beta, kh))
        gc_c.append(pair_col(gc, kh))
        gl_c.append(pair_col(jnp.broadcast_to(glast, (R, GDN_V_HEADS)), kh))
    q64 = jnp.concatenate(q_t, axis=0)
    k64 = jnp.concatenate(k_t, axis=0)
    v64 = jnp.concatenate(v_t, axis=0)
    bcol = jnp.concatenate(b_c, axis=0)
    gcol = jnp.concatenate(gc_c, axis=0)
    glcol = jnp.concatenate(gl_c, axis=0)

    ri = lax.broadcasted_iota(jnp.int32, (M, M), 0)
    ci = lax.broadcasted_iota(jnp.int32, (M, M), 1)
    same = (ri // T) == (ci // T)
    tril = same & ((ci % T) <= (ri % T))
    strict = same & ((ci % T) < (ri % T))
    kb = k64.astype(BF16)
    kk = _dot_nt(kb, kb)
    qk0 = _dot_nt(q64.astype(BF16), kb)
    diff = _pairwise_diff(gcol)
    decay = jnp.where(tril, jnp.exp(jnp.where(tril, diff, 0.0)), 0.0)
    a_mat = jnp.where(strict, kk * decay * bcol, 0.0)
    t_inv = _unit_lower_inverse(a_mat, max(1, (T - 1).bit_length())).astype(BF16)
    egc = jnp.exp(gcol)
    u = _dot(t_inv, (v64 * bcol).astype(BF16))
    wm = _dot(t_inv, (k64 * (bcol * egc)).astype(BF16))
    qg = q64 * egc
    kd = k64 * jnp.exp(glcol - gcol)
    qkd = (qk0 * decay).astype(BF16)

    v_new_t, qs_t = [], []
    for kh in range(NP):
        lhs = jnp.concatenate([wm[kh * R:(kh + 1) * R], qg[kh * R:(kh + 1) * R]], axis=0).astype(BF16)
        r0 = _dot(lhs, s_ref[0, 2 * kh].astype(BF16))
        r1 = _dot(lhs, s_ref[0, 2 * kh + 1].astype(BF16))
        ws = jnp.where(first, r0[:R], r1[:R])
        qs_t.append(jnp.where(first, r0[R:], r1[R:]))
        v_new_t.append(u[kh * R:(kh + 1) * R] - ws)
    v_new = jnp.concatenate(v_new_t, axis=0)
    o64 = jnp.concatenate(qs_t, axis=0) + _dot(qkd, v_new.astype(BF16))
    on = _gated_rmsnorm(o64, jnp.concatenate(z_t, axis=0), normw)

    for kh in range(NP):
        vn = v_new_t[kh].astype(BF16)
        kdp = kd[kh * R:(kh + 1) * R]
        for j in range(2):
            h = 2 * kh + j
            keep = first if j == 0 else jnp.logical_not(first)
            kdm = jnp.where(keep, kdp, 0.0).astype(BF16)
            eg = jnp.exp(glast[:, h:h + 1])
            snew_ref[0, h] = s_ref[0, h] * eg + _dot_tn(kdm, vn)
            tile = on[kh * R:(kh + 1) * R]
            if j == 1:
                tile = pltpu.roll(tile, T, axis=0)
            o_ref[0, :, h * GDN_HEAD:(h + 1) * GDN_HEAD] = tile[:T].astype(o_ref.dtype)


def _gdn_sample(conv_state, mixed, z, ba, conv_w, a_log, dt_bias, norm_w, state):
    nb, t, _ = mixed.shape
    sq = SAMPLE_SEQS_PER_STEP
    assert nb % sq == 0
    seq = lambda *tail: pl.BlockSpec((sq,) + tail, lambda b: (b,) + (0,) * len(tail))
    return pl.pallas_call(
        _gdn_sample_kernel,
        grid=(nb // sq,),
        in_specs=[seq(GDN_CONV_WIDTH - 1, GDN_CONV_DIM), seq(t, GDN_CONV_DIM), seq(t, GDN_VALUE_DIM),
                  seq(t, 2 * GDN_V_HEADS),
                  _resident((GDN_CONV_WIDTH, GDN_CONV_DIM)), _resident((1, GDN_V_HEADS)),
                  _resident((1, GDN_V_HEADS)), _resident((1, GDN_HEAD)),
                  seq(GDN_V_HEADS, GDN_HEAD, GDN_HEAD)],
        out_specs=[seq(t, GDN_VALUE_DIM), seq(GDN_V_HEADS, GDN_HEAD, GDN_HEAD)],
        out_shape=[jax.ShapeDtypeStruct((nb, t, GDN_VALUE_DIM), BF16),
                   jax.ShapeDtypeStruct(state.shape, F32)],
        scratch_shapes=[pltpu.VMEM((sq, GDN_CONV_WIDTH - 1 + t, GDN_CONV_DIM), F32),
                        pltpu.VMEM((sq, 2 * t, GDN_CONV_DIM), F32),
                        pltpu.VMEM((sq, 2 * t, GDN_VALUE_DIM), F32),
                        pltpu.VMEM((sq, 2 * t, 2 * GDN_V_HEADS), F32)],
        compiler_params=_cparams(("parallel",)),
        name="gdn_sample",
    )(conv_state, mixed, z, ba, conv_w, a_log, dt_bias, norm_w, state)


def _rope_tables(pos, half, period):
    inv_freq = ROPE_THETA ** (-jnp.arange(half, dtype=F32) * 2.0 / (2 * half))
    ang = pos.astype(F32)[:, None] * inv_freq[None, :]
    cos, sin = jnp.cos(ang), jnp.sin(ang)
    ones = jnp.ones((pos.shape[0], period - 2 * half), F32)
    cos_p = jnp.concatenate([cos, cos, ones], axis=1)
    sin_p = jnp.concatenate([-sin, sin, 0.0 * ones], axis=1)
    reps = LANES // period
    return jnp.tile(cos_p, (1, reps)), jnp.tile(sin_p, (1, reps))


def _rope_tile(x, cos, sin, half, period):
    lane = lax.broadcasted_iota(jnp.int32, x.shape, 1) % period
    partner = jnp.where(lane < half, pltpu.roll(x, LANES - half, axis=1), pltpu.roll(x, half, axis=1))
    return x * cos + partner * sin


def _dsa_proj_kernel(x_ref, w_ref, ca_ref, sa_ref, ci_ref, si_ref, g_ref, b_ref,
                     q_ref, k_ref, v_ref, vt_ref, iq_ref, ikw_ref, ik2_ref):
    x = x_ref[...].astype(BF16)
    ca, sa, ci, si = ca_ref[...], sa_ref[...], ci_ref[...], si_ref[...]
    q_off, k_off = 0, ATT_HEADS * ATT_HEAD_DIM
    v_off = k_off + ATT_KV_HEADS * ATT_HEAD_DIM
    iq_off = v_off + ATT_KV_HEADS * ATT_HEAD_DIM
    ik_off = iq_off + IDX_HEADS * IDX_HEAD_DIM

    def proj(off):
        return _dot(x, w_ref[:, off:off + LANES])

    for h in range(ATT_HEADS):
        t = _rope_tile(proj(q_off + h * LANES), ca, sa, ATT_ROT_HALF, LANES)
        q_ref[:, h * LANES:(h + 1) * LANES] = (t * (ATT_HEAD_DIM ** -0.5)).astype(q_ref.dtype)
    for h in range(ATT_KV_HEADS):
        k_ref[:, h * LANES:(h + 1) * LANES] = _rope_tile(proj(k_off + h * LANES), ca, sa, ATT_ROT_HALF, LANES)
        v = proj(v_off + h * LANES)
        v_ref[:, h * LANES:(h + 1) * LANES] = v
        vt_ref[h * LANES:(h + 1) * LANES, :] = v.T.astype(vt_ref.dtype)
    for h in range(IDX_HEADS * IDX_HEAD_DIM // LANES):
        t = _rope_tile(proj(iq_off + h * LANES), ci, si, IDX_ROT_HALF, IDX_HEAD_DIM)
        iq_ref[:, h * LANES:(h + 1) * LANES] = t.astype(iq_ref.dtype)
    t = proj(ik_off)
    lane = lax.broadcasted_iota(jnp.int32, t.shape, 1)
    is_ik = lane < IDX_HEAD_DIM
    mu = jnp.sum(jnp.where(is_ik, t, 0.0), axis=-1, keepdims=True) / IDX_HEAD_DIM
    tc = jnp.where(is_ik, t - mu, 0.0)
    var = jnp.sum(tc * tc, axis=-1, keepdims=True) / IDX_HEAD_DIM
    ik = _rope_tile(tc * lax.rsqrt(var + LN_EPS) * g_ref[...] + b_ref[...], ci, si, IDX_ROT_HALF, IDX_HEAD_DIM)
    ikw = jnp.where(is_ik, ik, t * (IDX_HEADS ** -0.5))
    ikw_ref[...] = ikw
    ik_only = jnp.where(is_ik, ik, 0.0)
    ik2_ref[...] = (ik_only + pltpu.roll(ik_only, IDX_HEAD_DIM, axis=1)).astype(ik2_ref.dtype)


def _dsa_proj(x, w, tables, ik_g, ik_b):
    rows, d = x.shape
    n = w.shape[1]
    kv = ATT_KV_HEADS * ATT_HEAD_DIM
    row_spec = lambda width: pl.BlockSpec((ROW_TILE, width), lambda i: (i, 0))
    const_spec = _resident
    return pl.pallas_call(
        _dsa_proj_kernel,
        grid=(rows // ROW_TILE,),
        in_specs=[row_spec(d), const_spec((d, n))] + [row_spec(LANES)] * 4 + [const_spec((1, LANES))] * 2,
        out_specs=[row_spec(ATT_HEADS * ATT_HEAD_DIM), row_spec(kv), row_spec(kv),
                   pl.BlockSpec((kv, ROW_TILE), lambda i: (0, i)),
                   row_spec(IDX_HEADS * IDX_HEAD_DIM), row_spec(LANES), row_spec(LANES)],
        out_shape=[jax.ShapeDtypeStruct((rows, ATT_HEADS * ATT_HEAD_DIM), BF16),
                   jax.ShapeDtypeStruct((rows, kv), F32),
                   jax.ShapeDtypeStruct((rows, kv), F32),
                   jax.ShapeDtypeStruct((kv, rows), BF16),
                   jax.ShapeDtypeStruct((rows, IDX_HEADS * IDX_HEAD_DIM), BF16),
                   jax.ShapeDtypeStruct((rows, LANES), F32),
                   jax.ShapeDtypeStruct((rows, LANES), BF16)],
        compiler_params=_cparams(("parallel",)),
        name="dsa_proj",
    )(x, w, *tables, ik_g, ik_b)


def _float_order_key(x):
    b = pltpu.bitcast(x, jnp.int32)
    return b ^ ((b >> 31) & jnp.int32(0x7FFFFFFF))


def _float_from_key(k):
    return pltpu.bitcast(k ^ ((k >> 31) & jnp.int32(0x7FFFFFFF)), F32)


VALUE_STEPS = 8
VALUE_ROUNDS = 2
KEY_STEPS = 4
KEY_ROUNDS = 32 // KEY_STEPS
F32_MAX = 3.4028234663852886e38


def _kth_largest_bounds(count_ge, max_below, target, amax):
    bound = jnp.minimum(2.0 * amax, F32_MAX)
    lo = _float_order_key(-bound)
    hi = _float_order_key(bound) + 1
    c_lo = jnp.full(target.shape, -1.0, F32)

    def step(carry, split_values):
        lo, hi, c_lo, lo_val = carry
        mid = (lo >> 1) + (hi >> 1) + (lo & hi & 1)
        if split_values:
            vmid = _float_order_key(_float_from_key(lo) * 0.5 + _float_from_key(hi) * 0.5)
            mid = jnp.where((vmid > lo) & (vmid < hi), vmid, mid)
        mid_val = _float_from_key(mid)
        cnt = count_ge(mid_val)
        ok = cnt >= target
        return (jnp.where(ok, mid, lo), jnp.where(ok, hi, mid), jnp.where(ok, cnt, c_lo),
                jnp.where(ok & (mid != lo), mid_val, lo_val))

    def pending(carry):
        lo, hi, c_lo, _ = carry
        return (c_lo != target) & (hi != lo + 1)

    def any_pending(carry):
        return jnp.max(jnp.where(pending(carry), 1.0, 0.0)) > 0.0

    def rounds(carry, n_rounds, n_steps, split_values):
        def body(state):
            it, carry = state
            for _ in range(n_steps):
                carry = step(carry, split_values)
            return it + 1, carry

        return lax.while_loop(lambda st: (st[0] < n_rounds) & any_pending(st[1]), body, (jnp.int32(0), carry))[1]

    def snap(carry):
        lo, hi, c_lo, lo_val = carry
        v = max_below(_float_from_key(hi))
        cnt = count_ge(v)
        hit = pending(carry) & (cnt >= target)
        vk = _float_order_key(jnp.where(v == 0.0, 0.0, v))
        return (jnp.where(hit, vk, lo), jnp.where(hit, vk + 1, hi), jnp.where(hit, cnt, c_lo),
                jnp.where(hit, v, lo_val))

    def snap_round(state):
        it, carry = state
        return it + 1, rounds(snap(carry), 1, KEY_STEPS, False)

    carry = rounds((lo, hi, c_lo, -bound), VALUE_ROUNDS, VALUE_STEPS, True)
    carry = lax.while_loop(lambda st: (st[0] < KEY_ROUNDS) & any_pending(st[1]), snap_round,
                           (jnp.int32(0), carry))[1]
    return carry[3]


def _dsa_prompt_kernel(topk, q_ref, iq_ref, ikw_ref, k_ref, vt_ref, ik2_ref, o_ref,
                       s_ref, xh_ref, qg_ref, sc_ref, acc_ref):
    i = pl.program_id(1)
    KT = KEY_TILE
    CH = KEY_CHUNK
    nc = i // (CH // KT) + 1
    row = lax.broadcasted_iota(jnp.int32, (CH, KT), 0)
    lane = lax.broadcasted_iota(jnp.int32, (KT, KT), 1)
    qpos = i * KT + lax.broadcasted_iota(jnp.int32, (1, KT), 1)
    GQ = ATT_GROUP * KT

    def chunk(c):
        return pl.ds(pl.multiple_of(c * CH, CH), CH)

    NACC = 8

    def fold_rows(x):
        return x.reshape(CH // (NACC * SUBLANES), NACC, SUBLANES, x.shape[-1])

    def unfold(x, op):
        return op(op(x, axis=0), axis=0, keepdims=True)

    for h in range(IDX_HEADS):
        tile = iq_ref[:, (h // 2) * LANES:(h // 2 + 1) * LANES]
        mine = (lane // IDX_HEAD_DIM) == (h % 2)
        xh_ref[h * KT:(h + 1) * KT, :] = jnp.where(mine, tile, jnp.zeros_like(tile))
    for g in range(ATT_KV_HEADS):
        for hq in range(ATT_GROUP):
            h = g * ATT_GROUP + hq
            qg_ref[g, hq * KT:(hq + 1) * KT, :] = q_ref[:, h * LANES:(h + 1) * LANES]
    w_rows = ikw_ref[...].T[IDX_HEAD_DIM:IDX_HEAD_DIM + IDX_HEADS, :]

    n_pairs = (nc + 1) // 2

    def pair(j):
        return 2 * j, jnp.minimum(2 * j + 1, nc - 1)

    def score_chunk(c, amax):
        d = _dot_nt(ik2_ref[chunk(c), :], xh_ref[...])
        acc = w_rows[0:1, :] * jnp.maximum(d[:, 0:KT], 0.0)
        for h in range(1, IDX_HEADS):
            acc = acc + w_rows[h:h + 1, :] * jnp.maximum(d[:, h * KT:(h + 1) * KT], 0.0)
        acc = acc * (IDX_HEAD_DIM ** -0.5)
        kpos = c * CH + row
        causal = kpos <= qpos
        meta = kpos < N_META
        s_ref[chunk(c), :] = jnp.where(causal, jnp.where(meta, jnp.inf, acc), NEG_INF)
        finite = jnp.where(causal & jnp.logical_not(meta), jnp.abs(acc), 0.0)
        return jnp.maximum(amax, jnp.max(fold_rows(finite), axis=0))

    def score_pair(j, amax):
        c0, c1 = pair(j)
        return score_chunk(c1, score_chunk(c0, amax))

    amax = unfold(lax.fori_loop(0, n_pairs, score_pair, jnp.zeros((NACC, SUBLANES, KT), F32)), jnp.max)

    target = jnp.minimum(qpos + 1, topk).astype(F32)

    def count_where(test):
        def body(c, acc):
            hit = jnp.where(test(s_ref[chunk(c), :]), 1.0, 0.0)
            return acc + jnp.sum(fold_rows(hit), axis=0)
        return unfold(lax.fori_loop(0, nc, body, jnp.zeros((NACC, SUBLANES, KT), F32)), jnp.sum)

    def count_ge(thr):
        return count_where(lambda t: t >= thr)

    def max_below(thr):
        def body(c, acc):
            t = s_ref[chunk(c), :]
            return jnp.maximum(acc, jnp.max(fold_rows(jnp.where(t < thr, t, NEG_INF)), axis=0))
        return unfold(lax.fori_loop(0, nc, body, jnp.full((NACC, SUBLANES, KT), NEG_INF, F32)), jnp.max)

    lo = _kth_largest_bounds(count_ge, max_below, target, amax)
    surplus = count_ge(lo) - target

    @pl.when(jnp.max(surplus) > 0.0)
    def _():
        need = target - count_where(lambda t: t > lo)
        rr = lax.broadcasted_iota(jnp.int32, (CH, CH), 0)
        cc = lax.broadcasted_iota(jnp.int32, (CH, CH), 1)
        tri = (cc <= rr).astype(BF16)

        def drop(c, carry):
            t = s_ref[chunk(c), :]
            tie = t == lo
            tie_f = jnp.where(tie, 1.0, 0.0)
            rank = carry + _dot(tri, tie_f.astype(BF16))
            s_ref[chunk(c), :] = jnp.where(tie & (rank > need), NEG_INF, t)
            return carry + jnp.sum(tie_f, axis=0, keepdims=True)

        lax.fori_loop(0, nc, drop, jnp.zeros((1, KT), F32))

    def bias_cols(c):
        b = jnp.where(s_ref[chunk(c), :] >= lo, 0.0, NEG_INF)
        return jnp.concatenate([b] * ATT_GROUP, axis=1)

    G = ATT_KV_HEADS

    def max_chunk(c, macc):
        bias = bias_cols(c)
        tops = []
        for g in range(G):
            sc = _dot_nt(k_ref[chunk(c), g * LANES:(g + 1) * LANES], qg_ref[g]) + bias
            sc_ref[chunk(c), g * GQ:(g + 1) * GQ] = sc
            tops.append(jnp.max(fold_rows(sc), axis=0))
        return jnp.maximum(macc, jnp.concatenate(tops, axis=-1))

    def max_pair(j, macc):
        c0, c1 = pair(j)
        return max_chunk(c1, max_chunk(c0, macc))

    macc = lax.fori_loop(0, n_pairs, max_pair, jnp.full((NACC, SUBLANES, G * GQ), NEG_INF, F32))
    m = unfold(macc, jnp.max)

    ones_rows = jnp.ones((2 * SUBLANES, CH), BF16)
    acc_ref[...] = jnp.zeros_like(acc_ref)

    def sum_chunk(c, shift):
        for g in range(G):
            p = jnp.exp(sc_ref[chunk(c), g * GQ:(g + 1) * GQ] - shift[:, g * GQ:(g + 1) * GQ]).astype(BF16)
            vt = jnp.concatenate([vt_ref[g * LANES:(g + 1) * LANES, chunk(c)], ones_rows], axis=0)
            acc_ref[g] += _dot(vt, p)

    def sum_pair(j, _):
        c0, c1 = pair(j)
        sum_chunk(c0, m)
        sum_chunk(c1, jnp.where(2 * j + 1 < nc, m, jnp.inf))
        return 0

    lax.fori_loop(0, n_pairs, sum_pair, 0)
    for g in range(G):
        acc = acc_ref[g]
        out_t = acc[:ATT_HEAD_DIM] / acc[ATT_HEAD_DIM:ATT_HEAD_DIM + 1]
        for hq in range(ATT_GROUP):
            h = g * ATT_GROUP + hq
            o_ref[:, h * LANES:(h + 1) * LANES] = out_t[:, hq * KT:(hq + 1) * KT].T.astype(o_ref.dtype)


def _dsa_prompt(q, iq, ikw, k, vt, ik2, n_batch, lp, topk):
    KT = KEY_TILE
    nq = lp // KT
    kv = ATT_KV_HEADS * ATT_HEAD_DIM
    qspec = lambda width: pl.BlockSpec((KT, width), lambda b, i: (b * nq + i, 0))
    return pl.pallas_call(
        functools.partial(_dsa_prompt_kernel, topk),
        grid=(n_batch, nq),
        in_specs=[qspec(ATT_HEADS * ATT_HEAD_DIM), qspec(IDX_HEADS * IDX_HEAD_DIM), qspec(LANES),
                  pl.BlockSpec((lp, kv), lambda b, i: (b, 0)),
                  pl.BlockSpec((kv, lp), lambda b, i: (0, b)),
                  pl.BlockSpec((lp, LANES), lambda b, i: (b, 0))],
        out_specs=qspec(ATT_HEADS * ATT_HEAD_DIM),
        out_shape=jax.ShapeDtypeStruct(q.shape, BF16),
        scratch_shapes=[pltpu.VMEM((lp, KT), F32),
                        pltpu.VMEM((IDX_HEADS * KT, LANES), BF16),
                        pltpu.VMEM((ATT_KV_HEADS, ATT_GROUP * KT, ATT_HEAD_DIM), BF16),
                        pltpu.VMEM((lp, ATT_HEADS * KT), F32),
                        pltpu.VMEM((ATT_KV_HEADS, ATT_HEAD_DIM + 2 * SUBLANES, ATT_GROUP * KT), F32)],
        compiler_params=_cparams(("parallel", "arbitrary")),
        name="dsa_prompt",
    )(q, iq, ikw, k, vt, ik2)


SAMPLE_GROUP = 16


def _dsa_sample_select_kernel(topk, n_pages, pt_ref, iq_ref, ikw_ref, *rest):
    del pt_ref
    page_refs = rest[:n_pages]
    sel_ref, lo_ref, stack_ref, st_ref, iq8_ref, ikw8_ref, newk_ref = rest[n_pages:]
    b = pl.program_id(0)
    j = b % SAMPLE_GROUP
    T = iq_ref.shape[1]
    R = 2 * T
    past = n_pages * PAGE
    n_keys = stack_ref.shape[1]
    GR = SAMPLE_GROUP * R

    iq8_ref[0:T, :] = iq_ref[0].astype(F32)
    iq8_ref[T:R, :] = iq_ref[0].astype(F32)
    ikw8_ref[0:T, :] = ikw_ref[0]
    ikw8_ref[T:R, :] = ikw_ref[0]
    lane = lax.broadcasted_iota(jnp.int32, (R, LANES), 1)
    ikw8 = ikw8_ref[...]
    xh = []
    for h in range(IDX_HEADS):
        tile = iq8_ref[:, (h // 2) * LANES:(h // 2 + 1) * LANES]
        if h % 2:
            tile = pltpu.roll(tile, IDX_HEAD_DIM, axis=1)
        xh.append(jnp.where(lane < IDX_HEAD_DIM, tile, 0.0))
    x_all = jnp.concatenate(xh, axis=0).astype(BF16)

    def index_scores(d):
        acc = jnp.zeros((R, LANES), F32)
        for h in range(IDX_HEADS):
            wcol = ikw8[:, IDX_HEAD_DIM + h:IDX_HEAD_DIM + h + 1]
            acc = acc + wcol * jnp.maximum(d[h * R:(h + 1) * R], 0.0)
        return acc * (IDX_HEAD_DIM ** -0.5)

    rows = pl.ds(pl.multiple_of(j * R, R), R)
    zeros64 = jnp.zeros((LANES - IDX_HEAD_DIM, PAGE), BF16)
    for p in range(n_pages):
        keys_t = jnp.concatenate([page_refs[p][0].astype(BF16), zeros64], axis=0)
        sc = index_scores(_dot(x_all, keys_t))
        if p == 0:
            sc = jnp.where(lane < N_META, jnp.inf, sc)
        stack_ref[rows, p * PAGE:(p + 1) * PAGE] = sc
    tmod = lax.broadcasted_iota(jnp.int32, (R, 1), 0) % T
    newk_ref[...] = jnp.zeros_like(newk_ref)
    newk_ref[0:T, :] = ikw_ref[0]
    nk = newk_ref[...]
    lane_k = lax.broadcasted_iota(jnp.int32, nk.shape, 1)
    sc_new = index_scores(_dot_nt(x_all, jnp.where(lane_k < IDX_HEAD_DIM, nk, 0.0).astype(BF16)))
    stack_ref[rows, past:n_keys] = jnp.where(lane <= tmod, sc_new, NEG_INF)

    @pl.when(j == SAMPLE_GROUP - 1)
    def _():
        st_ref[...] = stack_ref[...].T
        qlane = lax.broadcasted_iota(jnp.int32, (1, GR), 1)
        qpos = past + (qlane % R) % T
        target = jnp.minimum(qpos + 1, topk).astype(F32)
        NACC = 8

        def count_where(test):
            hit = jnp.where(test(st_ref[...]), 1.0, 0.0)
            part = jnp.sum(hit.reshape(n_keys // (NACC * SUBLANES), NACC, SUBLANES, GR), axis=0)
            return jnp.sum(jnp.sum(part, axis=0), axis=0, keepdims=True)

        def count_ge(thr):
            return count_where(lambda t: t >= thr)

        mag = jnp.abs(st_ref[...])
        mag = jnp.where(mag < jnp.inf, mag, 0.0)
        amax = jnp.max(jnp.max(mag.reshape(n_keys // SUBLANES, SUBLANES, GR), axis=0), axis=0, keepdims=True)
        def max_below(thr):
            t = st_ref[...]
            below = jnp.where(t < thr, t, NEG_INF).reshape(n_keys // (NACC * SUBLANES), NACC, SUBLANES, GR)
            return jnp.max(jnp.max(jnp.max(below, axis=0), axis=0), axis=0, keepdims=True)

        lo = _kth_largest_bounds(count_ge, max_below, target, amax)
        surplus = count_ge(lo) - target

        @pl.when(jnp.max(surplus) > 0.0)
        def _():
            need = target - count_where(lambda t: t > lo)
            rr = lax.broadcasted_iota(jnp.int32, (LANES, LANES), 0)
            cc = lax.broadcasted_iota(jnp.int32, (LANES, LANES), 1)
            tri = (cc <= rr).astype(BF16)
            carry = jnp.zeros((1, GR), F32)
            for kt in range(n_keys // LANES):
                t = st_ref[kt * LANES:(kt + 1) * LANES, :]
                tie = t == lo
                tie_f = jnp.where(tie, 1.0, 0.0)
                rank = carry + _dot(tri, tie_f.astype(BF16))
                st_ref[kt * LANES:(kt + 1) * LANES, :] = jnp.where(tie & (rank > need), NEG_INF, t)
                carry = carry + jnp.sum(tie_f, axis=0, keepdims=True)

        sel_ref[...] = st_ref[...].T
        lo_ref[...] = jnp.broadcast_to(lo, (GR, GR)).T


def _dsa_sample_select(page_table, iq, ikw, cache_ik, topk):
    nb, t, _ = iq.shape
    n_pages = page_table.shape[1]
    n_keys = n_pages * PAGE + LANES
    gr = SAMPLE_GROUP * 2 * t
    assert nb % SAMPLE_GROUP == 0 and gr == LANES
    tok = lambda width: pl.BlockSpec((1, t, width), lambda b, pt: (b, 0, 0))
    page = lambda p: pl.BlockSpec((1, IDX_HEAD_DIM, PAGE), lambda b, pt: (pt[b, p], 0, 0))
    grid_spec = pltpu.PrefetchScalarGridSpec(
        num_scalar_prefetch=1,
        grid=(nb,),
        in_specs=[tok(IDX_HEADS * IDX_HEAD_DIM), tok(LANES)] + [page(p) for p in range(n_pages)],
        out_specs=[pl.BlockSpec((gr, n_keys), lambda b, pt: (b // SAMPLE_GROUP, 0)),
                   pl.BlockSpec((gr, LANES), lambda b, pt: (b // SAMPLE_GROUP, 0))],
        scratch_shapes=[pltpu.VMEM((gr, n_keys), F32),
                        pltpu.VMEM((n_keys, gr), F32),
                        pltpu.VMEM((2 * t, IDX_HEADS * IDX_HEAD_DIM), F32),
                        pltpu.VMEM((2 * t, LANES), F32),
                        pltpu.VMEM((LANES, LANES), F32)])
    return pl.pallas_call(
        functools.partial(_dsa_sample_select_kernel, topk, n_pages),
        grid_spec=grid_spec,
        out_shape=[jax.ShapeDtypeStruct((nb * 2 * t, n_keys), F32),
                   jax.ShapeDtypeStruct((nb * 2 * t, LANES), F32)],
        compiler_params=_cparams(("arbitrary",)),
        name="dsa_sample_select",
    )(page_table, iq, ikw, *([cache_ik] * n_pages))


def _dsa_sample_attend_kernel(n_pages, pt_ref, q_ref, kn_ref, vn_ref, sel_ref, lo_ref, *rest):
    del pt_ref
    n_seq = q_ref.shape[0]
    pages = rest[:2 * n_seq * n_pages]
    o_ref, q8_ref, newk_ref, newv_ref = rest[2 * n_seq * n_pages:]
    R = 2 * q_ref.shape[1]
    for i in range(n_seq):
        one = pl.ds(i, 1)
        _dsa_sample_attend_seq(q_ref.at[one], kn_ref.at[one], vn_ref.at[one],
                               sel_ref.at[pl.ds(i * R, R)], lo_ref.at[pl.ds(i * R, R)],
                               pages[i * n_pages:(i + 1) * n_pages],
                               pages[(n_seq + i) * n_pages:(n_seq + i + 1) * n_pages],
                               o_ref.at[one], q8_ref.at[i], newk_ref.at[i], newv_ref.at[i])


def _dsa_sample_attend_seq(q_ref, kn_ref, vn_ref, sel_ref, lo_ref, k_refs, v_refs, o_ref,
                           q8_ref, newk_ref, newv_ref):
    T = q_ref.shape[1]
    R = 2 * T
    q8_ref[0:T, :] = q_ref[0].astype(F32)
    q8_ref[T:R, :] = q_ref[0].astype(F32)
    newk_ref[...] = jnp.zeros_like(newk_ref)
    newv_ref[...] = jnp.zeros_like(newv_ref)
    newk_ref[0:T, :] = kn_ref[0]
    newv_ref[0:T, :] = vn_ref[0]
    bias8 = jnp.where(sel_ref[...] >= lo_ref[:, 0:1], 0.0, NEG_INF)
    bias = jnp.concatenate([bias8] * ATT_GROUP, axis=0)
    for g in range(ATT_KV_HEADS):
        qg = jnp.concatenate([q8_ref[:, (g * ATT_GROUP + hq) * LANES:(g * ATT_GROUP + hq + 1) * LANES]
                              for hq in range(ATT_GROUP)], axis=0).astype(BF16)
        head_rows = pl.ds(g, PAGE, stride=ATT_KV_HEADS)
        k_all = jnp.concatenate([r[0, head_rows, :].astype(BF16) for r in k_refs]
                                + [newk_ref[:, g * LANES:(g + 1) * LANES].astype(BF16)], axis=0)
        v_all = jnp.concatenate([r[0, head_rows, :].astype(BF16) for r in v_refs]
                                + [newv_ref[:, g * LANES:(g + 1) * LANES].astype(BF16)], axis=0)
        sc = _dot_nt(qg, k_all) + bias
        m = jnp.max(sc, axis=1, keepdims=True)
        pr = jnp.exp(sc - m)
        out = _dot(pr.astype(BF16), v_all) / jnp.sum(pr, axis=1, keepdims=True)
        for hq in range(ATT_GROUP):
            h = g * ATT_GROUP + hq
            o_ref[0, :, h * LANES:(h + 1) * LANES] = out[hq * R:hq * R + T].astype(o_ref.dtype)


def _dsa_sample_attend(page_table, q, k_new, v_new, sel, lo, cache_k, cache_v):
    nb, t, _ = q.shape
    n_pages = page_table.shape[1]
    kv = ATT_KV_HEADS * ATT_HEAD_DIM
    n_keys = sel.shape[1]
    sq = SAMPLE_SEQS_PER_STEP
    assert nb % sq == 0
    tok = lambda width: pl.BlockSpec((sq, t, width), lambda b, pt: (b, 0, 0))
    page = lambda i, p: pl.BlockSpec((1, ATT_KV_HEADS * PAGE, ATT_HEAD_DIM),
                                     lambda b, pt: (pt[b * sq + i, p], 0, 0))
    pages = [page(i, p) for i in range(sq) for p in range(n_pages)]
    grid_spec = pltpu.PrefetchScalarGridSpec(
        num_scalar_prefetch=1,
        grid=(nb // sq,),
        in_specs=[tok(ATT_HEADS * ATT_HEAD_DIM), tok(kv), tok(kv),
                  pl.BlockSpec((sq * 2 * t, n_keys), lambda b, pt: (b, 0)),
                  pl.BlockSpec((sq * 2 * t, LANES), lambda b, pt: (b, 0))] + pages * 2,
        out_specs=tok(ATT_HEADS * ATT_HEAD_DIM),
        scratch_shapes=[pltpu.VMEM((sq, 2 * t, ATT_HEADS * ATT_HEAD_DIM), F32),
                        pltpu.VMEM((sq, LANES, kv), F32),
                        pltpu.VMEM((sq, LANES, kv), F32)])
    return pl.pallas_call(
        functools.partial(_dsa_sample_attend_kernel, n_pages),
        grid_spec=grid_spec,
        out_shape=jax.ShapeDtypeStruct((nb, t, ATT_HEADS * ATT_HEAD_DIM), BF16),
        compiler_params=_cparams(("parallel",)),
        name="dsa_sample_attend",
    )(page_table, q, k_new, v_new, sel, lo, *([cache_k] * (sq * n_pages)), *([cache_v] * (sq * n_pages)))


def kernel(x_prompt, x_sample, state_gdn, state_gdn_conv, cache_k, cache_v, cache_idx_k, page_table,
           meta_tokens, ln1_g, ln1_b, ln2_g, ln2_b, mlp_w1, mlp_w2,
           gdn_w_in, gdn_conv_w, gdn_a_log, gdn_dt_bias, gdn_norm_w, gdn_w_out,
           dsa_w_in, dsa_ik_norm_g, dsa_ik_norm_b, dsa_w_o):
    nb, seq, d = x_prompt.shape
    ns, ts, _ = x_sample.shape
    n_tok = N_META + seq
    lp = -(-n_tok // KEY_TILE) * KEY_TILE
    n_prompt_rows = nb * lp
    rows = n_prompt_rows + ns * ts
    assert rows % ROW_TILE == 0 and lp % GDN_CHUNK == 0 and lp % KEY_CHUNK == 0 and d == D_MODEL
    kvd = ATT_KV_HEADS * ATT_HEAD_DIM

    meta = meta_tokens.astype(x_prompt.dtype)
    pad = jnp.zeros((lp - n_tok, d), x_prompt.dtype)
    h = jnp.concatenate([piece for b in range(nb) for piece in (meta, x_prompt[b], pad)]
                        + [x_sample.reshape(ns * ts, d)], 0)

    def row2(x):
        return x.reshape(1, -1)

    def prompt_rows(x, width, first=0):
        rows_b = [x[b * lp + first:b * lp + n_tok, :width] for b in range(nb)]
        return jnp.concatenate(rows_b, 0).reshape(nb, n_tok - first, width)

    def with_sample_rows(a_prompt, a_sample):
        return lax.dynamic_update_slice(a_prompt, a_sample.reshape(ns * ts, -1), (n_prompt_rows, 0))

    w_in = gdn_w_in[0]
    split = GDN_CONV_DIM + GDN_VALUE_DIM
    n_gate = 2 * GDN_V_HEADS
    w_cat = jnp.concatenate([w_in, jnp.zeros((d, LANES - n_gate), w_in.dtype)], 1).astype(BF16)
    proj, ba = _gdn_inproj(h, w_cat, split)
    decay_lanes = lambda v: jnp.pad(row2(v), ((0, 0), (GDN_V_HEADS, LANES - n_gate)))
    a_p, gdn_state_prompt = _gdn_prompt(proj, ba, gdn_conv_w[0], decay_lanes(gdn_a_log[0]),
                                        decay_lanes(gdn_dt_bias[0]), row2(gdn_norm_w[0]), nb, lp, n_tok)
    proj_s = proj[n_prompt_rows:].reshape(ns, ts, split)
    mixed_s = proj_s[:, :, :GDN_CONV_DIM]
    a_s, gdn_state_sample = _gdn_sample(state_gdn_conv[0], mixed_s, proj_s[:, :, GDN_CONV_DIM:],
                                        ba[n_prompt_rows:, :n_gate].reshape(ns, ts, n_gate), gdn_conv_w[0],
                                        row2(gdn_a_log[0]), row2(gdn_dt_bias[0]), row2(gdn_norm_w[0]),
                                        state_gdn[0])
    def tail_weights(i, w_out):
        return (w_out.astype(BF16), row2(ln1_g[i]), row2(ln1_b[i]),
                mlp_w1[i].astype(BF16), mlp_w2[i].astype(BF16), row2(ln2_g[i]), row2(ln2_b[i]))

    h = _block_tail(with_sample_rows(a_p, a_s), h, tail_weights(0, gdn_w_out[0]))
    keep = GDN_CONV_WIDTH - 1
    gdn_conv_prompt = prompt_rows(proj, GDN_CONV_DIM, first=n_tok - keep)
    gdn_conv_sample = jnp.concatenate([state_gdn_conv[0], mixed_s], 1)[:, -keep:]

    past = page_table.shape[1] * PAGE
    pos = jnp.concatenate([jnp.tile(jnp.arange(lp, dtype=jnp.int32), nb),
                           jnp.tile(past + jnp.arange(ts, dtype=jnp.int32), ns)])
    tables = _rope_tables(pos, ATT_ROT_HALF, LANES) + _rope_tables(pos, IDX_ROT_HALF, IDX_HEAD_DIM)
    n_in = dsa_w_in.shape[2]
    w_dsa = jnp.pad(dsa_w_in[0], ((0, 0), (0, -n_in % LANES))).astype(BF16)
    pad_lanes = lambda v: row2(jnp.pad(v, (0, LANES - v.shape[0])))
    q, k, v, vt, iq, ikw, ik2 = _dsa_proj(h, w_dsa, tables, pad_lanes(dsa_ik_norm_g[0]),
                                          pad_lanes(dsa_ik_norm_b[0]))
    o_p = _dsa_prompt(q, iq, ikw, k.astype(BF16), vt, ik2, nb, lp, min(TOPK_MAX, (n_tok - N_META) // 4))
    smp = lambda x: x[n_prompt_rows:].reshape(ns, ts, -1)
    n_pool = cache_k.shape[1]
    sel, lo = _dsa_sample_select(page_table, smp(iq), smp(ikw), jnp.swapaxes(cache_idx_k[0], 1, 2),
                                 min(TOPK_MAX, (past + ts) // 4))
    rows_kh = lambda c: c[0].reshape(n_pool, PAGE * ATT_KV_HEADS, ATT_HEAD_DIM)
    o_s = _dsa_sample_attend(page_table, smp(q), smp(k), smp(v), sel, lo, rows_kh(cache_k), rows_kh(cache_v))
    last = tail_weights(1, dsa_w_o[0])
    y_prompt = _block_tail_sequences(o_p, h, last, nb, lp, N_META, seq)
    y_sample = _block_tail(o_s.reshape(ns * ts, -1), h[n_prompt_rows:], last).reshape(ns, ts, d)

    heads = lambda x: x.reshape(x.shape[:-1] + (ATT_KV_HEADS, ATT_HEAD_DIM))
    return (y_prompt, y_sample,
            gdn_state_prompt[None], gdn_conv_prompt[None], gdn_state_sample[None], gdn_conv_sample[None],
            heads(prompt_rows(k, kvd))[None], heads(prompt_rows(v, kvd))[None],
            prompt_rows(ikw, IDX_HEAD_DIM)[None],
            heads(smp(k))[None], heads(smp(v))[None], smp(ikw)[:, :, :IDX_HEAD_DIM][None])
```

```python
import functools

import jax
import jax.numpy as jnp
from jax import lax
from jax.experimental import pallas as pl
from jax.experimental.pallas import tpu as pltpu

F32 = jnp.float32
BF16 = jnp.bfloat16
HIGHEST = lax.Precision.HIGHEST

D_MODEL = 1024
N_META = 16
DEPTH = 2
LN_EPS = 1e-5
DEEPNORM_ALPHA = (2 * DEPTH) ** 0.25
GDN_K_HEADS = 8
GDN_V_HEADS = 16
GDN_HEAD = 128
GDN_KEY_DIM = GDN_K_HEADS * GDN_HEAD
GDN_VALUE_DIM = GDN_V_HEADS * GDN_HEAD
GDN_CONV_DIM = 2 * GDN_KEY_DIM + GDN_VALUE_DIM
GDN_CONV_WIDTH = 4
GDN_CHUNK = 128
L2_EPS = 1e-6
RMS_EPS = 1e-6
ATT_HEADS = 8
ATT_KV_HEADS = 2
ATT_HEAD_DIM = 128
ATT_GROUP = ATT_HEADS // ATT_KV_HEADS
IDX_HEADS = 8
IDX_HEAD_DIM = 64
TOPK_MAX = 256
ROPE_THETA = 500000.0
ATT_ROT_HALF = ATT_HEAD_DIM // 8
IDX_ROT_HALF = IDX_HEAD_DIM // 8
PAGE = 128

LANES = 128
SUBLANES = 8
ROW_TILE = 512
KEY_TILE = 128
KEY_CHUNK = 3 * KEY_TILE
VMEM_LIMIT = 56 * 1024 * 1024

NEG_INF = float("-inf")


def _cparams(sem):
    return pltpu.CompilerParams(dimension_semantics=sem, vmem_limit_bytes=VMEM_LIMIT)


def _dot(a, b):
    return jnp.dot(a, b, preferred_element_type=F32)


def _dot_nt(a, b, precision=None):
    return lax.dot_general(a, b, (((1,), (1,)), ((), ())), preferred_element_type=F32,
                           precision=precision)


def _dot_tn(a, b):
    return lax.dot_general(a, b, (((0,), (0,)), ((), ())), preferred_element_type=F32)


def _layernorm_rows(x, g, b):
    mu = jnp.mean(x, axis=-1, keepdims=True)
    xc = x - mu
    var = jnp.mean(xc * xc, axis=-1, keepdims=True)
    return xc * lax.rsqrt(var + LN_EPS) * g + b


def _sigmoid(x):
    return 0.5 * jnp.tanh(0.5 * x) + 0.5


def _silu(x):
    return x * _sigmoid(x)


def _softplus(x):
    return jnp.maximum(x, 0.0) + jnp.log(1.0 + jnp.exp(-jnp.abs(x)))


def _resident(shape):
    return pl.BlockSpec(shape, lambda *_: (0,) * len(shape), pipeline_mode=pl.Buffered(1))


def _gdn_inproj_kernel(x_ref, w_ref, proj_ref, ba_ref):
    x = x_ref[...].astype(BF16)
    n_main = proj_ref.shape[1]
    slab = 1024
    for j in range(n_main // slab):
        proj_ref[:, j * slab:(j + 1) * slab] = _dot(x, w_ref[:, j * slab:(j + 1) * slab])
    ba_ref[...] = _dot(x, w_ref[:, n_main:])


def _gdn_inproj(x, w, n_main):
    rows, k = x.shape
    n = w.shape[1]
    return pl.pallas_call(
        _gdn_inproj_kernel,
        grid=(rows // ROW_TILE,),
        in_specs=[pl.BlockSpec((ROW_TILE, k), lambda i: (i, 0)), _resident((k, n))],
        out_specs=[pl.BlockSpec((ROW_TILE, n_main), lambda i: (i, 0)),
                   pl.BlockSpec((ROW_TILE, n - n_main), lambda i: (i, 0))],
        out_shape=[jax.ShapeDtypeStruct((rows, n_main), F32),
                   jax.ShapeDtypeStruct((rows, n - n_main), F32)],
        compiler_params=_cparams(("parallel",)),
        name="gdn_inproj",
    )(x, w)


def _block_tail_kernel(a_ref, x_ref, wo_ref, g1_ref, b1_ref, w1_ref, w2_ref, g2_ref, b2_ref, o_ref):
    x = _layernorm_rows(DEEPNORM_ALPHA * x_ref[...] + _dot(a_ref[...], wo_ref[...]), g1_ref[...], b1_ref[...])
    t = jnp.maximum(_dot(x.astype(BF16), w1_ref[...]), 0.0)
    y = DEEPNORM_ALPHA * x + _dot((t * t).astype(BF16), w2_ref[...])
    o_ref[...] = _layernorm_rows(y, g2_ref[...], b2_ref[...])


def _tail_weight_specs(k, d, f):
    return [_resident((k, d)), _resident((1, d)), _resident((1, d)),
            _resident((d, f)), _resident((f, d)), _resident((1, d)), _resident((1, d))]


def _block_tail(a, x, weights):
    rows, k = a.shape
    d = x.shape[1]
    f = weights[3].shape[1]
    return pl.pallas_call(
        _block_tail_kernel,
        grid=(rows // ROW_TILE,),
        in_specs=[pl.BlockSpec((ROW_TILE, k), lambda i: (i, 0)),
                  pl.BlockSpec((ROW_TILE, d), lambda i: (i, 0))] + _tail_weight_specs(k, d, f),
        out_specs=pl.BlockSpec((ROW_TILE, d), lambda i: (i, 0)),
        out_shape=jax.ShapeDtypeStruct((rows, d), F32),
        compiler_params=_cparams(("parallel",)),
        name="block_tail",
    )(a, x, *weights)


def _block_tail_sequences(a, x, weights, n_seq, stride, first, n_out):
    k = a.shape[1]
    d = x.shape[1]
    f = weights[3].shape[1]
    align = 2 * SUBLANES
    assert n_out % ROW_TILE == 0 and (stride % align, first % align) == (0, 0)

    def kern(*refs):
        _block_tail_kernel(*refs[:-1], refs[-1].at[0])

    def window(width):
        return pl.BlockSpec((pl.Element(ROW_TILE), pl.Element(width)),
                            lambda s, t: (pl.multiple_of(s * stride + first + t * ROW_TILE, align), 0))

    return pl.pallas_call(
        kern,
        grid=(n_seq, n_out // ROW_TILE),
        in_specs=[window(k), window(d)] + _tail_weight_specs(k, d, f),
        out_specs=pl.BlockSpec((1, ROW_TILE, d), lambda s, t: (s, t, 0)),
        out_shape=jax.ShapeDtypeStruct((n_seq, n_out, d), F32),
        compiler_params=_cparams(("parallel", "parallel")),
        name="block_tail_sequences",
    )(a, x, *weights)


def _unit_lower_inverse(a, n_factors):
    c = a.shape[0]
    row = lax.broadcasted_iota(jnp.int32, (c, c), 0)
    col = lax.broadcasted_iota(jnp.int32, (c, c), 1)
    eye = (row == col).astype(F32)
    p = -a
    t = eye + p
    for _ in range(n_factors - 1):
        pb = p.astype(BF16)
        p = _dot(pb, pb)
        t = t + _dot(t.astype(BF16), p.astype(BF16))
    return t


def _bdot(a, b):
    return lax.dot_general(a, b, (((2,), (1,)), ((0,), (0,))), preferred_element_type=F32)


def _bdot_nt(a, b):
    return lax.dot_general(a, b, (((2,), (2,)), ((0,), (0,))), preferred_element_type=F32)


def _bdot_tn(a, b):
    return lax.dot_general(a, b, (((1,), (1,)), ((0,), (0,))), preferred_element_type=F32)


def _unit_lower_inverse_batched(a):
    c = a.shape[-1]
    row = lax.broadcasted_iota(jnp.int32, (c, c), 0)
    col = lax.broadcasted_iota(jnp.int32, (c, c), 1)
    def coupling(s):
        couples = ((row // (2 * s)) == (col // (2 * s))) & ((row // s) != (col // s))
        return jnp.where(couples, a, 0.0)

    t = (row == col).astype(F32) - coupling(1)
    s = 2
    while s < c:
        tb = t.astype(BF16)
        t = t - _bdot(_bdot(tb, coupling(s).astype(BF16)).astype(BF16), tb)
        s *= 2
    return t


def _pairwise_diff(col):
    c = col.shape[0]
    lane = lax.broadcasted_iota(jnp.int32, (c, LANES), 1)
    left = jnp.where(lane == 0, col, jnp.where(lane == 1, 1.0, 0.0))
    right = jnp.where(lane == 0, 1.0, jnp.where(lane == 1, -col, 0.0))
    return _dot_nt(left, right, precision=HIGHEST)


def _l2norm_rows(x):
    return x * lax.rsqrt(jnp.sum(x * x, axis=-1, keepdims=True) + L2_EPS)


def _gated_rmsnorm(o, z, norm_w):
    on = o * lax.rsqrt(jnp.mean(o * o, axis=-1, keepdims=True) + RMS_EPS) * norm_w
    return on * _silu(z)


def _gdn_gates(ba, a_log, dt_bias):
    beta = _sigmoid(ba[:, :GDN_V_HEADS])
    g = -jnp.exp(a_log) * _softplus(ba[:, GDN_V_HEADS:] + dt_bias)
    return beta, g


def _gdn_prompt_kernel(n_tokens, mixed_ref, z_ref, ba_ref, convw_ref, alog_ref, dtb_ref, normw_ref,
                       o_ref, sfin_ref, xc_ref, s_ref):
    c = pl.program_id(1)
    C = GDN_CHUNK
    W = GDN_CONV_WIDTH

    @pl.when(c == 0)
    def _():
        xc_ref[...] = jnp.zeros_like(xc_ref)
        s_ref[...] = jnp.zeros_like(s_ref)

    x = mixed_ref[...]
    tail = xc_ref[...]
    w = convw_ref[...]
    row8 = lax.broadcasted_iota(jnp.int32, (SUBLANES, 1), 0)
    acc = x * w[W - 1:W, :]
    for s in range(1, W):
        shifted = pltpu.roll(x, s, axis=0)
        head = jnp.where(row8 < s, pltpu.roll(tail, s, axis=0), shifted[:SUBLANES])
        acc = acc + jnp.concatenate([head, shifted[SUBLANES:]], axis=0) * w[W - 1 - s:W - s, :]
    qkv = _silu(acc)
    xc_ref[...] = x[C - SUBLANES:]

    row = lax.broadcasted_iota(jnp.int32, (C, 1), 0)
    valid = (c * C + row) < n_tokens
    ba = ba_ref[...]
    beta = jnp.where(valid, _sigmoid(ba), 0.0)
    g = jnp.where(valid, -jnp.exp(alog_ref[...]) * _softplus(ba + dtb_ref[...]), 0.0)

    ri = lax.broadcasted_iota(jnp.int32, (C, C), 0)
    ci = lax.broadcasted_iota(jnp.int32, (C, C), 1)
    tril = ci <= ri
    strict = ci < ri
    gc = jnp.dot(tril.astype(F32), g, preferred_element_type=F32, precision=HIGHEST)
    gc_rows = gc.T
    egc = jnp.exp(gc)
    glast = gc[C - 1:C, :]
    ekd = jnp.exp(glast - gc)
    eglast = jnp.exp(glast)
    normw = normw_ref[...]
    NH = GDN_V_HEADS

    def gcol(x, h):
        return x[:, NH + h:NH + h + 1]

    qk_raw = [qkv[:, i * GDN_HEAD:(i + 1) * GDN_HEAD] for i in range(2 * GDN_K_HEADS)]
    sq = jnp.concatenate([t * t for t in qk_raw], axis=0)
    sq_hi = sq.astype(BF16)
    sq_lo = (sq - sq_hi.astype(F32)).astype(BF16)
    ones = jnp.ones((GDN_HEAD, GDN_HEAD), BF16)
    inv_norm = lax.rsqrt(_dot(sq_hi, ones) + _dot(sq_lo, ones) + L2_EPS)
    qk_n = [t * inv_norm[i * C:(i + 1) * C] for i, t in enumerate(qk_raw)]
    q_l = [t * (GDN_HEAD ** -0.5) for t in qk_n[:GDN_K_HEADS]]
    k_l = qk_n[GDN_K_HEADS:]
    k8 = jnp.stack(k_l).astype(BF16)
    kk8 = _bdot_nt(k8, k8)
    qk8 = _bdot_nt(jnp.stack(q_l).astype(BF16), k8)
    a_l, qkd_l, rhs_l, qg_l, kd_l = [], [], [], [], []
    for h in range(NH):
        kh = h // 2
        v = qkv[:, 2 * GDN_KEY_DIM + h * GDN_HEAD:2 * GDN_KEY_DIM + (h + 1) * GDN_HEAD]
        bcol = beta[:, h:h + 1]
        diff = gcol(gc, h) - gc_rows[NH + h:NH + h + 1, :]
        decay = jnp.where(tril, jnp.exp(jnp.where(tril, diff, 0.0)), 0.0)
        a_l.append(jnp.where(strict, kk8[kh] * decay * bcol, 0.0))
        qkd_l.append((qk8[kh] * decay).astype(BF16))
        rhs_l.append(jnp.concatenate([v * bcol, k_l[kh] * (bcol * gcol(egc, h))], axis=1).astype(BF16))
        qg_l.append(q_l[kh] * gcol(egc, h))
        kd_l.append((k_l[kh] * gcol(ekd, h)).astype(BF16))
    t_inv = _unit_lower_inverse_batched(jnp.stack(a_l)).astype(BF16)
    sol = _bdot(t_inv, jnp.stack(rhs_l))
    u, wm = sol[:, :, :GDN_HEAD], sol[:, :, GDN_HEAD:]
    s_old = s_ref[...]
    lhs = jnp.concatenate([wm, jnp.stack(qg_l)], axis=1).astype(BF16)
    ws_qs = _bdot(lhs, s_old.astype(BF16))
    v_new = u - ws_qs[:, :C]
    vnb = v_new.astype(BF16)
    o = ws_qs[:, C:] + _bdot(jnp.stack(qkd_l), vnb)
    upd = _bdot_tn(jnp.stack(kd_l), vnb)
    for h in range(NH):
        s_ref[h] = s_old[h] * gcol(eglast, h) + upd[h]
        zh = z_ref[:, h * GDN_HEAD:(h + 1) * GDN_HEAD]
        o_ref[:, h * GDN_HEAD:(h + 1) * GDN_HEAD] = _gated_rmsnorm(o[h], zh, normw).astype(o_ref.dtype)

    @pl.when(c == pl.num_programs(1) - 1)
    def _():
        sfin_ref[0] = s_ref[...]


def _gdn_prompt(proj, ba, conv_w, a_log, dt_bias, norm_w, n_batch, lp, n_tokens):
    C = GDN_CHUNK
    per_b = lp // C
    n_chunks = per_b
    z_off = GDN_CONV_DIM // GDN_VALUE_DIM
    kern = functools.partial(_gdn_prompt_kernel, n_tokens)
    return pl.pallas_call(
        kern,
        grid=(n_batch, n_chunks),
        in_specs=[pl.BlockSpec((C, GDN_CONV_DIM), lambda b, c: (b * per_b + c, 0)),
                  pl.BlockSpec((C, GDN_VALUE_DIM), lambda b, c: (b * per_b + c, z_off)),
                  pl.BlockSpec((C, LANES), lambda b, c: (b * per_b + c, 0)),
                  pl.BlockSpec((GDN_CONV_WIDTH, GDN_CONV_DIM), lambda b, c: (0, 0)),
                  pl.BlockSpec((1, LANES), lambda b, c: (0, 0)),
                  pl.BlockSpec((1, LANES), lambda b, c: (0, 0)),
                  pl.BlockSpec((1, GDN_HEAD), lambda b, c: (0, 0))],
        out_specs=[pl.BlockSpec((C, GDN_VALUE_DIM), lambda b, c: (b * per_b + c, 0)),
                   pl.BlockSpec((1, GDN_V_HEADS, GDN_HEAD, GDN_HEAD), lambda b, c: (b, 0, 0, 0))],
        out_shape=[jax.ShapeDtypeStruct((proj.shape[0], GDN_VALUE_DIM), BF16),
                   jax.ShapeDtypeStruct((n_batch, GDN_V_HEADS, GDN_HEAD, GDN_HEAD), F32)],
        scratch_shapes=[pltpu.VMEM((SUBLANES, GDN_CONV_DIM), F32),
                        pltpu.VMEM((GDN_V_HEADS, GDN_HEAD, GDN_HEAD), F32)],
        compiler_params=_cparams(("parallel", "arbitrary")),
        name="gdn_prompt",
    )(proj, proj, ba, conv_w, a_log, dt_bias, norm_w)


def _twice(dst_ref, x):
    t = x.shape[0]
    dst_ref[0:t, :] = x
    dst_ref[t:2 * t, :] = x
    return dst_ref[...]


SAMPLE_SEQS_PER_STEP = 4


def _gdn_sample_kernel(cs_ref, mixed_ref, z_ref, ba_ref, convw_ref, alog_ref, dtb_ref, normw_ref, s_ref,
                       o_ref, snew_ref, xc_ref, dq_ref, dz_ref, dba_ref):
    for i in range(mixed_ref.shape[0]):
        one = pl.ds(i, 1)
        _gdn_sample_seq(cs_ref.at[one], mixed_ref.at[one], z_ref.at[one], ba_ref.at[one], convw_ref, alog_ref,
                        dtb_ref, normw_ref, s_ref.at[one], o_ref.at[one], snew_ref.at[one],
                        xc_ref.at[i], dq_ref.at[i], dz_ref.at[i], dba_ref.at[i])


def _gdn_sample_seq(cs_ref, mixed_ref, z_ref, ba_ref, convw_ref, alog_ref, dtb_ref, normw_ref, s_ref,
                    o_ref, snew_ref, xc_ref, dq_ref, dz_ref, dba_ref):
    T = mixed_ref.shape[1]
    R = 2 * T
    W = GDN_CONV_WIDTH
    xc_ref[0:W - 1, :] = cs_ref[0]
    xc_ref[W - 1:W - 1 + T, :] = mixed_ref[0]
    w = convw_ref[...]
    acc = xc_ref[0:T, :] * w[0:1, :]
    for j in range(1, W):
        acc = acc + xc_ref[j:j + T, :] * w[j:j + 1, :]
    qkv = _twice(dq_ref, _silu(acc))
    z2 = _twice(dz_ref, z_ref[0])
    ba2 = _twice(dba_ref, ba_ref[0])
    beta, g = _gdn_gates(ba2, alog_ref[...], dtb_ref[...])

    rr = lax.broadcasted_iota(jnp.int32, (R, 1), 0)
    tmod = rr % T
    first = rr < T
    gc = jnp.zeros_like(g)
    for s in range(T):
        gc = gc + jnp.where(tmod >= s, g[s:s + 1, :], 0.0)
    glast = gc[T - 1:T, :]
    normw = normw_ref[...]

    NP = GDN_K_HEADS
    M = NP * R

    def pair_col(x, kh):
        return jnp.where(first, x[:, 2 * kh:2 * kh + 1], x[:, 2 * kh + 1:2 * kh + 2])

    def pair_tile(x, base, kh):
        a = x[:, base + (2 * kh) * GDN_HEAD:base + (2 * kh + 1) * GDN_HEAD]
        b = x[:, base + (2 * kh + 1) * GDN_HEAD:base + (2 * kh + 2) * GDN_HEAD]
        return jnp.where(first, a, b)

    q_t, k_t, v_t, z_t, b_c, gc_c, gl_c = [], [], [], [], [], [], []
    for kh in range(NP):
        q_t.append(_l2norm_rows(qkv[:, kh * GDN_HEAD:(kh + 1) * GDN_HEAD]) * (GDN_HEAD ** -0.5))
        k_t.append(_l2norm_rows(qkv[:, GDN_KEY_DIM + kh * GDN_HEAD:GDN_KEY_DIM + (kh + 1) * GDN_HEAD]))
        v_t.append(pair_tile(qkv, 2 * GDN_KEY_DIM, kh))
        z_t.append(pair_tile(z2, 0, kh))
        b_c.append(pair_col(beta, kh))
        gc_c.append(pair_col(gc, kh))
        gl_c.append(pair_col(jnp.broadcast_to(glast, (R, GDN_V_HEADS)), kh))
    q64 = jnp.concatenate(q_t, axis=0)
    k64 = jnp.concatenate(k_t, axis=0)
    v64 = jnp.concatenate(v_t, axis=0)
    bcol = jnp.concatenate(b_c, axis=0)
    gcol = jnp.concatenate(gc_c, axis=0)
    glcol = jnp.concatenate(gl_c, axis=0)

    ri = lax.broadcasted_iota(jnp.int32, (M, M), 0)
    ci = lax.broadcasted_iota(jnp.int32, (M, M), 1)
    same = (ri // T) == (ci // T)
    tril = same & ((ci % T) <= (ri % T))
    strict = same & ((ci % T) < (ri % T))
    kb = k64.astype(BF16)
    kk = _dot_nt(kb, kb)
    qk0 = _dot_nt(q64.astype(BF16), kb)
    diff = _pairwise_diff(gcol)
    decay = jnp.where(tril, jnp.exp(jnp.where(tril, diff, 0.0)), 0.0)
    a_mat = jnp.where(strict, kk * decay * bcol, 0.0)
    t_inv = _unit_lower_inverse(a_mat, max(1, (T - 1).bit_length())).astype(BF16)
    egc = jnp.exp(gcol)
    u = _dot(t_inv, (v64 * bcol).astype(BF16))
    wm = _dot(t_inv, (k64 * (bcol * egc)).astype(BF16))
    qg = q64 * egc
    kd = k64 * jnp.exp(glcol - gcol)
    qkd = (qk0 * decay).astype(BF16)

    v_new_t, qs_t = [], []
    for kh in range(NP):
        lhs = jnp.concatenate([wm[kh * R:(kh + 1) * R], qg[kh * R:(kh + 1) * R]], axis=0).astype(BF16)
        r0 = _dot(lhs, s_ref[0, 2 * kh].astype(BF16))
        r1 = _dot(lhs, s_ref[0, 2 * kh + 1].astype(BF16))
        ws = jnp.where(first, r0[:R], r1[:R])
        qs_t.append(jnp.where(first, r0[R:], r1[R:]))
        v_new_t.append(u[kh * R:(kh + 1) * R] - ws)
    v_new = jnp.concatenate(v_new_t, axis=0)
    o64 = jnp.concatenate(qs_t, axis=0) + _dot(qkd, v_new.astype(BF16))
    on = _gated_rmsnorm(o64, jnp.concatenate(z_t, axis=0), normw)

    for kh in range(NP):
        vn = v_new_t[kh].astype(BF16)
        kdp = kd[kh * R:(kh + 1) * R]
        for j in range(2):
            h = 2 * kh + j
            keep = first if j == 0 else jnp.logical_not(first)
            kdm = jnp.where(keep, kdp, 0.0).astype(BF16)
            eg = jnp.exp(glast[:, h:h + 1])
            snew_ref[0, h] = s_ref[0, h] * eg + _dot_tn(kdm, vn)
            tile = on[kh * R:(kh + 1) * R]
            if j == 1:
                tile = pltpu.roll(tile, T, axis=0)
            o_ref[0, :, h * GDN_HEAD:(h + 1) * GDN_HEAD] = tile[:T].astype(o_ref.dtype)


def _gdn_sample(conv_state, mixed, z, ba, conv_w, a_log, dt_bias, norm_w, state):
    nb, t, _ = mixed.shape
    sq = SAMPLE_SEQS_PER_STEP
    assert nb % sq == 0
    seq = lambda *tail: pl.BlockSpec((sq,) + tail, lambda b: (b,) + (0,) * len(tail))
    return pl.pallas_call(
        _gdn_sample_kernel,
        grid=(nb // sq,),
        in_specs=[seq(GDN_CONV_WIDTH - 1, GDN_CONV_DIM), seq(t, GDN_CONV_DIM), seq(t, GDN_VALUE_DIM),
                  seq(t, 2 * GDN_V_HEADS),
                  _resident((GDN_CONV_WIDTH, GDN_CONV_DIM)), _resident((1, GDN_V_HEADS)),
                  _resident((1, GDN_V_HEADS)), _resident((1, GDN_HEAD)),
                  seq(GDN_V_HEADS, GDN_HEAD, GDN_HEAD)],
        out_specs=[seq(t, GDN_VALUE_DIM), seq(GDN_V_HEADS, GDN_HEAD, GDN_HEAD)],
        out_shape=[jax.ShapeDtypeStruct((nb, t, GDN_VALUE_DIM), BF16),
                   jax.ShapeDtypeStruct(state.shape, F32)],
        scratch_shapes=[pltpu.VMEM((sq, GDN_CONV_WIDTH - 1 + t, GDN_CONV_DIM), F32),
                        pltpu.VMEM((sq, 2 * t, GDN_CONV_DIM), F32),
                        pltpu.VMEM((sq, 2 * t, GDN_VALUE_DIM), F32),
                        pltpu.VMEM((sq, 2 * t, 2 * GDN_V_HEADS), F32)],
        compiler_params=_cparams(("parallel",)),
        name="gdn_sample",
    )(conv_state, mixed, z, ba, conv_w, a_log, dt_bias, norm_w, state)


def _rope_tables(pos, half, period):
    inv_freq = ROPE_THETA ** (-jnp.arange(half, dtype=F32) * 2.0 / (2 * half))
    ang = pos.astype(F32)[:, None] * inv_freq[None, :]
    cos, sin = jnp.cos(ang), jnp.sin(ang)
    ones = jnp.ones((pos.shape[0], period - 2 * half), F32)
    cos_p = jnp.concatenate([cos, cos, ones], axis=1)
    sin_p = jnp.concatenate([-sin, sin, 0.0 * ones], axis=1)
    reps = LANES // period
    return jnp.tile(cos_p, (1, reps)), jnp.tile(sin_p, (1, reps))


def _rope_tile(x, cos, sin, half, period):
    lane = lax.broadcasted_iota(jnp.int32, x.shape, 1) % period
    partner = jnp.where(lane < half, pltpu.roll(x, LANES - half, axis=1), pltpu.roll(x, half, axis=1))
    return x * cos + partner * sin


def _dsa_proj_kernel(x_ref, w_ref, ca_ref, sa_ref, ci_ref, si_ref, g_ref, b_ref,
                     q_ref, k_ref, v_ref, vt_ref, iq_ref, ikw_ref, ik2_ref):
    half = x_ref.shape[0] // 2
    for r0 in (0, half):
        rows = pl.ds(r0, half)
        _dsa_proj_rows(x_ref.at[rows], w_ref, ca_ref.at[rows], sa_ref.at[rows], ci_ref.at[rows], si_ref.at[rows],
                       g_ref, b_ref, q_ref.at[rows], k_ref.at[rows], v_ref.at[rows], vt_ref.at[:, rows],
                       iq_ref.at[rows], ikw_ref.at[rows], ik2_ref.at[rows])


def _dsa_proj_rows(x_ref, w_ref, ca_ref, sa_ref, ci_ref, si_ref, g_ref, b_ref,
                   q_ref, k_ref, v_ref, vt_ref, iq_ref, ikw_ref, ik2_ref):
    x = x_ref[...].astype(BF16)
    ca, sa, ci, si = ca_ref[...], sa_ref[...], ci_ref[...], si_ref[...]
    q_off, k_off = 0, ATT_HEADS * ATT_HEAD_DIM
    v_off = k_off + ATT_KV_HEADS * ATT_HEAD_DIM
    iq_off = v_off + ATT_KV_HEADS * ATT_HEAD_DIM
    ik_off = iq_off + IDX_HEADS * IDX_HEAD_DIM

    def proj(off):
        return _dot(x, w_ref[:, off:off + LANES])

    for h in range(ATT_HEADS):
        t = _rope_tile(proj(q_off + h * LANES), ca, sa, ATT_ROT_HALF, LANES)
        q_ref[:, h * LANES:(h + 1) * LANES] = (t * (ATT_HEAD_DIM ** -0.5)).astype(q_ref.dtype)
    for h in range(ATT_KV_HEADS):
        k_ref[:, h * LANES:(h + 1) * LANES] = _rope_tile(proj(k_off + h * LANES), ca, sa, ATT_ROT_HALF, LANES)
        v = proj(v_off + h * LANES)
        v_ref[:, h * LANES:(h + 1) * LANES] = v
        vt_ref[h * LANES:(h + 1) * LANES, :] = v.T.astype(vt_ref.dtype)
    for h in range(IDX_HEADS * IDX_HEAD_DIM // LANES):
        t = _rope_tile(proj(iq_off + h * LANES), ci, si, IDX_ROT_HALF, IDX_HEAD_DIM)
        iq_ref[:, h * LANES:(h + 1) * LANES] = t.astype(iq_ref.dtype)
    t = proj(ik_off)
    lane = lax.broadcasted_iota(jnp.int32, t.shape, 1)
    is_ik = lane < IDX_HEAD_DIM
    mu = jnp.sum(jnp.where(is_ik, t, 0.0), axis=-1, keepdims=True) / IDX_HEAD_DIM
    tc = jnp.where(is_ik, t - mu, 0.0)
    var = jnp.sum(tc * tc, axis=-1, keepdims=True) / IDX_HEAD_DIM
    ik = _rope_tile(tc * lax.rsqrt(var + LN_EPS) * g_ref[...] + b_ref[...], ci, si, IDX_ROT_HALF, IDX_HEAD_DIM)
    ikw = jnp.where(is_ik, ik, t * (IDX_HEADS ** -0.5))
    ikw_ref[...] = ikw
    ik_only = jnp.where(is_ik, ik, 0.0)
    ik2_ref[...] = (ik_only + pltpu.roll(ik_only, IDX_HEAD_DIM, axis=1)).astype(ik2_ref.dtype)


def _dsa_proj(x, w, tables, ik_g, ik_b):
    rows, d = x.shape
    n = w.shape[1]
    kv = ATT_KV_HEADS * ATT_HEAD_DIM
    row_spec = lambda width: pl.BlockSpec((ROW_TILE, width), lambda i: (i, 0))
    const_spec = _resident
    return pl.pallas_call(
        _dsa_proj_kernel,
        grid=(rows // ROW_TILE,),
        in_specs=[row_spec(d), const_spec((d, n))] + [row_spec(LANES)] * 4 + [const_spec((1, LANES))] * 2,
        out_specs=[row_spec(ATT_HEADS * ATT_HEAD_DIM), row_spec(kv), row_spec(kv),
                   pl.BlockSpec((kv, ROW_TILE), lambda i: (0, i)),
                   row_spec(IDX_HEADS * IDX_HEAD_DIM), row_spec(LANES), row_spec(LANES)],
        out_shape=[jax.ShapeDtypeStruct((rows, ATT_HEADS * ATT_HEAD_DIM), BF16),
                   jax.ShapeDtypeStruct((rows, kv), F32),
                   jax.ShapeDtypeStruct((rows, kv), F32),
                   jax.ShapeDtypeStruct((kv, rows), BF16),
                   jax.ShapeDtypeStruct((rows, IDX_HEADS * IDX_HEAD_DIM), BF16),
                   jax.ShapeDtypeStruct((rows, LANES), F32),
                   jax.ShapeDtypeStruct((rows, LANES), BF16)],
        compiler_params=_cparams(("parallel",)),
        name="dsa_proj",
    )(x, w, *tables, ik_g, ik_b)


def _float_order_key(x):
    b = pltpu.bitcast(x, jnp.int32)
    return b ^ ((b >> 31) & jnp.int32(0x7FFFFFFF))


def _float_from_key(k):
    return pltpu.bitcast(k ^ ((k >> 31) & jnp.int32(0x7FFFFFFF)), F32)


VALUE_STEPS = 8
VALUE_ROUNDS = 2
KEY_STEPS = 4
KEY_ROUNDS = 32 // KEY_STEPS
F32_MAX = 3.4028234663852886e38


def _kth_largest_bounds(count_ge, max_below, target, amax):
    bound = jnp.minimum(2.0 * amax, F32_MAX)
    lo = _float_order_key(-bound)
    hi = _float_order_key(bound) + 1
    c_lo = jnp.full(target.shape, -1.0, F32)

    def step(carry, split_values):
        lo, hi, c_lo, lo_val = carry
        mid = (lo >> 1) + (hi >> 1) + (lo & hi & 1)
        if split_values:
            vmid = _float_order_key(_float_from_key(lo) * 0.5 + _float_from_key(hi) * 0.5)
            mid = jnp.where((vmid > lo) & (vmid < hi), vmid, mid)
        mid_val = _float_from_key(mid)
        cnt = count_ge(mid_val)
        ok = cnt >= target
        return (jnp.where(ok, mid, lo), jnp.where(ok, hi, mid), jnp.where(ok, cnt, c_lo),
                jnp.where(ok & (mid != lo), mid_val, lo_val))

    def pending(carry):
        lo, hi, c_lo, _ = carry
        return (c_lo != target) & (hi != lo + 1)

    def any_pending(carry):
        return jnp.max(jnp.where(pending(carry), 1.0, 0.0)) > 0.0

    def rounds(carry, n_rounds, n_steps, split_values):
        def body(state):
            it, carry = state
            for _ in range(n_steps):
                carry = step(carry, split_values)
            return it + 1, carry

        return lax.while_loop(lambda st: (st[0] < n_rounds) & any_pending(st[1]), body, (jnp.int32(0), carry))[1]

    def snap(carry):
        lo, hi, c_lo, lo_val = carry
        v = max_below(_float_from_key(hi))
        cnt = count_ge(v)
        hit = pending(carry) & (cnt >= target)
        vk = _float_order_key(jnp.where(v == 0.0, 0.0, v))
        return (jnp.where(hit, vk, lo), jnp.where(hit, vk + 1, hi), jnp.where(hit, cnt, c_lo),
                jnp.where(hit, v, lo_val))

    def snap_round(state):
        it, carry = state
        return it + 1, rounds(snap(carry), 1, KEY_STEPS, False)

    carry = rounds((lo, hi, c_lo, -bound), VALUE_ROUNDS, VALUE_STEPS, True)
    carry = lax.while_loop(lambda st: (st[0] < KEY_ROUNDS) & any_pending(st[1]), snap_round,
                           (jnp.int32(0), carry))[1]
    return carry[3]


def _dsa_prompt_kernel(topk, q_ref, iq_ref, ikw_ref, k_ref, vt_ref, ik2_ref, o_ref,
                       s_ref, xh_ref, qg_ref, sc_ref, acc_ref):
    i = pl.program_id(1)
    KT = KEY_TILE
    CH = KEY_CHUNK
    nc = i // (CH // KT) + 1
    row = lax.broadcasted_iota(jnp.int32, (CH, KT), 0)
    lane = lax.broadcasted_iota(jnp.int32, (KT, KT), 1)
    qpos = i * KT + lax.broadcasted_iota(jnp.int32, (1, KT), 1)
    GQ = ATT_GROUP * KT

    def chunk(c):
        return pl.ds(pl.multiple_of(c * CH, CH), CH)

    NACC = 8

    def fold_rows(x):
        return x.reshape(CH // (NACC * SUBLANES), NACC, SUBLANES, x.shape[-1])

    def unfold(x, op):
        return op(op(x, axis=0), axis=0, keepdims=True)

    for h in range(IDX_HEADS):
        tile = iq_ref[:, (h // 2) * LANES:(h // 2 + 1) * LANES]
        mine = (lane // IDX_HEAD_DIM) == (h % 2)
        xh_ref[h * KT:(h + 1) * KT, :] = jnp.where(mine, tile, jnp.zeros_like(tile))
    for g in range(ATT_KV_HEADS):
        for hq in range(ATT_GROUP):
            h = g * ATT_GROUP + hq
            qg_ref[g, hq * KT:(hq + 1) * KT, :] = q_ref[:, h * LANES:(h + 1) * LANES]
    w_rows = ikw_ref[...].T[IDX_HEAD_DIM:IDX_HEAD_DIM + IDX_HEADS, :]

    n_pairs = (nc + 1) // 2

    def pair(j):
        return 2 * j, jnp.minimum(2 * j + 1, nc - 1)

    def score_chunk(c, amax):
        d = _dot_nt(ik2_ref[chunk(c), :], xh_ref[...])
        acc = w_rows[0:1, :] * jnp.maximum(d[:, 0:KT], 0.0)
        for h in range(1, IDX_HEADS):
            acc = acc + w_rows[h:h + 1, :] * jnp.maximum(d[:, h * KT:(h + 1) * KT], 0.0)
        acc = acc * (IDX_HEAD_DIM ** -0.5)
        kpos = c * CH + row
        causal = kpos <= qpos
        meta = kpos < N_META
        s_ref[chunk(c), :] = jnp.where(causal, jnp.where(meta, jnp.inf, acc), NEG_INF)
        finite = jnp.where(causal & jnp.logical_not(meta), jnp.abs(acc), 0.0)
        return jnp.maximum(amax, jnp.max(fold_rows(finite), axis=0))

    def score_pair(j, amax):
        c0, c1 = pair(j)
        return score_chunk(c1, score_chunk(c0, amax))

    amax = unfold(lax.fori_loop(0, n_pairs, score_pair, jnp.zeros((NACC, SUBLANES, KT), F32)), jnp.max)

    target = jnp.minimum(qpos + 1, topk).astype(F32)

    def count_where(test):
        def body(c, acc):
            hit = jnp.where(test(s_ref[chunk(c), :]), 1.0, 0.0)
            return acc + jnp.sum(fold_rows(hit), axis=0)
        return unfold(lax.fori_loop(0, nc, body, jnp.zeros((NACC, SUBLANES, KT), F32)), jnp.sum)

    def count_ge(thr):
        return count_where(lambda t: t >= thr)

    def max_below(thr):
        def body(c, acc):
            t = s_ref[chunk(c), :]
            return jnp.maximum(acc, jnp.max(fold_rows(jnp.where(t < thr, t, NEG_INF)), axis=0))
        return unfold(lax.fori_loop(0, nc, body, jnp.full((NACC, SUBLANES, KT), NEG_INF, F32)), jnp.max)

    lo = _kth_largest_bounds(count_ge, max_below, target, amax)
    surplus = count_ge(lo) - target

    @pl.when(jnp.max(surplus) > 0.0)
    def _():
        need = target - count_where(lambda t: t > lo)
        rr = lax.broadcasted_iota(jnp.int32, (CH, CH), 0)
        cc = lax.broadcasted_iota(jnp.int32, (CH, CH), 1)
        tri = (cc <= rr).astype(BF16)

        def drop(c, carry):
            t = s_ref[chunk(c), :]
            tie = t == lo
            tie_f = jnp.where(tie, 1.0, 0.0)
            rank = carry + _dot(tri, tie_f.astype(BF16))
            s_ref[chunk(c), :] = jnp.where(tie & (rank > need), NEG_INF, t)
            return carry + jnp.sum(tie_f, axis=0, keepdims=True)

        lax.fori_loop(0, nc, drop, jnp.zeros((1, KT), F32))

    def bias_cols(c):
        b = jnp.where(s_ref[chunk(c), :] >= lo, 0.0, NEG_INF)
        return jnp.concatenate([b] * ATT_GROUP, axis=1)

    G = ATT_KV_HEADS

    def max_chunk(c, macc):
        bias = bias_cols(c)
        tops = []
        for g in range(G):
            sc = _dot_nt(k_ref[chunk(c), g * LANES:(g + 1) * LANES], qg_ref[g]) + bias
            sc_ref[chunk(c), g * GQ:(g + 1) * GQ] = sc
            tops.append(jnp.max(fold_rows(sc), axis=0))
        return jnp.maximum(macc, jnp.concatenate(tops, axis=-1))

    def max_pair(j, macc):
        c0, c1 = pair(j)
        return max_chunk(c1, max_chunk(c0, macc))

    macc = lax.fori_loop(0, n_pairs, max_pair, jnp.full((NACC, SUBLANES, G * GQ), NEG_INF, F32))
    m = unfold(macc, jnp.max)

    ones_rows = jnp.ones((2 * SUBLANES, CH), BF16)
    acc_ref[...] = jnp.zeros_like(acc_ref)

    def sum_chunk(c, shift):
        for g in range(G):
            p = jnp.exp(sc_ref[chunk(c), g * GQ:(g + 1) * GQ] - shift[:, g * GQ:(g + 1) * GQ]).astype(BF16)
            vt = jnp.concatenate([vt_ref[g * LANES:(g + 1) * LANES, chunk(c)], ones_rows], axis=0)
            acc_ref[g] += _dot(vt, p)

    def sum_pair(j, _):
        c0, c1 = pair(j)
        sum_chunk(c0, m)
        sum_chunk(c1, jnp.where(2 * j + 1 < nc, m, jnp.inf))
        return 0

    lax.fori_loop(0, n_pairs, sum_pair, 0)
    for g in range(G):
        acc = acc_ref[g]
        out_t = acc[:ATT_HEAD_DIM] / acc[ATT_HEAD_DIM:ATT_HEAD_DIM + 1]
        for hq in range(ATT_GROUP):
            h = g * ATT_GROUP + hq
            o_ref[:, h * LANES:(h + 1) * LANES] = out_t[:, hq * KT:(hq + 1) * KT].T.astype(o_ref.dtype)


def _dsa_prompt(q, iq, ikw, k, vt, ik2, n_batch, lp, topk):
    KT = KEY_TILE
    nq = lp // KT
    kv = ATT_KV_HEADS * ATT_HEAD_DIM
    qspec = lambda width: pl.BlockSpec((KT, width), lambda b, i: (b * nq + i, 0))
    return pl.pallas_call(
        functools.partial(_dsa_prompt_kernel, topk),
        grid=(n_batch, nq),
        in_specs=[qspec(ATT_HEADS * ATT_HEAD_DIM), qspec(IDX_HEADS * IDX_HEAD_DIM), qspec(LANES),
                  pl.BlockSpec((lp, kv), lambda b, i: (b, 0)),
                  pl.BlockSpec((kv, lp), lambda b, i: (0, b)),
                  pl.BlockSpec((lp, LANES), lambda b, i: (b, 0))],
        out_specs=qspec(ATT_HEADS * ATT_HEAD_DIM),
        out_shape=jax.ShapeDtypeStruct(q.shape, BF16),
        scratch_shapes=[pltpu.VMEM((lp, KT), F32),
                        pltpu.VMEM((IDX_HEADS * KT, LANES), BF16),
                        pltpu.VMEM((ATT_KV_HEADS, ATT_GROUP * KT, ATT_HEAD_DIM), BF16),
                        pltpu.VMEM((lp, ATT_HEADS * KT), F32),
                        pltpu.VMEM((ATT_KV_HEADS, ATT_HEAD_DIM + 2 * SUBLANES, ATT_GROUP * KT), F32)],
        compiler_params=_cparams(("parallel", "arbitrary")),
        name="dsa_prompt",
    )(q, iq, ikw, k, vt, ik2)


SAMPLE_GROUP = 16


def _dsa_sample_select_kernel(topk, n_pages, pt_ref, iq_ref, ikw_ref, *rest):
    del pt_ref
    page_refs = rest[:n_pages]
    sel_ref, lo_ref, stack_ref, st_ref, iq8_ref, ikw8_ref, newk_ref = rest[n_pages:]
    b = pl.program_id(0)
    j = b % SAMPLE_GROUP
    T = iq_ref.shape[1]
    R = 2 * T
    past = n_pages * PAGE
    n_keys = stack_ref.shape[1]
    GR = SAMPLE_GROUP * R

    iq8_ref[0:T, :] = iq_ref[0].astype(F32)
    iq8_ref[T:R, :] = iq_ref[0].astype(F32)
    ikw8_ref[0:T, :] = ikw_ref[0]
    ikw8_ref[T:R, :] = ikw_ref[0]
    lane = lax.broadcasted_iota(jnp.int32, (R, LANES), 1)
    ikw8 = ikw8_ref[...]
    xh = []
    for h in range(IDX_HEADS):
        tile = iq8_ref[:, (h // 2) * LANES:(h // 2 + 1) * LANES]
        if h % 2:
            tile = pltpu.roll(tile, IDX_HEAD_DIM, axis=1)
        xh.append(jnp.where(lane < IDX_HEAD_DIM, tile, 0.0))
    x_all = jnp.concatenate(xh, axis=0).astype(BF16)

    def index_scores(d):
        acc = jnp.zeros((R, LANES), F32)
        for h in range(IDX_HEADS):
            wcol = ikw8[:, IDX_HEAD_DIM + h:IDX_HEAD_DIM + h + 1]
            acc = acc + wcol * jnp.maximum(d[h * R:(h + 1) * R], 0.0)
        return acc * (IDX_HEAD_DIM ** -0.5)

    rows = pl.ds(pl.multiple_of(j * R, R), R)
    zeros64 = jnp.zeros((LANES - IDX_HEAD_DIM, PAGE), BF16)
    for p in range(n_pages):
        keys_t = jnp.concatenate([page_refs[p][0].astype(BF16), zeros64], axis=0)
        sc = index_scores(_dot(x_all, keys_t))
        if p == 0:
            sc = jnp.where(lane < N_META, jnp.inf, sc)
        stack_ref[rows, p * PAGE:(p + 1) * PAGE] = sc
    tmod = lax.broadcasted_iota(jnp.int32, (R, 1), 0) % T
    newk_ref[...] = jnp.zeros_like(newk_ref)
    newk_ref[0:T, :] = ikw_ref[0]
    nk = newk_ref[...]
    lane_k = lax.broadcasted_iota(jnp.int32, nk.shape, 1)
    sc_new = index_scores(_dot_nt(x_all, jnp.where(lane_k < IDX_HEAD_DIM, nk, 0.0).astype(BF16)))
    stack_ref[rows, past:n_keys] = jnp.where(lane <= tmod, sc_new, NEG_INF)

    @pl.when(j == SAMPLE_GROUP - 1)
    def _():
        st_ref[...] = stack_ref[...].T
        qlane = lax.broadcasted_iota(jnp.int32, (1, GR), 1)
        qpos = past + (qlane % R) % T
        target = jnp.minimum(qpos + 1, topk).astype(F32)
        NACC = 8

        def count_where(test):
            hit = jnp.where(test(st_ref[...]), 1.0, 0.0)
            part = jnp.sum(hit.reshape(n_keys // (NACC * SUBLANES), NACC, SUBLANES, GR), axis=0)
            return jnp.sum(jnp.sum(part, axis=0), axis=0, keepdims=True)

        def count_ge(thr):
            return count_where(lambda t: t >= thr)

        mag = jnp.abs(st_ref[...])
        mag = jnp.where(mag < jnp.inf, mag, 0.0)
        amax = jnp.max(jnp.max(mag.reshape(n_keys // SUBLANES, SUBLANES, GR), axis=0), axis=0, keepdims=True)
        def max_below(thr):
            t = st_ref[...]
            below = jnp.where(t < thr, t, NEG_INF).reshape(n_keys // (NACC * SUBLANES), NACC, SUBLANES, GR)
            return jnp.max(jnp.max(jnp.max(below, axis=0), axis=0), axis=0, keepdims=True)

        lo = _kth_largest_bounds(count_ge, max_below, target, amax)
        surplus = count_ge(lo) - target

        @pl.when(jnp.max(surplus) > 0.0)
        def _():
            need = target - count_where(lambda t: t > lo)
            rr = lax.broadcasted_iota(jnp.int32, (LANES, LANES), 0)
            cc = lax.broadcasted_iota(jnp.int32, (LANES, LANES), 1)
            tri = (cc <= rr).astype(BF16)
            carry = jnp.zeros((1, GR), F32)
            for kt in range(n_keys // LANES):
                t = st_ref[kt * LANES:(kt + 1) * LANES, :]
                tie = t == lo
                tie_f = jnp.where(tie, 1.0, 0.0)
                rank = carry + _dot(tri, tie_f.astype(BF16))
                st_ref[kt * LANES:(kt + 1) * LANES, :] = jnp.where(tie & (rank > need), NEG_INF, t)
                carry = carry + jnp.sum(tie_f, axis=0, keepdims=True)

        sel_ref[...] = st_ref[...].T
        lo_ref[...] = jnp.broadcast_to(lo, (GR, GR)).T


def _dsa_sample_select(page_table, iq, ikw, cache_ik, topk):
    nb, t, _ = iq.shape
    n_pages = page_table.shape[1]
    n_keys = n_pages * PAGE + LANES
    gr = SAMPLE_GROUP * 2 * t
    assert nb % SAMPLE_GROUP == 0 and gr == LANES
    tok = lambda width: pl.BlockSpec((1, t, width), lambda b, pt: (b, 0, 0))
    page = lambda p: pl.BlockSpec((1, IDX_HEAD_DIM, PAGE), lambda b, pt: (pt[b, p], 0, 0))
    grid_spec = pltpu.PrefetchScalarGridSpec(
        num_scalar_prefetch=1,
        grid=(nb,),
        in_specs=[tok(IDX_HEADS * IDX_HEAD_DIM), tok(LANES)] + [page(p) for p in range(n_pages)],
        out_specs=[pl.BlockSpec((gr, n_keys), lambda b, pt: (b // SAMPLE_GROUP, 0)),
                   pl.BlockSpec((gr, LANES), lambda b, pt: (b // SAMPLE_GROUP, 0))],
        scratch_shapes=[pltpu.VMEM((gr, n_keys), F32),
                        pltpu.VMEM((n_keys, gr), F32),
                        pltpu.VMEM((2 * t, IDX_HEADS * IDX_HEAD_DIM), F32),
                        pltpu.VMEM((2 * t, LANES), F32),
                        pltpu.VMEM((LANES, LANES), F32)])
    return pl.pallas_call(
        functools.partial(_dsa_sample_select_kernel, topk, n_pages),
        grid_spec=grid_spec,
        out_shape=[jax.ShapeDtypeStruct((nb * 2 * t, n_keys), F32),
                   jax.ShapeDtypeStruct((nb * 2 * t, LANES), F32)],
        compiler_params=_cparams(("arbitrary",)),
        name="dsa_sample_select",
    )(page_table, iq, ikw, *([cache_ik] * n_pages))


def _dsa_sample_attend_kernel(n_pages, pt_ref, q_ref, kn_ref, vn_ref, sel_ref, lo_ref, *rest):
    del pt_ref
    n_seq = q_ref.shape[0]
    pages = rest[:2 * n_seq * n_pages]
    o_ref, q8_ref, newk_ref, newv_ref = rest[2 * n_seq * n_pages:]
    R = 2 * q_ref.shape[1]
    for i in range(n_seq):
        one = pl.ds(i, 1)
        _dsa_sample_attend_seq(q_ref.at[one], kn_ref.at[one], vn_ref.at[one],
                               sel_ref.at[pl.ds(i * R, R)], lo_ref.at[pl.ds(i * R, R)],
                               pages[i * n_pages:(i + 1) * n_pages],
                               pages[(n_seq + i) * n_pages:(n_seq + i + 1) * n_pages],
                               o_ref.at[one], q8_ref.at[i], newk_ref.at[i], newv_ref.at[i])


def _dsa_sample_attend_seq(q_ref, kn_ref, vn_ref, sel_ref, lo_ref, k_refs, v_refs, o_ref,
                           q8_ref, newk_ref, newv_ref):
    T = q_ref.shape[1]
    R = 2 * T
    q8_ref[0:T, :] = q_ref[0].astype(F32)
    q8_ref[T:R, :] = q_ref[0].astype(F32)
    newk_ref[...] = jnp.zeros_like(newk_ref)
    newv_ref[...] = jnp.zeros_like(newv_ref)
    newk_ref[0:T, :] = kn_ref[0]
    newv_ref[0:T, :] = vn_ref[0]
    bias8 = jnp.where(sel_ref[...] >= lo_ref[:, 0:1], 0.0, NEG_INF)
    bias = jnp.concatenate([bias8] * ATT_GROUP, axis=0)
    for g in range(ATT_KV_HEADS):
        qg = jnp.concatenate([q8_ref[:, (g * ATT_GROUP + hq) * LANES:(g * ATT_GROUP + hq + 1) * LANES]
                              for hq in range(ATT_GROUP)], axis=0).astype(BF16)
        head_rows = pl.ds(g, PAGE, stride=ATT_KV_HEADS)
        k_all = jnp.concatenate([r[0, head_rows, :].astype(BF16) for r in k_refs]
                                + [newk_ref[:, g * LANES:(g + 1) * LANES].astype(BF16)], axis=0)
        v_all = jnp.concatenate([r[0, head_rows, :].astype(BF16) for r in v_refs]
                                + [newv_ref[:, g * LANES:(g + 1) * LANES].astype(BF16)], axis=0)
        sc = _dot_nt(qg, k_all) + bias
        m = jnp.max(sc, axis=1, keepdims=True)
        pr = jnp.exp(sc - m)
        out = _dot(pr.astype(BF16), v_all) / jnp.sum(pr, axis=1, keepdims=True)
        for hq in range(ATT_GROUP):
            h = g * ATT_GROUP + hq
            o_ref[0, :, h * LANES:(h + 1) * LANES] = out[hq * R:hq * R + T].astype(o_ref.dtype)


def _dsa_sample_attend(page_table, q, k_new, v_new, sel, lo, cache_k, cache_v):
    nb, t, _ = q.shape
    n_pages = page_table.shape[1]
    kv = ATT_KV_HEADS * ATT_HEAD_DIM
    n_keys = sel.shape[1]
    sq = SAMPLE_SEQS_PER_STEP
    assert nb % sq == 0
    tok = lambda width: pl.BlockSpec((sq, t, width), lambda b, pt: (b, 0, 0))
    page = lambda i, p: pl.BlockSpec((1, ATT_KV_HEADS * PAGE, ATT_HEAD_DIM),
                                     lambda b, pt: (pt[b * sq + i, p], 0, 0))
    pages = [page(i, p) for i in range(sq) for p in range(n_pages)]
    grid_spec = pltpu.PrefetchScalarGridSpec(
        num_scalar_prefetch=1,
        grid=(nb // sq,),
        in_specs=[tok(ATT_HEADS * ATT_HEAD_DIM), tok(kv), tok(kv),
                  pl.BlockSpec((sq * 2 * t, n_keys), lambda b, pt: (b, 0)),
                  pl.BlockSpec((sq * 2 * t, LANES), lambda b, pt: (b, 0))] + pages * 2,
        out_specs=tok(ATT_HEADS * ATT_HEAD_DIM),
        scratch_shapes=[pltpu.VMEM((sq, 2 * t, ATT_HEADS * ATT_HEAD_DIM), F32),
                        pltpu.VMEM((sq, LANES, kv), F32),
                        pltpu.VMEM((sq, LANES, kv), F32)])
    return pl.pallas_call(
        functools.partial(_dsa_sample_attend_kernel, n_pages),
        grid_spec=grid_spec,
        out_shape=jax.ShapeDtypeStruct((nb, t, ATT_HEADS * ATT_HEAD_DIM), BF16),
        compiler_params=_cparams(("parallel",)),
        name="dsa_sample_attend",
    )(page_table, q, k_new, v_new, sel, lo, *([cache_k] * (sq * n_pages)), *([cache_v] * (sq * n_pages)))


def kernel(x_prompt, x_sample, state_gdn, state_gdn_conv, cache_k, cache_v, cache_idx_k, page_table,
           meta_tokens, ln1_g, ln1_b, ln2_g, ln2_b, mlp_w1, mlp_w2,
           gdn_w_in, gdn_conv_w, gdn_a_log, gdn_dt_bias, gdn_norm_w, gdn_w_out,
           dsa_w_in, dsa_ik_norm_g, dsa_ik_norm_b, dsa_w_o):
    nb, seq, d = x_prompt.shape
    ns, ts, _ = x_sample.shape
    n_tok = N_META + seq
    lp = -(-n_tok // KEY_TILE) * KEY_TILE
    n_prompt_rows = nb * lp
    rows = n_prompt_rows + ns * ts
    assert rows % ROW_TILE == 0 and lp % GDN_CHUNK == 0 and lp % KEY_CHUNK == 0 and d == D_MODEL
    kvd = ATT_KV_HEADS * ATT_HEAD_DIM

    meta = meta_tokens.astype(x_prompt.dtype)
    pad = jnp.zeros((lp - n_tok, d), x_prompt.dtype)
    h = jnp.concatenate([piece for b in range(nb) for piece in (meta, x_prompt[b], pad)]
                        + [x_sample.reshape(ns * ts, d)], 0)

    def row2(x):
        return x.reshape(1, -1)

    def prompt_rows(x, width, first=0):
        rows_b = [x[b * lp + first:b * lp + n_tok, :width] for b in range(nb)]
        return jnp.concatenate(rows_b, 0).reshape(nb, n_tok - first, width)

    def with_sample_rows(a_prompt, a_sample):
        return lax.dynamic_update_slice(a_prompt, a_sample.reshape(ns * ts, -1), (n_prompt_rows, 0))

    w_in = gdn_w_in[0]
    split = GDN_CONV_DIM + GDN_VALUE_DIM
    n_gate = 2 * GDN_V_HEADS
    w_cat = jnp.concatenate([w_in, jnp.zeros((d, LANES - n_gate), w_in.dtype)], 1).astype(BF16)
    proj, ba = _gdn_inproj(h, w_cat, split)
    decay_lanes = lambda v: jnp.pad(row2(v), ((0, 0), (GDN_V_HEADS, LANES - n_gate)))
    a_p, gdn_state_prompt = _gdn_prompt(proj, ba, gdn_conv_w[0], decay_lanes(gdn_a_log[0]),
                                        decay_lanes(gdn_dt_bias[0]), row2(gdn_norm_w[0]), nb, lp, n_tok)
    proj_s = proj[n_prompt_rows:].reshape(ns, ts, split)
    mixed_s = proj_s[:, :, :GDN_CONV_DIM]
    a_s, gdn_state_sample = _gdn_sample(state_gdn_conv[0], mixed_s, proj_s[:, :, GDN_CONV_DIM:],
                                        ba[n_prompt_rows:, :n_gate].reshape(ns, ts, n_gate), gdn_conv_w[0],
                                        row2(gdn_a_log[0]), row2(gdn_dt_bias[0]), row2(gdn_norm_w[0]),
                                        state_gdn[0])
    def tail_weights(i, w_out):
        return (w_out.astype(BF16), row2(ln1_g[i]), row2(ln1_b[i]),
                mlp_w1[i].astype(BF16), mlp_w2[i].astype(BF16), row2(ln2_g[i]), row2(ln2_b[i]))

    h = _block_tail(with_sample_rows(a_p, a_s), h, tail_weights(0, gdn_w_out[0]))
    keep = GDN_CONV_WIDTH - 1
    gdn_conv_prompt = prompt_rows(proj, GDN_CONV_DIM, first=n_tok - keep)
    gdn_conv_sample = jnp.concatenate([state_gdn_conv[0], mixed_s], 1)[:, -keep:]

    past = page_table.shape[1] * PAGE
    pos = jnp.concatenate([jnp.tile(jnp.arange(lp, dtype=jnp.int32), nb),
                           jnp.tile(past + jnp.arange(ts, dtype=jnp.int32), ns)])
    tables = _rope_tables(pos, ATT_ROT_HALF, LANES) + _rope_tables(pos, IDX_ROT_HALF, IDX_HEAD_DIM)
    n_in = dsa_w_in.shape[2]
    w_dsa = jnp.pad(dsa_w_in[0], ((0, 0), (0, -n_in % LANES))).astype(BF16)
    pad_lanes = lambda v: row2(jnp.pad(v, (0, LANES - v.shape[0])))
    q, k, v, vt, iq, ikw, ik2 = _dsa_proj(h, w_dsa, tables, pad_lanes(dsa_ik_norm_g[0]),
                                          pad_lanes(dsa_ik_norm_b[0]))
    o_p = _dsa_prompt(q, iq, ikw, k.astype(BF16), vt, ik2, nb, lp, min(TOPK_MAX, (n_tok - N_META) // 4))
    smp = lambda x: x[n_prompt_rows:].reshape(ns, ts, -1)
    n_pool = cache_k.shape[1]
    sel, lo = _dsa_sample_select(page_table, smp(iq), smp(ikw), jnp.swapaxes(cache_idx_k[0], 1, 2),
                                 min(TOPK_MAX, (past + ts) // 4))
    rows_kh = lambda c: c[0].reshape(n_pool, PAGE * ATT_KV_HEADS, ATT_HEAD_DIM)
    o_s = _dsa_sample_attend(page_table, smp(q), smp(k), smp(v), sel, lo, rows_kh(cache_k), rows_kh(cache_v))
    last = tail_weights(1, dsa_w_o[0])
    y_prompt = _block_tail_sequences(o_p, h, last, nb, lp, N_META, seq)
    y_sample = _block_tail(o_s.reshape(ns * ts, -1), h[n_prompt_rows:], last).reshape(ns, ts, d)

    heads = lambda x: x.reshape(x.shape[:-1] + (ATT_KV_HEADS, ATT_HEAD_DIM))
    return (y_prompt, y_sample,
            gdn_state_prompt[None], gdn_conv_prompt[None], gdn_state_sample[None], gdn_conv_sample[None],
            heads(prompt_rows(k, kvd))[None], heads(prompt_rows(v, kvd))[None],
            prompt_rows(ikw, IDX_HEAD_DIM)[None],
            heads(smp(k))[None], heads(smp(v))[None], smp(ikw)[:, :, :IDX_HEAD_DIM][None])
```
